```python
import jax, jax.numpy as jnp
from jax import lax
import numpy as np

D_MODEL = 1024
BATCH = 2
SEQ = 8192
DEPTH = 2

GRID_W = 64
RMS_EPS = 1e-6

ML_WIDTH = D_MODEL
ML_HEADS = 4
ML_HEAD_DIM = ML_WIDTH // ML_HEADS
ML_CHUNK = 64

NA_WIDTH = D_MODEL
NA_HEADS = 16
NA_HEAD_DIM = NA_WIDTH // NA_HEADS
NA_KH = 8
NA_KW = 16

GQ_WIDTH = D_MODEL
GQ_HEADS = 8
GQ_HEAD_DIM = GQ_WIDTH // GQ_HEADS
GQ_KV_HEADS = 2
GQ_KV_WIDTH = GQ_KV_HEADS * GQ_HEAD_DIM
GQ_BLOCK = 128
ROPE_THETA = 10000.0
ROPE_HALF = GQ_HEAD_DIM // 4

CV_WIDTH = D_MODEL
CV_K = 3

EVEN_SPLIT = (ML_WIDTH,) * 5 + (4 * ML_HEADS,) + (NA_WIDTH,) * 4
ODD_SPLIT = (GQ_WIDTH, GQ_KV_WIDTH, GQ_KV_WIDTH, GQ_WIDTH) + (CV_WIDTH,) * 4
EVEN_IN = sum(EVEN_SPLIT)
ODD_IN = sum(ODD_SPLIT)
MIX_EVEN = ML_WIDTH + NA_WIDTH
MIX_ODD = GQ_WIDTH + CV_WIDTH

kernel_name = "hybrid_mlstm_natten_gqa_shortconv_encoder"


def _split(a, sizes):
    return jnp.split(a, np.cumsum(sizes)[:-1].tolist(), axis=-1)


def rmsnorm(x, g):
    x32 = x.astype(jnp.float32)
    y = x32 * lax.rsqrt(jnp.mean(x32 * x32, axis=-1, keepdims=True) + RMS_EPS)
    return (y * g.astype(jnp.float32)).astype(x.dtype)


def mlstm_scan(q, k, v, ig, lf):
    B, H, S, dh = q.shape
    L = ML_CHUNK
    nc = S // L

    def chunks(a):
        return jnp.moveaxis(a.reshape(B, H, nc, L, *a.shape[3:]), 2, 0)

    tril = jnp.tril(jnp.ones((L, L), dtype=bool))

    def step(carry, xs):
        c_mat, n_vec, m = carry
        qc, kc, vc, igc, lfc = xs
        b = jnp.cumsum(lfc, axis=-1)
        g = b[..., -1]
        d = jnp.where(tril, b[..., :, None] - b[..., None, :] + igc[..., None, :], -jnp.inf)
        inter = b + m[..., None]
        m_t = jnp.maximum(inter, jnp.max(d, axis=-1))
        w = jnp.exp(d - m_t[..., None]) * jnp.einsum('bhtd,bhsd->bhts', qc, kc)
        decay = jnp.exp(inter - m_t)
        num = decay[..., None] * jnp.einsum('bhvk,bhtk->bhtv', c_mat, qc) + jnp.einsum('bhts,bhsv->bhtv', w, vc)
        den = decay * jnp.einsum('bhk,bhtk->bht', n_vec, qc) + jnp.sum(w, axis=-1)
        h = num / jnp.maximum(jnp.abs(den), jnp.exp(-m_t))[..., None]
        a = g[..., None] - b + igc
        m_new = jnp.maximum(g + m, jnp.max(a, axis=-1))
        carry_scale = jnp.exp(g + m - m_new)
        wk = jnp.exp(a - m_new[..., None])
        c_new = carry_scale[..., None, None] * c_mat + jnp.einsum('bhsv,bhsk->bhvk', wk[..., None] * vc, kc)
        n_new = carry_scale[..., None] * n_vec + jnp.einsum('bhs,bhsk->bhk', wk, kc)
        return (c_new, n_new, m_new), h

    f32 = jnp.float32
    init = (jnp.zeros((B, H, dh, dh), f32), jnp.zeros((B, H, dh), f32), jnp.zeros((B, H), f32))
    _, h = lax.scan(step, init, (chunks(q), chunks(k), chunks(v), chunks(ig), chunks(lf)))
    return jnp.moveaxis(h, 0, 2).reshape(B, H, S, dh)


def mlstm_branch(q, k, v, o, z, gates, gate_b, norm_g):
    B, S, _ = q.shape
    f32 = jnp.float32

    def heads(a):
        return a.reshape(B, S, ML_HEADS, ML_HEAD_DIM).transpose(0, 2, 1, 3).astype(f32)

    qh, kh, vh = heads(q), heads(k) * (ML_HEAD_DIM ** -0.5), heads(v)
    g = (gates.astype(f32) + gate_b.astype(f32)).reshape(B, S, 4, ML_HEADS).transpose(2, 0, 3, 1)
    ig_f, lf_f = g[0], jax.nn.log_sigmoid(g[1])
    ig_b, lf_b = g[2], jax.nn.log_sigmoid(g[3])
    h_f = mlstm_scan(qh, kh, vh, ig_f, lf_f)
    rev = lambda a: jnp.flip(a, axis=2)
    h_b = rev(mlstm_scan(rev(qh), rev(kh), rev(vh), rev(ig_b), rev(lf_b)))
    h = h_f + h_b
    h = h * lax.rsqrt(jnp.mean(h * h, axis=-1, keepdims=True) + RMS_EPS)
    h = h.transpose(0, 2, 1, 3).reshape(B, S, ML_WIDTH) * norm_g.astype(f32)
    out = h * jax.nn.sigmoid(o.astype(f32)) * jax.nn.silu(z.astype(f32))
    return out.astype(q.dtype)


def na_branch(q, k, v, z, rpb):
    B, S, _ = q.shape
    rows = S // GRID_W
    kh = min(NA_KH, rows)

    def grid(a):
        return a.reshape(B, rows, GRID_W, NA_HEADS, NA_HEAD_DIM).transpose(0, 3, 1, 2, 4)

    qg, kg, vg = grid(q) * (NA_HEAD_DIM ** -0.5), grid(k), grid(v)
    c = jnp.arange(GRID_W)
    c0 = jnp.clip(c - NA_KW // 2, 0, GRID_W - NA_KW)
    col_idx = c0[:, None] + jnp.arange(NA_KW)[None, :]
    dc = col_idx - c[:, None] + (NA_KW - 1)

    def row(r):
        r0 = jnp.clip(r - kh // 2, 0, rows - kh)
        kr = lax.dynamic_slice_in_dim(kg, r0, kh, axis=2)
        vr = lax.dynamic_slice_in_dim(vg, r0, kh, axis=2)
        kw = jnp.take(kr, col_idx, axis=3)
        vw = jnp.take(vr, col_idx, axis=3)
        qr = lax.dynamic_index_in_dim(qg, r, axis=2, keepdims=False)
        s = jnp.einsum('bhcd,bhicjd->bhcij', qr, kw).astype(jnp.float32)
        dr = r0 + jnp.arange(kh) - r + (NA_KH - 1)
        bias = rpb[:, dr[None, :, None], dc[:, None, :]]
        s = s + bias.astype(jnp.float32)
        p = jax.nn.softmax(s.reshape(B, NA_HEADS, GRID_W, kh * NA_KW), axis=-1).reshape(s.shape)
        return jnp.einsum('bhcij,bhicjd->bhcd', p.astype(vw.dtype), vw)

    out = lax.map(row, jnp.arange(rows))
    out = out.transpose(1, 0, 3, 2, 4).reshape(B, S, NA_WIDTH)
    return out * jax.nn.silu(z)


def axial_rope_tables(S):
    t = jnp.arange(S)
    pos = jnp.stack([t // GRID_W, t % GRID_W], axis=-1).astype(jnp.float32)
    inv = ROPE_THETA ** (-jnp.arange(ROPE_HALF, dtype=jnp.float32) / ROPE_HALF)
    ang = pos[:, :, None] * inv
    return jnp.cos(ang)[:, None], jnp.sin(ang)[:, None]


def apply_rope(x, cos, sin):
    xs = x.astype(jnp.float32).reshape(*x.shape[:-1], 2, 2, ROPE_HALF)
    x1, x2 = xs[..., 0, :], xs[..., 1, :]
    out = jnp.stack([x1 * cos - x2 * sin, x1 * sin + x2 * cos], axis=-2)
    return out.reshape(x.shape).astype(x.dtype)


def gqa_branch(q, k, v, z, qn_g, kn_g):
    B, S, _ = q.shape
    qh = rmsnorm(q.reshape(B, S, GQ_HEADS, GQ_HEAD_DIM), qn_g)
    kh = rmsnorm(k.reshape(B, S, GQ_KV_HEADS, GQ_HEAD_DIM), kn_g)
    vh = v.reshape(B, S, GQ_KV_HEADS, GQ_HEAD_DIM)
    cos, sin = axial_rope_tables(S)
    qh, kh = apply_rope(qh, cos, sin), apply_rope(kh, cos, sin)
    grp = GQ_HEADS // GQ_KV_HEADS
    nb = S // GQ_BLOCK
    qb = qh.reshape(B, nb, GQ_BLOCK, GQ_KV_HEADS, grp, GQ_HEAD_DIM).transpose(1, 0, 3, 4, 2, 5)
    qb = qb * (GQ_HEAD_DIM ** -0.5)
    kt = kh.transpose(0, 2, 1, 3)
    vt = vh.transpose(0, 2, 1, 3)

    def block(qblk):
        s = jnp.einsum('bkgqd,bksd->bkgqs', qblk, kt).astype(jnp.float32)
        p = jax.nn.softmax(s, axis=-1).astype(vt.dtype)
        return jnp.einsum('bkgqs,bksd->bkgqd', p, vt)

    out = lax.map(block, qb)
    out = out.transpose(1, 0, 4, 2, 3, 5).reshape(B, S, GQ_WIDTH)
    return out * jax.nn.silu(z)


def conv_branch(bg, cg, xin, z, w, b):
    xc = cg * xin
    xp = jnp.pad(xc, ((0, 0), (1, 1), (0, 0)))
    y = xp[:, :-2] * w[0] + xp[:, 1:-1] * w[1] + xp[:, 2:] * w[2] + b
    return bg * y * jax.nn.silu(z)


def even_mixer(u, w_in, gate_b, w_out, ml_norm_g, na_rpb):
    p = u @ w_in
    ml_q, ml_k, ml_v, ml_o, ml_z, ml_gates, na_q, na_k, na_v, na_z = _split(p, EVEN_SPLIT)
    a = mlstm_branch(ml_q, ml_k, ml_v, ml_o, ml_z, ml_gates, gate_b, ml_norm_g)
    bb = na_branch(na_q, na_k, na_v, na_z, na_rpb)
    return jnp.concatenate([a, bb], axis=-1) @ w_out


def odd_mixer(u, w_in, w_out, q_norm_g, k_norm_g, conv_w, conv_b):
    p = u @ w_in
    g_q, g_k, g_v, g_z, c_b, c_c, c_x, c_z = _split(p, ODD_SPLIT)
    a = gqa_branch(g_q, g_k, g_v, g_z, q_norm_g, k_norm_g)
    d = conv_branch(c_b, c_c, c_x, c_z, conv_w, conv_b)
    return jnp.concatenate([a, d], axis=-1) @ w_out


def setup_inputs(seed: int = 0) -> dict:
    key = jax.random.key(seed)
    ks = jax.random.split(key, 14)
    n_even = (DEPTH + 1) // 2
    n_odd = DEPTH // 2
    f32 = jnp.float32

    def nrm(k, shape, scale):
        return jax.random.normal(k, shape, f32) * scale

    f_base = jnp.linspace(3.0, 6.0, ML_HEADS, dtype=f32)
    z_h = jnp.zeros((ML_HEADS,), f32)
    gate_base = jnp.concatenate([z_h, f_base, z_h, f_base])
    return {
        "x": nrm(ks[0], (BATCH, SEQ, D_MODEL), 1.0),
        "norm_g": 1.0 + nrm(ks[1], (DEPTH, D_MODEL), 0.02),
        "final_g": 1.0 + nrm(ks[2], (D_MODEL,), 0.02),
        "ev_w_in": nrm(ks[3], (n_even, D_MODEL, EVEN_IN), D_MODEL ** -0.5),
        "ev_gate_b": gate_base + nrm(ks[4], (n_even, 4 * ML_HEADS), 0.1),
        "ev_w_out": nrm(ks[5], (n_even, MIX_EVEN, D_MODEL), MIX_EVEN ** -0.5),
        "ev_ml_norm_g": 1.0 + nrm(ks[6], (n_even, ML_WIDTH), 0.02),
        "ev_na_rpb": nrm(ks[7], (n_even, NA_HEADS, 2 * NA_KH - 1, 2 * NA_KW - 1), 0.1),
        "od_w_in": nrm(ks[8], (n_odd, D_MODEL, ODD_IN), D_MODEL ** -0.5),
        "od_w_out": nrm(ks[9], (n_odd, MIX_ODD, D_MODEL), MIX_ODD ** -0.5),
        "od_q_norm_g": 1.0 + nrm(ks[10], (n_odd, GQ_HEAD_DIM), 0.02),
        "od_k_norm_g": 1.0 + nrm(ks[11], (n_odd, GQ_HEAD_DIM), 0.02),
        "od_conv_w": nrm(ks[12], (n_odd, CV_K, CV_WIDTH), CV_K ** -0.5),
        "od_conv_b": nrm(ks[13], (n_odd, CV_WIDTH), 0.02),
    }


def reference(x, norm_g, final_g, ev_w_in, ev_gate_b, ev_w_out, ev_ml_norm_g, ev_na_rpb,
              od_w_in, od_w_out, od_q_norm_g, od_k_norm_g, od_conv_w, od_conv_b):
    h = x
    for layer in range(DEPTH):
        u = rmsnorm(h, norm_g[layer])
        j = layer // 2
        if layer % 2 == 0:
            y = even_mixer(u, ev_w_in[j], ev_gate_b[j], ev_w_out[j], ev_ml_norm_g[j], ev_na_rpb[j])
        else:
            y = odd_mixer(u, od_w_in[j], od_w_out[j], od_q_norm_g[j], od_k_norm_g[j],
                          od_conv_w[j], od_conv_b[j])
        h = h + y.astype(h.dtype)
    return rmsnorm(h, final_g)
```

```python
import functools

import jax
import jax.numpy as jnp
import numpy as np
from jax import lax
from jax.experimental import pallas as pl
from jax.experimental.pallas import tpu as pltpu

F32 = jnp.float32
BF16 = jnp.bfloat16

D_MODEL = 1024
GRID_W = 64
RMS_EPS = 1e-6

ML_HEADS = 4
ML_HEAD_DIM = 256
ML_CHUNK = 256

NA_HEADS = 16
NA_HEAD_DIM = 64
NA_KH = 8
NA_KW = 16
NA_ROWS_PER_STEP = 8

GQ_HEADS = 8
GQ_HEAD_DIM = 128
GQ_KV_HEADS = 2
GQ_GROUP = GQ_HEADS // GQ_KV_HEADS
ROPE_THETA = 10000.0
ROPE_HALF = GQ_HEAD_DIM // 4
GQ_BLOCK_Q = 128
GQ_BLOCK_K = 512

CV_K = 3

LANES = 128
SUBLANES = 8
MASK_VALUE = -1e30

VMEM_LIMIT = 56 * 1024 * 1024


def _params(*sem):
    return pltpu.CompilerParams(dimension_semantics=sem, vmem_limit_bytes=VMEM_LIMIT)


def _sigmoid(x):
    return 1.0 / (1.0 + jnp.exp(-x))


def _silu(x):
    return x * _sigmoid(x)


def _log_sigmoid(x):
    return jnp.minimum(x, 0.0) - jnp.log1p(jnp.exp(-jnp.abs(x)))


def _rms(x, eps=RMS_EPS):
    return x * lax.rsqrt(jnp.mean(x * x, axis=-1, keepdims=True) + eps)


def _dot(a, b):
    return jnp.dot(a, b, preferred_element_type=F32)


def _dot_nt(a, b):
    return lax.dot_general(a, b, (((1,), (1,)), ((), ())), preferred_element_type=F32)


def _dot_tn(a, b):
    return lax.dot_general(a, b, (((0,), (0,)), ((), ())), preferred_element_type=F32)


def _even_in_kernel(x_ref, g_ref, w_ref, wg_ref, o_ref, og_ref, u_ref):
    @pl.when(pl.program_id(1) == 0)
    def _():
        u = (_rms(x_ref[...]) * g_ref[...]).astype(BF16)
        u_ref[...] = u
        og_ref[...] = _dot(u, wg_ref[...])

    o_ref[...] = _dot(u_ref[...], w_ref[...]).astype(o_ref.dtype)


def _even_in_proj(x, g, w, wg, *, tm, tn):
    t, d = x.shape
    n = w.shape[1]
    return pl.pallas_call(
        _even_in_kernel,
        grid=(t // tm, n // tn),
        in_specs=[
            pl.BlockSpec((tm, d), lambda i, j: (i, 0)),
            pl.BlockSpec((1, d), lambda i, j: (0, 0)),
            pl.BlockSpec((d, tn), lambda i, j: (0, j)),
            pl.BlockSpec((d, LANES), lambda i, j: (0, 0)),
        ],
        out_specs=[
            pl.BlockSpec((tm, tn), lambda i, j: (i, j)),
            pl.BlockSpec((tm, LANES), lambda i, j: (i, 0)),
        ],
        out_shape=[
            jax.ShapeDtypeStruct((t, n), BF16),
            jax.ShapeDtypeStruct((t, LANES), F32),
        ],
        scratch_shapes=[pltpu.VMEM((tm, d), BF16)],
        compiler_params=_params("parallel", "arbitrary"),
        name="even_in_proj",
    )(x, g, w, wg)


def _rope_norm(acc, gain, cos, sin, scale):
    nh = acc.shape[1] // GQ_HEAD_DIM
    y = jnp.concatenate(
        [_rms(acc[:, h * GQ_HEAD_DIM:(h + 1) * GQ_HEAD_DIM]) for h in range(nh)], axis=-1) * gain
    width = y.shape[1]
    lane = lax.broadcasted_iota(jnp.int32, y.shape, 1)
    first_half = (lane % (2 * ROPE_HALF)) < ROPE_HALF
    partner = jnp.where(first_half, pltpu.roll(y, width - ROPE_HALF, 1), pltpu.roll(y, ROPE_HALF, 1))
    return (y * cos + partner * sin) * scale


def _odd_in_kernel(x_ref, g_ref, w_ref, qg_ref, kg_ref, cos_ref, sin_ref, o_ref, u_ref, *, n_q_tiles, kv_tile):
    j = pl.program_id(1)

    @pl.when(j == 0)
    def _():
        u_ref[...] = (_rms(x_ref[...]) * g_ref[...]).astype(BF16)

    acc = _dot(u_ref[...], w_ref[...])
    is_q = j < n_q_tiles
    is_kv = j == kv_tile

    @pl.when(is_q)
    def _():
        o_ref[...] = _rope_norm(acc, qg_ref[...], cos_ref[...], sin_ref[...],
                                GQ_HEAD_DIM ** -0.5).astype(o_ref.dtype)

    @pl.when(is_kv)
    def _():
        kw = GQ_KV_HEADS * GQ_HEAD_DIM
        k = _rope_norm(acc[:, :kw], kg_ref[...], cos_ref[:, :kw], sin_ref[:, :kw], 1.0)
        o_ref[...] = jnp.concatenate([k, acc[:, kw:]], axis=-1).astype(o_ref.dtype)

    @pl.when(jnp.logical_not(jnp.logical_or(is_q, is_kv)))
    def _():
        o_ref[...] = acc.astype(o_ref.dtype)


def _odd_in_proj(x, g, w, qg, kg, cos, sin, *, tm, tn, seq):
    t, d = x.shape
    n = w.shape[1]
    n_q_tiles = (GQ_HEADS * GQ_HEAD_DIM) // tn
    kv_tile = n // tn - 1
    sb = seq // tm
    kern = functools.partial(_odd_in_kernel, n_q_tiles=n_q_tiles, kv_tile=kv_tile)
    return pl.pallas_call(
        kern,
        grid=(t // tm, n // tn),
        in_specs=[
            pl.BlockSpec((tm, d), lambda i, j: (i, 0)),
            pl.BlockSpec((1, d), lambda i, j: (0, 0)),
            pl.BlockSpec((d, tn), lambda i, j: (0, j)),
            pl.BlockSpec((1, tn), lambda i, j: (0, 0)),
            pl.BlockSpec((1, GQ_KV_HEADS * GQ_HEAD_DIM), lambda i, j: (0, 0)),
            pl.BlockSpec((tm, tn), lambda i, j: (i % sb, 0)),
            pl.BlockSpec((tm, tn), lambda i, j: (i % sb, 0)),
        ],
        out_specs=pl.BlockSpec((tm, tn), lambda i, j: (i, j)),
        out_shape=jax.ShapeDtypeStruct((t, n), BF16),
        scratch_shapes=[pltpu.VMEM((tm, d), BF16)],
        compiler_params=_params("parallel", "arbitrary"),
        name="odd_in_proj",
    )(x, g, w, qg, kg, cos, sin)


def _mlstm_kernel(qf_ref, kf_ref, vf_ref, gf_ref, qb_ref, kb_ref, vb_ref, gb_ref, bias_ref,
                  hf_ref, hb_ref, c_ref, n_ref, m_ref):
    L = qf_ref.shape[0]
    dh = ML_HEAD_DIM
    k_scale = dh ** -0.5

    @pl.when(pl.program_id(1) == 0)
    def _():
        c_ref[...] = jnp.zeros_like(c_ref)
        n_ref[...] = jnp.zeros_like(n_ref)
        m_ref[...] = jnp.zeros_like(m_ref)

    row = lax.broadcasted_iota(jnp.int32, (L, L), 0)
    col = lax.broadcasted_iota(jnp.int32, (L, L), 1)
    lower = row >= col
    upper = row <= col
    lane = lax.broadcasted_iota(jnp.int32, (L, LANES), 1)
    is_forget = ((lane >= ML_HEADS) & (lane < 2 * ML_HEADS)) | ((lane >= 3 * ML_HEADS) & (lane < 4 * ML_HEADS))

    def gate_tables(g_ref, tri):
        g = g_ref[...] + bias_ref[...]
        lf = jnp.where(is_forget, _log_sigmoid(g), 0.0)
        cum = jnp.dot(tri.astype(F32), lf, preferred_element_type=F32, precision=lax.Precision.HIGHEST)
        tab = jnp.where(is_forget, cum, g)
        return tab, tab.T

    tab_f, tab_ft = gate_tables(gf_ref, lower)
    tab_b, tab_bt = gate_tables(gb_ref, upper)

    def chain(idx, q, k, v, bcol, igcol, brow, igrow, g_tot, valid):
        c = c_ref[idx]
        n = n_ref[idx][0:1, :]
        m_prev = m_ref[idx][0:1, 0:1]
        d = jnp.where(valid, bcol - brow + igrow, MASK_VALUE)
        inter = bcol + m_prev
        m_t = jnp.maximum(inter, jnp.max(d, axis=-1, keepdims=True))
        w = jnp.exp(d - m_t) * (_dot_nt(q, k) * k_scale)
        decay = jnp.exp(inter - m_t)
        qf = q.astype(F32)
        num = decay * _dot(q, c.astype(BF16)) + _dot(w.astype(BF16), v)
        den = decay * jnp.sum(qf * n, axis=-1, keepdims=True) + jnp.sum(w, axis=-1, keepdims=True)
        h = num / jnp.maximum(jnp.abs(den), jnp.exp(-m_t))
        a = g_tot - bcol + igcol
        m_new = jnp.maximum(g_tot + m_prev, jnp.max(a, axis=0, keepdims=True))
        carry = jnp.exp(g_tot + m_prev - m_new)
        wk = jnp.exp(a - m_new)
        wkv = (wk * v.astype(F32)).astype(BF16)
        c_ref[idx] = carry * c + _dot_tn(k, wkv) * k_scale
        n_new = carry * n + jnp.sum(wk * k.astype(F32), axis=0, keepdims=True) * k_scale
        n_ref[idx] = jnp.broadcast_to(n_new, n_ref.shape[1:])
        m_ref[idx] = jnp.broadcast_to(m_new, m_ref.shape[1:])
        return h

    for hd in range(ML_HEADS):
        sl = slice(hd * dh, (hd + 1) * dh)
        fi, ff = hd, ML_HEADS + hd
        h_f = chain(hd, qf_ref[:, sl], kf_ref[:, sl], vf_ref[:, sl],
                    tab_f[:, ff:ff + 1], tab_f[:, fi:fi + 1], tab_ft[ff:ff + 1, :], tab_ft[fi:fi + 1, :],
                    tab_f[L - 1:L, ff:ff + 1], lower)
        hf_ref[:, sl] = h_f.astype(hf_ref.dtype)
        bi, bf = 2 * ML_HEADS + hd, 3 * ML_HEADS + hd
        h_b = chain(ML_HEADS + hd, qb_ref[:, sl], kb_ref[:, sl], vb_ref[:, sl],
                    tab_b[:, bf:bf + 1], tab_b[:, bi:bi + 1], tab_bt[bf:bf + 1, :], tab_bt[bi:bi + 1, :],
                    tab_b[0:1, bf:bf + 1], upper)
        hb_ref[:, sl] = h_b.astype(hb_ref.dtype)


def _mlstm(p, gates, bias, *, batch, seq):
    t = p.shape[0]
    L = ML_CHUNK
    nc = seq // L
    w = ML_HEADS * ML_HEAD_DIM

    def fwd(cb):
        return lambda b, c: (b * nc + c, cb)

    def bwd(cb):
        return lambda b, c: (b * nc + nc - 1 - c, cb)

    return pl.pallas_call(
        _mlstm_kernel,
        grid=(batch, nc),
        in_specs=[
            pl.BlockSpec((L, w), fwd(0)), pl.BlockSpec((L, w), fwd(1)), pl.BlockSpec((L, w), fwd(2)),
            pl.BlockSpec((L, LANES), fwd(0)),
            pl.BlockSpec((L, w), bwd(0)), pl.BlockSpec((L, w), bwd(1)), pl.BlockSpec((L, w), bwd(2)),
            pl.BlockSpec((L, LANES), bwd(0)),
            pl.BlockSpec((1, LANES), lambda b, c: (0, 0)),
        ],
        out_specs=[pl.BlockSpec((L, w), fwd(0)), pl.BlockSpec((L, w), bwd(0))],
        out_shape=[jax.ShapeDtypeStruct((t, w), BF16), jax.ShapeDtypeStruct((t, w), BF16)],
        scratch_shapes=[
            pltpu.VMEM((2 * ML_HEADS, ML_HEAD_DIM, ML_HEAD_DIM), F32),
            pltpu.VMEM((2 * ML_HEADS, SUBLANES, ML_HEAD_DIM), F32),
            pltpu.VMEM((2 * ML_HEADS, SUBLANES, LANES), F32),
        ],
        compiler_params=_params("parallel", "arbitrary"),
        name="mlstm",
    )(p, p, p, gates, p, p, p, gates, bias)


def _na_bias_table(rpb):
    c = jnp.arange(GRID_W)
    c0 = jnp.clip(c - NA_KW // 2, 0, GRID_W - NA_KW)
    kc = jnp.arange(GRID_W)
    valid = (kc[None, :] >= c0[:, None]) & (kc[None, :] < c0[:, None] + NA_KW)
    dc = jnp.clip(kc[None, :] - c[:, None] + (NA_KW - 1), 0, 2 * NA_KW - 2)
    rel = jnp.arange(NA_KH)
    dr = jnp.arange(NA_KH)[None, :] - rel[:, None] + (NA_KH - 1)
    tbl = rpb[:, dr[:, None, :, None], dc[None, :, None, :]]
    tbl = jnp.where(valid[None, None, :, None, :], tbl.astype(F32), MASK_VALUE)
    return tbl.reshape(rpb.shape[0], NA_KH, GRID_W, NA_KH * GRID_W)


def _na_kernel(q_ref, k_ref, v_ref, z_ref, tbl_ref, o_ref, *, rows):
    rb = pl.program_id(2)
    win = NA_KH * GRID_W
    lane_q = lax.broadcasted_iota(jnp.int32, (GRID_W, LANES), 1)
    head0_q = lane_q < NA_HEAD_DIM
    for i in range(NA_ROWS_PER_STEP):
        r = rb * NA_ROWS_PER_STEP + i
        r0 = jnp.clip(r - NA_KH // 2, 0, rows - NA_KH)
        rel = r - r0
        start = pl.multiple_of(r0 * GRID_W, GRID_W)
        kw = k_ref[pl.ds(start, win), :]
        vw = v_ref[pl.ds(start, win), :]
        q = q_ref[i * GRID_W:(i + 1) * GRID_W, :]
        zero = jnp.zeros_like(q)
        qs = jnp.concatenate([jnp.where(head0_q, q, zero), jnp.where(head0_q, zero, q)], axis=0)
        s = _dot_nt(qs, kw) * (NA_HEAD_DIM ** -0.5)
        s = s + jnp.concatenate([tbl_ref[0, rel], tbl_ref[1, rel]], axis=0)
        s = s - jnp.max(s, axis=-1, keepdims=True)
        e = jnp.exp(s)
        p = e / jnp.sum(e, axis=-1, keepdims=True)
        o = _dot(p.astype(BF16), vw)
        out = jnp.where(head0_q, o[:GRID_W], o[GRID_W:])
        zz = z_ref[i * GRID_W:(i + 1) * GRID_W, :].astype(F32)
        o_ref[i * GRID_W:(i + 1) * GRID_W, :] = (out * _silu(zz)).astype(o_ref.dtype)


def _natten(p, tbl, *, batch, seq, col0):
    t = p.shape[0]
    rows = seq // GRID_W
    width = NA_HEADS * NA_HEAD_DIM
    pairs = width // LANES
    tq = NA_ROWS_PER_STEP * GRID_W
    nrb = seq // tq
    cb = col0 // LANES
    kern = functools.partial(_na_kernel, rows=rows)
    return pl.pallas_call(
        kern,
        grid=(batch, pairs, nrb),
        in_specs=[
            pl.BlockSpec((tq, LANES), lambda b, h, r: (b * nrb + r, cb + h)),
            pl.BlockSpec((seq, LANES), lambda b, h, r: (b, cb + pairs + h)),
            pl.BlockSpec((seq, LANES), lambda b, h, r: (b, cb + 2 * pairs + h)),
            pl.BlockSpec((tq, LANES), lambda b, h, r: (b * nrb + r, cb + 3 * pairs + h)),
            pl.BlockSpec((2, NA_KH, GRID_W, NA_KH * GRID_W), lambda b, h, r: (h, 0, 0, 0)),
        ],
        out_specs=pl.BlockSpec((tq, LANES), lambda b, h, r: (b * nrb + r, h)),
        out_shape=jax.ShapeDtypeStruct((t, width), BF16),
        compiler_params=_params("parallel", "parallel", "arbitrary"),
        name="natten",
    )(p, p, p, p, tbl)


def _even_out_kernel(hf_ref, hb_ref, o_ref, z_ref, na_ref, g_ref, w_ref, x_ref, out_ref):
    h = hf_ref[...].astype(F32) + hb_ref[...].astype(F32)
    hn = jnp.concatenate(
        [_rms(h[:, i * ML_HEAD_DIM:(i + 1) * ML_HEAD_DIM]) for i in range(ML_HEADS)], axis=-1)
    a = hn * g_ref[...] * _sigmoid(o_ref[...].astype(F32)) * _silu(z_ref[...].astype(F32))
    wa = ML_HEADS * ML_HEAD_DIM
    y = _dot(a.astype(BF16), w_ref[:wa, :]) + _dot(na_ref[...], w_ref[wa:, :])
    out_ref[...] = x_ref[...] + y


def _even_out(hf, hb, p, na, g, w, x, *, tm):
    t, d = x.shape
    wa = hf.shape[1]
    return pl.pallas_call(
        _even_out_kernel,
        grid=(t // tm,),
        in_specs=[
            pl.BlockSpec((tm, wa), lambda i: (i, 0)),
            pl.BlockSpec((tm, wa), lambda i: (i, 0)),
            pl.BlockSpec((tm, wa), lambda i: (i, 3)),
            pl.BlockSpec((tm, wa), lambda i: (i, 4)),
            pl.BlockSpec((tm, na.shape[1]), lambda i: (i, 0)),
            pl.BlockSpec((1, wa), lambda i: (0, 0)),
            pl.BlockSpec(w.shape, lambda i: (0, 0)),
            pl.BlockSpec((tm, d), lambda i: (i, 0)),
        ],
        out_specs=pl.BlockSpec((tm, d), lambda i: (i, 0)),
        out_shape=jax.ShapeDtypeStruct((t, d), F32),
        compiler_params=_params("parallel"),
        name="even_out_proj",
    )(hf, hb, p, p, na, g, w, x)


def _gqa_kernel(q_ref, k_ref, v_ref, o_ref, m_ref, l_ref, acc_ref, *, tk):
    tq = q_ref.shape[0]
    seq = k_ref.shape[0]
    dh = GQ_HEAD_DIM
    q = q_ref[...]
    qs = jnp.concatenate([q[:, g * dh:(g + 1) * dh] for g in range(GQ_GROUP)], axis=0)
    m_ref[...] = jnp.full_like(m_ref, -jnp.inf)
    l_ref[...] = jnp.zeros_like(l_ref)
    acc_ref[...] = jnp.zeros_like(acc_ref)

    def body(c, carry):
        start = pl.multiple_of(c * tk, tk)
        k = k_ref[pl.ds(start, tk), :]
        v = v_ref[pl.ds(start, tk), :]
        s = _dot_nt(qs, k)
        m_prev = m_ref[...]
        m_new = jnp.maximum(m_prev, jnp.max(s, axis=-1, keepdims=True))
        alpha = jnp.exp(m_prev - m_new)
        p = jnp.exp(s - m_new)
        l_ref[...] = alpha * l_ref[...] + jnp.sum(p, axis=-1, keepdims=True)
        acc_ref[...] = alpha * acc_ref[...] + _dot(p.astype(BF16), v)
        m_ref[...] = m_new
        return carry

    lax.fori_loop(0, seq // tk, body, 0)
    out = acc_ref[...] / l_ref[...]
    o_ref[...] = jnp.concatenate(
        [out[g * tq:(g + 1) * tq] for g in range(GQ_GROUP)], axis=-1).astype(o_ref.dtype)


def _gqa(p, *, batch, seq, k_col, v_col):
    t = p.shape[0]
    tq, tk = GQ_BLOCK_Q, GQ_BLOCK_K
    nq = seq // tq
    gw = GQ_GROUP * GQ_HEAD_DIM
    kb, vb = k_col // GQ_HEAD_DIM, v_col // GQ_HEAD_DIM
    kern = functools.partial(_gqa_kernel, tk=tk)
    return pl.pallas_call(
        kern,
        grid=(batch, GQ_KV_HEADS, nq),
        in_specs=[
            pl.BlockSpec((tq, gw), lambda b, h, i: (b * nq + i, h)),
            pl.BlockSpec((seq, GQ_HEAD_DIM), lambda b, h, i: (b, kb + h)),
            pl.BlockSpec((seq, GQ_HEAD_DIM), lambda b, h, i: (b, vb + h)),
        ],
        out_specs=pl.BlockSpec((tq, gw), lambda b, h, i: (b * nq + i, h)),
        out_shape=jax.ShapeDtypeStruct((t, GQ_HEADS * GQ_HEAD_DIM), BF16),
        scratch_shapes=[
            pltpu.VMEM((GQ_GROUP * tq, 1), F32),
            pltpu.VMEM((GQ_GROUP * tq, 1), F32),
            pltpu.VMEM((GQ_GROUP * tq, GQ_HEAD_DIM), F32),
        ],
        compiler_params=_params("parallel", "parallel", "arbitrary"),
        name="gqa_attention",
    )(p, p, p)


def _odd_out_kernel(att_ref, gz_ref, cb_ref, cc_ref, cx_ref, cz_ref, ccp_ref, cxp_ref, ccn_ref, cxn_ref,
                    cw_ref, cbias_ref, w_ref, h_ref, fg_ref, out_ref, *, blocks_per_seq):
    tm = att_ref.shape[0]
    i = pl.program_id(0) % blocks_per_seq
    xc = cc_ref[...].astype(F32) * cx_ref[...].astype(F32)
    prev_row = ccp_ref[SUBLANES - 1:SUBLANES, :].astype(F32) * cxp_ref[SUBLANES - 1:SUBLANES, :].astype(F32)
    next_row = ccn_ref[0:1, :].astype(F32) * cxn_ref[0:1, :].astype(F32)
    prev_row = jnp.where(i == 0, 0.0, prev_row)
    next_row = jnp.where(i == blocks_per_seq - 1, 0.0, next_row)
    ridx = lax.broadcasted_iota(jnp.int32, xc.shape, 0)
    x_prev = jnp.where(ridx == 0, prev_row, pltpu.roll(xc, 1, 0))
    x_next = jnp.where(ridx == tm - 1, next_row, pltpu.roll(xc, tm - 1, 0))
    y = x_prev * cw_ref[0:1, :] + xc * cw_ref[1:2, :] + x_next * cw_ref[2:3, :] + cbias_ref[...]
    conv = cb_ref[...].astype(F32) * y * _silu(cz_ref[...].astype(F32))
    att = att_ref[...].astype(F32) * _silu(gz_ref[...].astype(F32))
    wa = att_ref.shape[1]
    hh = h_ref[...] + _dot(att.astype(BF16), w_ref[:wa, :]) + _dot(conv.astype(BF16), w_ref[wa:, :])
    out_ref[...] = _rms(hh) * fg_ref[...]


def _odd_out(att, p, cw, cbias, w, h, fg, *, tm, seq, cols):
    t, d = h.shape
    bps = seq // tm
    sub = tm // SUBLANES
    nsub = t // SUBLANES
    z_c, b_c, c_c, x_c, cz_c = cols

    def blk(cb):
        return pl.BlockSpec((tm, d), lambda i: (i, cb))

    def halo_prev(cb):
        return pl.BlockSpec((SUBLANES, d), lambda i: (jnp.maximum(i * sub - 1, 0), cb))

    def halo_next(cb):
        return pl.BlockSpec((SUBLANES, d), lambda i: (jnp.minimum((i + 1) * sub, nsub - 1), cb))

    kern = functools.partial(_odd_out_kernel, blocks_per_seq=bps)
    return pl.pallas_call(
        kern,
        grid=(t // tm,),
        in_specs=[
            pl.BlockSpec((tm, d), lambda i: (i, 0)),
            blk(z_c), blk(b_c), blk(c_c), blk(x_c), blk(cz_c),
            halo_prev(c_c), halo_prev(x_c), halo_next(c_c), halo_next(x_c),
            pl.BlockSpec(cw.shape, lambda i: (0, 0)),
            pl.BlockSpec((1, d), lambda i: (0, 0)),
            pl.BlockSpec(w.shape, lambda i: (0, 0)),
            pl.BlockSpec((tm, d), lambda i: (i, 0)),
            pl.BlockSpec((1, d), lambda i: (0, 0)),
        ],
        out_specs=pl.BlockSpec((tm, d), lambda i: (i, 0)),
        out_shape=jax.ShapeDtypeStruct((t, d), F32),
        compiler_params=_params("parallel"),
        name="odd_out_proj",
    )(att, p, p, p, p, p, p, p, p, p, cw, cbias, w, h, fg)


def _rope_tables(seq, heads):
    t = jnp.arange(seq)
    pos = jnp.stack([t // GRID_W, t % GRID_W], axis=-1).astype(F32)
    inv = ROPE_THETA ** (-jnp.arange(ROPE_HALF, dtype=F32) / ROPE_HALF)
    ang = pos[:, :, None] * inv
    cos, sin = jnp.cos(ang), jnp.sin(ang)
    cos_l = jnp.stack([cos, cos], axis=2).reshape(seq, GQ_HEAD_DIM)
    sin_l = jnp.stack([-sin, sin], axis=2).reshape(seq, GQ_HEAD_DIM)
    return jnp.tile(cos_l, (1, heads)), jnp.tile(sin_l, (1, heads))


def kernel(x, norm_g, final_g, ev_w_in, ev_gate_b, ev_w_out, ev_ml_norm_g, ev_na_rpb,
           od_w_in, od_w_out, od_q_norm_g, od_k_norm_g, od_conv_w, od_conv_b):
    batch, seq, d = x.shape
    assert d == D_MODEL and norm_g.shape[0] == 2 and seq % max(ML_CHUNK, 512) == 0
    t = batch * seq
    xf = x.reshape(t, d)
    tm_in = min(1024, seq)
    tm_out = 512

    n_gate = 4 * ML_HEADS
    g0 = 5 * D_MODEL
    w_in = ev_w_in[0]
    w_main = jnp.concatenate([w_in[:, :g0], w_in[:, g0 + n_gate:]], axis=1).astype(BF16)
    w_gate = jnp.pad(w_in[:, g0:g0 + n_gate], ((0, 0), (0, LANES - n_gate))).astype(BF16)
    p_ev, gates = _even_in_proj(xf, norm_g[0][None, :], w_main, w_gate, tm=tm_in, tn=1024)
    gate_bias = jnp.pad(ev_gate_b[0], (0, LANES - n_gate))[None, :]
    hf, hb = _mlstm(p_ev, gates, gate_bias, batch=batch, seq=seq)
    na = _natten(p_ev, _na_bias_table(ev_na_rpb[0]), batch=batch, seq=seq, col0=g0)
    h1 = _even_out(hf, hb, p_ev, na, ev_ml_norm_g[0][None, :], ev_w_out[0].astype(BF16), xf, tm=tm_out)

    w_in = od_w_in[0]
    qw, kvw = GQ_HEADS * GQ_HEAD_DIM, GQ_KV_HEADS * GQ_HEAD_DIM
    w_perm = jnp.concatenate([w_in[:, :qw], w_in[:, qw + 2 * kvw:], w_in[:, qw:qw + 2 * kvw]], axis=1).astype(BF16)
    tn_odd = 512
    cos, sin = _rope_tables(seq, tn_odd // GQ_HEAD_DIM)
    qg = jnp.tile(od_q_norm_g[0], tn_odd // GQ_HEAD_DIM)[None, :]
    kg = jnp.tile(od_k_norm_g[0], GQ_KV_HEADS)[None, :]
    p_od = _odd_in_proj(h1, norm_g[1][None, :], w_perm, qg, kg, cos, sin, tm=tm_in, tn=tn_odd, seq=seq)
    k_col = w_perm.shape[1] - 2 * kvw
    att = _gqa(p_od, batch=batch, seq=seq, k_col=k_col, v_col=k_col + kvw)
    out = _odd_out(att, p_od, od_conv_w[0], od_conv_b[0][None, :], od_w_out[0].astype(BF16), h1,
                   final_g[None, :], tm=tm_out, seq=seq, cols=(1, 2, 3, 4, 5))
    return out.reshape(batch, seq, d)
```

```python
import functools

import jax
import jax.numpy as jnp
import numpy as np
from jax import lax
from jax.experimental import pallas as pl
from jax.experimental.pallas import tpu as pltpu

F32 = jnp.float32
BF16 = jnp.bfloat16

D_MODEL = 1024
GRID_W = 64
RMS_EPS = 1e-6

ML_HEADS = 4
ML_HEAD_DIM = 256
ML_CHUNK = 256

NA_HEADS = 16
NA_HEAD_DIM = 64
NA_KH = 8
NA_KW = 16
NA_ROWS_PER_STEP = 8

GQ_HEADS = 8
GQ_HEAD_DIM = 128
GQ_KV_HEADS = 2
GQ_GROUP = GQ_HEADS // GQ_KV_HEADS
ROPE_THETA = 10000.0
ROPE_HALF = GQ_HEAD_DIM // 4
GQ_BLOCK_Q = 128
GQ_BLOCK_K = 512

CV_K = 3

LANES = 128
SUBLANES = 8
MASK_VALUE = -1e30
LOG2_E = 1.4426950408889634

VMEM_LIMIT = 56 * 1024 * 1024


def _params(*sem):
    return pltpu.CompilerParams(dimension_semantics=sem, vmem_limit_bytes=VMEM_LIMIT)


def _sigmoid(x):
    return 1.0 / (1.0 + jnp.exp(-x))


def _silu(x):
    return x * _sigmoid(x)


def _log_sigmoid(x):
    return jnp.minimum(x, 0.0) - jnp.log1p(jnp.exp(-jnp.abs(x)))


def _rms(x, eps=RMS_EPS):
    return x * lax.rsqrt(jnp.mean(x * x, axis=-1, keepdims=True) + eps)


def _dot(a, b):
    return jnp.dot(a, b, preferred_element_type=F32)


def _dot_nt(a, b):
    return lax.dot_general(a, b, (((1,), (1,)), ((), ())), preferred_element_type=F32)


def _dot_tn(a, b):
    return lax.dot_general(a, b, (((0,), (0,)), ((), ())), preferred_element_type=F32)


def _even_in_kernel(x_ref, g_ref, w_ref, wg_ref, o_ref, og_ref, u_ref):
    @pl.when(pl.program_id(1) == 0)
    def _():
        u = (_rms(x_ref[...]) * g_ref[...]).astype(BF16)
        u_ref[...] = u
        og_ref[...] = _dot(u, wg_ref[...])

    o_ref[...] = _dot(u_ref[...], w_ref[...]).astype(o_ref.dtype)


def _even_in_proj(x, g, w, wg, *, tm, tn):
    t, d = x.shape
    n = w.shape[1]
    return pl.pallas_call(
        _even_in_kernel,
        grid=(t // tm, n // tn),
        in_specs=[
            pl.BlockSpec((tm, d), lambda i, j: (i, 0)),
            pl.BlockSpec((1, d), lambda i, j: (0, 0)),
            pl.BlockSpec((d, tn), lambda i, j: (0, j)),
            pl.BlockSpec((d, LANES), lambda i, j: (0, 0)),
        ],
        out_specs=[
            pl.BlockSpec((tm, tn), lambda i, j: (i, j)),
            pl.BlockSpec((tm, LANES), lambda i, j: (i, 0)),
        ],
        out_shape=[
            jax.ShapeDtypeStruct((t, n), BF16),
            jax.ShapeDtypeStruct((t, LANES), F32),
        ],
        scratch_shapes=[pltpu.VMEM((tm, d), BF16)],
        compiler_params=_params("parallel", "arbitrary"),
        name="even_in_proj",
    )(x, g, w, wg)


def _rope_norm(acc, gain, cos, sin, scale):
    nh = acc.shape[1] // GQ_HEAD_DIM
    y = jnp.concatenate(
        [_rms(acc[:, h * GQ_HEAD_DIM:(h + 1) * GQ_HEAD_DIM]) for h in range(nh)], axis=-1) * gain
    width = y.shape[1]
    lane = lax.broadcasted_iota(jnp.int32, y.shape, 1)
    first_half = (lane % (2 * ROPE_HALF)) < ROPE_HALF
    partner = jnp.where(first_half, pltpu.roll(y, width - ROPE_HALF, 1), pltpu.roll(y, ROPE_HALF, 1))
    return (y * cos + partner * sin) * scale


def _odd_in_kernel(x_ref, g_ref, w_ref, qg_ref, kg_ref, cos_ref, sin_ref, o_ref, u_ref, *, n_q_tiles, kv_tile):
    j = pl.program_id(1)

    @pl.when(j == 0)
    def _():
        u_ref[...] = (_rms(x_ref[...]) * g_ref[...]).astype(BF16)

    acc = _dot(u_ref[...], w_ref[...])
    is_q = j < n_q_tiles
    is_kv = j == kv_tile

    @pl.when(is_q)
    def _():
        o_ref[...] = _rope_norm(acc, qg_ref[...], cos_ref[...], sin_ref[...],
                                GQ_HEAD_DIM ** -0.5 * LOG2_E).astype(o_ref.dtype)

    @pl.when(is_kv)
    def _():
        kw = GQ_KV_HEADS * GQ_HEAD_DIM
        k = _rope_norm(acc[:, :kw], kg_ref[...], cos_ref[:, :kw], sin_ref[:, :kw], 1.0)
        o_ref[...] = jnp.concatenate([k, acc[:, kw:]], axis=-1).astype(o_ref.dtype)

    @pl.when(jnp.logical_not(jnp.logical_or(is_q, is_kv)))
    def _():
        o_ref[...] = acc.astype(o_ref.dtype)


def _odd_in_proj(x, g, w, qg, kg, cos, sin, *, tm, tn, seq):
    t, d = x.shape
    n = w.shape[1]
    n_q_tiles = (GQ_HEADS * GQ_HEAD_DIM) // tn
    kv_tile = n // tn - 1
    sb = seq // tm
    kern = functools.partial(_odd_in_kernel, n_q_tiles=n_q_tiles, kv_tile=kv_tile)
    return pl.pallas_call(
        kern,
        grid=(t // tm, n // tn),
        in_specs=[
            pl.BlockSpec((tm, d), lambda i, j: (i, 0)),
            pl.BlockSpec((1, d), lambda i, j: (0, 0)),
            pl.BlockSpec((d, tn), lambda i, j: (0, j)),
            pl.BlockSpec((1, tn), lambda i, j: (0, 0)),
            pl.BlockSpec((1, GQ_KV_HEADS * GQ_HEAD_DIM), lambda i, j: (0, 0)),
            pl.BlockSpec((tm, tn), lambda i, j: (i % sb, 0)),
            pl.BlockSpec((tm, tn), lambda i, j: (i % sb, 0)),
        ],
        out_specs=pl.BlockSpec((tm, tn), lambda i, j: (i, j)),
        out_shape=jax.ShapeDtypeStruct((t, n), BF16),
        scratch_shapes=[pltpu.VMEM((tm, d), BF16)],
        compiler_params=_params("parallel", "arbitrary"),
        name="odd_in_proj",
    )(x, g, w, qg, kg, cos, sin)


def _mlstm_kernel(qf_ref, kf_ref, vf_ref, gf_ref, qb_ref, kb_ref, vb_ref, gb_ref, bias_ref,
                  hf_ref, hb_ref, c_ref, n_ref, m_ref):
    L = qf_ref.shape[0]
    dh = ML_HEAD_DIM
    k_scale = dh ** -0.5

    @pl.when(pl.program_id(1) == 0)
    def _():
        c_ref[...] = jnp.zeros_like(c_ref)
        n_ref[...] = jnp.zeros_like(n_ref)
        m_ref[...] = jnp.zeros_like(m_ref)

    row = lax.broadcasted_iota(jnp.int32, (L, L), 0)
    col = lax.broadcasted_iota(jnp.int32, (L, L), 1)
    lower = row >= col
    upper = row <= col
    lane = lax.broadcasted_iota(jnp.int32, (L, LANES), 1)
    is_forget = ((lane >= ML_HEADS) & (lane < 2 * ML_HEADS)) | ((lane >= 3 * ML_HEADS) & (lane < 4 * ML_HEADS))

    def gate_tables(g_ref, tri):
        g = g_ref[...] + bias_ref[...]
        lf = jnp.where(is_forget, _log_sigmoid(g), 0.0)
        cum = jnp.dot(tri.astype(F32), lf, preferred_element_type=F32, precision=lax.Precision.HIGHEST)
        tab = jnp.where(is_forget, cum, g)
        return tab, tab.T

    tab_f, tab_ft = gate_tables(gf_ref, lower)
    tab_b, tab_bt = gate_tables(gb_ref, upper)

    def chain(idx, q, k, v, bcol, igcol, brow, igrow, g_tot, valid):
        c = c_ref[idx]
        n = n_ref[idx][0:1, :]
        m_prev = m_ref[idx][0:1, 0:1]
        d = jnp.where(valid, bcol - brow + igrow, MASK_VALUE)
        inter = bcol + m_prev
        m_t = jnp.maximum(inter, jnp.max(d, axis=-1, keepdims=True))
        w = jnp.exp(d - m_t) * (_dot_nt(q, k) * k_scale)
        decay = jnp.exp(inter - m_t)
        qf = q.astype(F32)
        num = decay * _dot(q, c.astype(BF16)) + _dot(w.astype(BF16), v)
        den = decay * jnp.sum(qf * n, axis=-1, keepdims=True) + jnp.sum(w, axis=-1, keepdims=True)
        h = num / jnp.maximum(jnp.abs(den), jnp.exp(-m_t))
        a = g_tot - bcol + igcol
        m_new = jnp.maximum(g_tot + m_prev, jnp.max(a, axis=0, keepdims=True))
        carry = jnp.exp(g_tot + m_prev - m_new)
        wk = jnp.exp(a - m_new)
        wkv = (wk * v.astype(F32)).astype(BF16)
        c_ref[idx] = carry * c + _dot_tn(k, wkv) * k_scale
        n_new = carry * n + jnp.sum(wk * k.astype(F32), axis=0, keepdims=True) * k_scale
        n_ref[idx] = jnp.broadcast_to(n_new, n_ref.shape[1:])
        m_ref[idx] = jnp.broadcast_to(m_new, m_ref.shape[1:])
        return h

    for hd in range(ML_HEADS):
        sl = slice(hd * dh, (hd + 1) * dh)
        fi, ff = hd, ML_HEADS + hd
        h_f = chain(hd, qf_ref[:, sl], kf_ref[:, sl], vf_ref[:, sl],
                    tab_f[:, ff:ff + 1], tab_f[:, fi:fi + 1], tab_ft[ff:ff + 1, :], tab_ft[fi:fi + 1, :],
                    tab_f[L - 1:L, ff:ff + 1], lower)
        hf_ref[:, sl] = h_f.astype(hf_ref.dtype)
        bi, bf = 2 * ML_HEADS + hd, 3 * ML_HEADS + hd
        h_b = chain(ML_HEADS + hd, qb_ref[:, sl], kb_ref[:, sl], vb_ref[:, sl],
                    tab_b[:, bf:bf + 1], tab_b[:, bi:bi + 1], tab_bt[bf:bf + 1, :], tab_bt[bi:bi + 1, :],
                    tab_b[0:1, bf:bf + 1], upper)
        hb_ref[:, sl] = h_b.astype(hb_ref.dtype)


def _mlstm(p, gates, bias, *, batch, seq):
    t = p.shape[0]
    L = ML_CHUNK
    nc = seq // L
    w = ML_HEADS * ML_HEAD_DIM

    def fwd(cb):
        return lambda b, c: (b * nc + c, cb)

    def bwd(cb):
        return lambda b, c: (b * nc + nc - 1 - c, cb)

    return pl.pallas_call(
        _mlstm_kernel,
        grid=(batch, nc),
        in_specs=[
            pl.BlockSpec((L, w), fwd(0)), pl.BlockSpec((L, w), fwd(1)), pl.BlockSpec((L, w), fwd(2)),
            pl.BlockSpec((L, LANES), fwd(0)),
            pl.BlockSpec((L, w), bwd(0)), pl.BlockSpec((L, w), bwd(1)), pl.BlockSpec((L, w), bwd(2)),
            pl.BlockSpec((L, LANES), bwd(0)),
            pl.BlockSpec((1, LANES), lambda b, c: (0, 0)),
        ],
        out_specs=[pl.BlockSpec((L, w), fwd(0)), pl.BlockSpec((L, w), bwd(0))],
        out_shape=[jax.ShapeDtypeStruct((t, w), BF16), jax.ShapeDtypeStruct((t, w), BF16)],
        scratch_shapes=[
            pltpu.VMEM((2 * ML_HEADS, ML_HEAD_DIM, ML_HEAD_DIM), F32),
            pltpu.VMEM((2 * ML_HEADS, SUBLANES, ML_HEAD_DIM), F32),
            pltpu.VMEM((2 * ML_HEADS, SUBLANES, LANES), F32),
        ],
        compiler_params=_params("parallel", "arbitrary"),
        name="mlstm",
    )(p, p, p, gates, p, p, p, gates, bias)


def _na_bias_table(rpb):
    nh = rpb.shape[0]
    c = np.arange(GRID_W)
    c0 = np.clip(c - NA_KW // 2, 0, GRID_W - NA_KW)
    kc = np.arange(GRID_W)
    valid = (kc[None, :] >= c0[:, None]) & (kc[None, :] < c0[:, None] + NA_KW)
    pad = GRID_W - NA_KW
    rp = jnp.pad(rpb.astype(F32), ((0, 0), (0, 0), (pad, pad)))
    cols = jnp.stack([rp[:, :, GRID_W - 1 - ci:2 * GRID_W - 1 - ci] for ci in range(GRID_W)], axis=2)
    cols = jnp.where(valid[None, None], cols, MASK_VALUE)
    tbl = jnp.stack([cols[:, NA_KH - 1 - rel:2 * NA_KH - 1 - rel] for rel in range(NA_KH)], axis=1)
    tbl = tbl.transpose(0, 1, 3, 2, 4)
    return tbl.reshape(nh, NA_KH, GRID_W, NA_KH * GRID_W)


def _na_kernel(q_ref, k_ref, v_ref, z_ref, tbl_ref, o_ref, *, rows):
    rb = pl.program_id(2)
    win = NA_KH * GRID_W
    lane_q = lax.broadcasted_iota(jnp.int32, (GRID_W, LANES), 1)
    head0_q = lane_q < NA_HEAD_DIM
    for i in range(NA_ROWS_PER_STEP):
        r = rb * NA_ROWS_PER_STEP + i
        r0 = jnp.clip(r - NA_KH // 2, 0, rows - NA_KH)
        rel = r - r0
        start = pl.multiple_of(r0 * GRID_W, GRID_W)
        kw = k_ref[pl.ds(start, win), :]
        vw = v_ref[pl.ds(start, win), :]
        q = q_ref[i * GRID_W:(i + 1) * GRID_W, :]
        zero = jnp.zeros_like(q)
        qs = jnp.concatenate([jnp.where(head0_q, q, zero), jnp.where(head0_q, zero, q)], axis=0)
        s = _dot_nt(qs, kw) * (NA_HEAD_DIM ** -0.5)
        s = s + jnp.concatenate([tbl_ref[0, rel], tbl_ref[1, rel]], axis=0)
        s = s - jnp.max(s, axis=-1, keepdims=True)
        e = jnp.exp(s)
        p = e / jnp.sum(e, axis=-1, keepdims=True)
        o = _dot(p.astype(BF16), vw)
        out = jnp.where(head0_q, o[:GRID_W], o[GRID_W:])
        zz = z_ref[i * GRID_W:(i + 1) * GRID_W, :].astype(F32)
        o_ref[i * GRID_W:(i + 1) * GRID_W, :] = (out * _silu(zz)).astype(o_ref.dtype)


def _natten(p, tbl, *, batch, seq, col0):
    t = p.shape[0]
    rows = seq // GRID_W
    width = NA_HEADS * NA_HEAD_DIM
    pairs = width // LANES
    tq = NA_ROWS_PER_STEP * GRID_W
    nrb = seq // tq
    cb = col0 // LANES
    kern = functools.partial(_na_kernel, rows=rows)
    return pl.pallas_call(
        kern,
        grid=(batch, pairs, nrb),
        in_specs=[
            pl.BlockSpec((tq, LANES), lambda b, h, r: (b * nrb + r, cb + h)),
            pl.BlockSpec((seq, LANES), lambda b, h, r: (b, cb + pairs + h)),
            pl.BlockSpec((seq, LANES), lambda b, h, r: (b, cb + 2 * pairs + h)),
            pl.BlockSpec((tq, LANES), lambda b, h, r: (b * nrb + r, cb + 3 * pairs + h)),
            pl.BlockSpec((2, NA_KH, GRID_W, NA_KH * GRID_W), lambda b, h, r: (h, 0, 0, 0)),
        ],
        out_specs=pl.BlockSpec((tq, LANES), lambda b, h, r: (b * nrb + r, h)),
        out_shape=jax.ShapeDtypeStruct((t, width), BF16),
        compiler_params=_params("parallel", "parallel", "arbitrary"),
        name="natten",
    )(p, p, p, p, tbl)


def _even_out_kernel(hf_ref, hb_ref, o_ref, z_ref, na_ref, g_ref, w_ref, x_ref, out_ref):
    h = hf_ref[...].astype(F32) + hb_ref[...].astype(F32)
    hn = jnp.concatenate(
        [_rms(h[:, i * ML_HEAD_DIM:(i + 1) * ML_HEAD_DIM]) for i in range(ML_HEADS)], axis=-1)
    a = hn * g_ref[...] * _sigmoid(o_ref[...].astype(F32)) * _silu(z_ref[...].astype(F32))
    wa = ML_HEADS * ML_HEAD_DIM
    y = _dot(a.astype(BF16), w_ref[:wa, :]) + _dot(na_ref[...], w_ref[wa:, :])
    out_ref[...] = x_ref[...] + y


def _even_out(hf, hb, p, na, g, w, x, *, tm):
    t, d = x.shape
    wa = hf.shape[1]
    return pl.pallas_call(
        _even_out_kernel,
        grid=(t // tm,),
        in_specs=[
            pl.BlockSpec((tm, wa), lambda i: (i, 0)),
            pl.BlockSpec((tm, wa), lambda i: (i, 0)),
            pl.BlockSpec((tm, wa), lambda i: (i, 3)),
            pl.BlockSpec((tm, wa), lambda i: (i, 4)),
            pl.BlockSpec((tm, na.shape[1]), lambda i: (i, 0)),
            pl.BlockSpec((1, wa), lambda i: (0, 0)),
            pl.BlockSpec(w.shape, lambda i: (0, 0)),
            pl.BlockSpec((tm, d), lambda i: (i, 0)),
        ],
        out_specs=pl.BlockSpec((tm, d), lambda i: (i, 0)),
        out_shape=jax.ShapeDtypeStruct((t, d), F32),
        compiler_params=_params("parallel"),
        name="even_out_proj",
    )(hf, hb, p, p, na, g, w, x)


def _gqa_kernel(q_ref, k_ref, v_ref, o_ref, vaug_ref, m_ref, acc_ref, *, tk):
    tq = q_ref.shape[0]
    seq = k_ref.shape[0]
    dh = GQ_HEAD_DIM

    @pl.when(pl.program_id(2) == 0)
    def _():
        vaug_ref[:, :dh] = v_ref[...]
        vaug_ref[:, dh:] = jnp.ones((seq, dh), BF16)

    q = q_ref[...]
    qs = jnp.concatenate([q[:, g * dh:(g + 1) * dh] for g in range(GQ_GROUP)], axis=0)
    m_ref[...] = jnp.full_like(m_ref, -jnp.inf)
    acc_ref[...] = jnp.zeros_like(acc_ref)

    def body(c, carry):
        start = pl.multiple_of(c * tk, tk)
        k = k_ref[pl.ds(start, tk), :]
        v = vaug_ref[pl.ds(start, tk), :]
        s = _dot_nt(qs, k)
        m_prev = m_ref[...]
        m_new = jnp.maximum(m_prev, jnp.max(s, axis=-1, keepdims=True))
        alpha = jnp.exp2(m_prev - m_new)
        p = jnp.exp2(s - pltpu.repeat(m_new, tk // LANES, axis=1))
        acc_ref[...] = pltpu.repeat(alpha, 2, axis=1) * acc_ref[...] + _dot(p.astype(BF16), v)
        m_ref[...] = m_new
        return carry

    lax.fori_loop(0, seq // tk, body, 0, unroll=8)
    acc = acc_ref[...]
    out = acc[:, :dh] / acc[:, dh:]
    o_ref[...] = jnp.concatenate(
        [out[g * tq:(g + 1) * tq] for g in range(GQ_GROUP)], axis=-1).astype(o_ref.dtype)


def _gqa(p, *, batch, seq, k_col, v_col):
    t = p.shape[0]
    tq, tk = GQ_BLOCK_Q, GQ_BLOCK_K
    nq = seq // tq
    gw = GQ_GROUP * GQ_HEAD_DIM
    kb, vb = k_col // GQ_HEAD_DIM, v_col // GQ_HEAD_DIM
    kern = functools.partial(_gqa_kernel, tk=tk)
    return pl.pallas_call(
        kern,
        grid=(batch, GQ_KV_HEADS, nq),
        in_specs=[
            pl.BlockSpec((tq, gw), lambda b, h, i: (b * nq + i, h)),
            pl.BlockSpec((seq, GQ_HEAD_DIM), lambda b, h, i: (b, kb + h)),
            pl.BlockSpec((seq, GQ_HEAD_DIM), lambda b, h, i: (b, vb + h)),
        ],
        out_specs=pl.BlockSpec((tq, gw), lambda b, h, i: (b * nq + i, h)),
        out_shape=jax.ShapeDtypeStruct((t, GQ_HEADS * GQ_HEAD_DIM), BF16),
        scratch_shapes=[
            pltpu.VMEM((seq, 2 * GQ_HEAD_DIM), BF16),
            pltpu.VMEM((GQ_GROUP * tq, LANES), F32),
            pltpu.VMEM((GQ_GROUP * tq, 2 * GQ_HEAD_DIM), F32),
        ],
        compiler_params=_params("parallel", "parallel", "arbitrary"),
        name="gqa_attention",
    )(p, p, p)


def _odd_out_kernel(att_ref, gz_ref, cb_ref, cc_ref, cx_ref, cz_ref, ccp_ref, cxp_ref, ccn_ref, cxn_ref,
                    cw_ref, cbias_ref, w_ref, h_ref, fg_ref, out_ref, *, blocks_per_seq):
    tm = att_ref.shape[0]
    i = pl.program_id(0) % blocks_per_seq
    xc = cc_ref[...].astype(F32) * cx_ref[...].astype(F32)
    prev_row = ccp_ref[SUBLANES - 1:SUBLANES, :].astype(F32) * cxp_ref[SUBLANES - 1:SUBLANES, :].astype(F32)
    next_row = ccn_ref[0:1, :].astype(F32) * cxn_ref[0:1, :].astype(F32)
    prev_row = jnp.where(i == 0, 0.0, prev_row)
    next_row = jnp.where(i == blocks_per_seq - 1, 0.0, next_row)
    ridx = lax.broadcasted_iota(jnp.int32, xc.shape, 0)
    x_prev = jnp.where(ridx == 0, prev_row, pltpu.roll(xc, 1, 0))
    x_next = jnp.where(ridx == tm - 1, next_row, pltpu.roll(xc, tm - 1, 0))
    y = x_prev * cw_ref[0:1, :] + xc * cw_ref[1:2, :] + x_next * cw_ref[2:3, :] + cbias_ref[...]
    conv = cb_ref[...].astype(F32) * y * _silu(cz_ref[...].astype(F32))
    att = att_ref[...].astype(F32) * _silu(gz_ref[...].astype(F32))
    wa = att_ref.shape[1]
    hh = h_ref[...] + _dot(att.astype(BF16), w_ref[:wa, :]) + _dot(conv.astype(BF16), w_ref[wa:, :])
    out_ref[...] = _rms(hh) * fg_ref[...]


def _odd_out(att, p, cw, cbias, w, h, fg, *, tm, seq, cols):
    t, d = h.shape
    bps = seq // tm
    sub = tm // SUBLANES
    nsub = t // SUBLANES
    z_c, b_c, c_c, x_c, cz_c = cols

    def blk(cb):
        return pl.BlockSpec((tm, d), lambda i: (i, cb))

    def halo_prev(cb):
        return pl.BlockSpec((SUBLANES, d), lambda i: (jnp.maximum(i * sub - 1, 0), cb))

    def halo_next(cb):
        return pl.BlockSpec((SUBLANES, d), lambda i: (jnp.minimum((i + 1) * sub, nsub - 1), cb))

    kern = functools.partial(_odd_out_kernel, blocks_per_seq=bps)
    return pl.pallas_call(
        kern,
        grid=(t // tm,),
        in_specs=[
            pl.BlockSpec((tm, d), lambda i: (i, 0)),
            blk(z_c), blk(b_c), blk(c_c), blk(x_c), blk(cz_c),
            halo_prev(c_c), halo_prev(x_c), halo_next(c_c), halo_next(x_c),
            pl.BlockSpec(cw.shape, lambda i: (0, 0)),
            pl.BlockSpec((1, d), lambda i: (0, 0)),
            pl.BlockSpec(w.shape, lambda i: (0, 0)),
            pl.BlockSpec((tm, d), lambda i: (i, 0)),
            pl.BlockSpec((1, d), lambda i: (0, 0)),
        ],
        out_specs=pl.BlockSpec((tm, d), lambda i: (i, 0)),
        out_shape=jax.ShapeDtypeStruct((t, d), F32),
        compiler_params=_params("parallel"),
        name="odd_out_proj",
    )(att, p, p, p, p, p, p, p, p, p, cw, cbias, w, h, fg)


def _rope_tables(seq, heads):
    t = jnp.arange(seq)
    pos = jnp.stack([t // GRID_W, t % GRID_W], axis=-1).astype(F32)
    inv = ROPE_THETA ** (-jnp.arange(ROPE_HALF, dtype=F32) / ROPE_HALF)
    ang = pos[:, :, None] * inv
    cos, sin = jnp.cos(ang), jnp.sin(ang)
    cos_l = jnp.stack([cos, cos], axis=2).reshape(seq, GQ_HEAD_DIM)
    sin_l = jnp.stack([-sin, sin], axis=2).reshape(seq, GQ_HEAD_DIM)
    return jnp.tile(cos_l, (1, heads)), jnp.tile(sin_l, (1, heads))


def kernel(x, norm_g, final_g, ev_w_in, ev_gate_b, ev_w_out, ev_ml_norm_g, ev_na_rpb,
           od_w_in, od_w_out, od_q_norm_g, od_k_norm_g, od_conv_w, od_conv_b):
    batch, seq, d = x.shape
    assert d == D_MODEL and norm_g.shape[0] == 2 and seq % max(ML_CHUNK, 512) == 0
    t = batch * seq
    xf = x.reshape(t, d)
    tm_in = min(1024, seq)
    tm_out = 512

    n_gate = 4 * ML_HEADS
    g0 = 5 * D_MODEL
    w_in = ev_w_in[0]
    w_main = jnp.concatenate([w_in[:, :g0], w_in[:, g0 + n_gate:]], axis=1).astype(BF16)
    w_gate = jnp.pad(w_in[:, g0:g0 + n_gate], ((0, 0), (0, LANES - n_gate))).astype(BF16)
    p_ev, gates = _even_in_proj(xf, norm_g[0][None, :], w_main, w_gate, tm=tm_in, tn=1024)
    gate_bias = jnp.pad(ev_gate_b[0], (0, LANES - n_gate))[None, :]
    hf, hb = _mlstm(p_ev, gates, gate_bias, batch=batch, seq=seq)
    na = _natten(p_ev, _na_bias_table(ev_na_rpb[0]), batch=batch, seq=seq, col0=g0)
    h1 = _even_out(hf, hb, p_ev, na, ev_ml_norm_g[0][None, :], ev_w_out[0].astype(BF16), xf, tm=tm_out)

    w_in = od_w_in[0]
    qw, kvw = GQ_HEADS * GQ_HEAD_DIM, GQ_KV_HEADS * GQ_HEAD_DIM
    w_perm = jnp.concatenate([w_in[:, :qw], w_in[:, qw + 2 * kvw:], w_in[:, qw:qw + 2 * kvw]], axis=1).astype(BF16)
    tn_odd = 512
    cos, sin = _rope_tables(seq, tn_odd // GQ_HEAD_DIM)
    qg = jnp.tile(od_q_norm_g[0], tn_odd // GQ_HEAD_DIM)[None, :]
    kg = jnp.tile(od_k_norm_g[0], GQ_KV_HEADS)[None, :]
    p_od = _odd_in_proj(h1, norm_g[1][None, :], w_perm, qg, kg, cos, sin, tm=tm_in, tn=tn_odd, seq=seq)
    k_col = w_perm.shape[1] - 2 * kvw
    att = _gqa(p_od, batch=batch, seq=seq, k_col=k_col, v_col=k_col + kvw)
    out = _odd_out(att, p_od, od_conv_w[0], od_conv_b[0][None, :], od_w_out[0].astype(BF16), h1,
                   final_g[None, :], tm=tm_out, seq=seq, cols=(1, 2, 3, 4, 5))
    return out.reshape(batch, seq, d)
```

```python
import functools

import jax
import jax.numpy as jnp
import numpy as np
from jax import lax
from jax.experimental import pallas as pl
from jax.experimental.pallas import tpu as pltpu

F32 = jnp.float32
BF16 = jnp.bfloat16

D_MODEL = 1024
GRID_W = 64
RMS_EPS = 1e-6

ML_HEADS = 4
ML_HEAD_DIM = 256
ML_CHUNK = 256

NA_HEADS = 16
NA_HEAD_DIM = 64
NA_KH = 8
NA_KW = 16
NA_ROWS_PER_STEP = 8

GQ_HEADS = 8
GQ_HEAD_DIM = 128
GQ_KV_HEADS = 2
GQ_GROUP = GQ_HEADS // GQ_KV_HEADS
ROPE_THETA = 10000.0
ROPE_HALF = GQ_HEAD_DIM // 4
GQ_BLOCK_Q = 128
GQ_BLOCK_K = 512

CV_K = 3

LANES = 128
SUBLANES = 8
MASK_VALUE = -1e30
LOG2_E = 1.4426950408889634

VMEM_LIMIT = 56 * 1024 * 1024


def _params(*sem):
    return pltpu.CompilerParams(dimension_semantics=sem, vmem_limit_bytes=VMEM_LIMIT)


def _sigmoid(x):
    return 1.0 / (1.0 + jnp.exp(-x))


def _silu(x):
    return x * _sigmoid(x)


def _log_sigmoid(x):
    return jnp.minimum(x, 0.0) - jnp.log1p(jnp.exp(-jnp.abs(x)))


def _rms(x, eps=RMS_EPS):
    return x * lax.rsqrt(jnp.mean(x * x, axis=-1, keepdims=True) + eps)


def _dot(a, b):
    return jnp.dot(a, b, preferred_element_type=F32)


def _dot_nt(a, b):
    return lax.dot_general(a, b, (((1,), (1,)), ((), ())), preferred_element_type=F32)


def _dot_tn(a, b):
    return lax.dot_general(a, b, (((0,), (0,)), ((), ())), preferred_element_type=F32)


def _even_in_kernel(x_ref, g_ref, w_ref, wg_ref, o_ref, og_ref, u_ref):
    @pl.when(pl.program_id(1) == 0)
    def _():
        u = (_rms(x_ref[...]) * g_ref[...]).astype(BF16)
        u_ref[...] = u
        og_ref[...] = _dot(u, wg_ref[...])

    o_ref[...] = _dot(u_ref[...], w_ref[...]).astype(o_ref.dtype)


def _even_in_proj(x, g, w, wg, *, tm, tn):
    t, d = x.shape
    n = w.shape[1]
    return pl.pallas_call(
        _even_in_kernel,
        grid=(t // tm, n // tn),
        in_specs=[
            pl.BlockSpec((tm, d), lambda i, j: (i, 0)),
            pl.BlockSpec((1, d), lambda i, j: (0, 0)),
            pl.BlockSpec((d, tn), lambda i, j: (0, j)),
            pl.BlockSpec((d, LANES), lambda i, j: (0, 0)),
        ],
        out_specs=[
            pl.BlockSpec((tm, tn), lambda i, j: (i, j)),
            pl.BlockSpec((tm, LANES), lambda i, j: (i, 0)),
        ],
        out_shape=[
            jax.ShapeDtypeStruct((t, n), BF16),
            jax.ShapeDtypeStruct((t, LANES), F32),
        ],
        scratch_shapes=[pltpu.VMEM((tm, d), BF16)],
        compiler_params=_params("parallel", "arbitrary"),
        name="even_in_proj",
    )(x, g, w, wg)


def _rope_norm(acc, gain, cos, sin, scale):
    nh = acc.shape[1] // GQ_HEAD_DIM
    y = jnp.concatenate(
        [_rms(acc[:, h * GQ_HEAD_DIM:(h + 1) * GQ_HEAD_DIM]) for h in range(nh)], axis=-1) * gain
    width = y.shape[1]
    lane = lax.broadcasted_iota(jnp.int32, y.shape, 1)
    first_half = (lane % (2 * ROPE_HALF)) < ROPE_HALF
    partner = jnp.where(first_half, pltpu.roll(y, width - ROPE_HALF, 1), pltpu.roll(y, ROPE_HALF, 1))
    return (y * cos + partner * sin) * scale


def _odd_in_kernel(x_ref, g_ref, w_ref, qg_ref, kg_ref, cos_ref, sin_ref, o_ref, u_ref, *, n_q_tiles, kv_tile):
    j = pl.program_id(1)

    @pl.when(j == 0)
    def _():
        u_ref[...] = (_rms(x_ref[...]) * g_ref[...]).astype(BF16)

    acc = _dot(u_ref[...], w_ref[...])
    is_q = j < n_q_tiles
    is_kv = j == kv_tile

    @pl.when(is_q)
    def _():
        o_ref[...] = _rope_norm(acc, qg_ref[...], cos_ref[...], sin_ref[...],
                                GQ_HEAD_DIM ** -0.5 * LOG2_E).astype(o_ref.dtype)

    @pl.when(is_kv)
    def _():
        kw = GQ_KV_HEADS * GQ_HEAD_DIM
        k = _rope_norm(acc[:, :kw], kg_ref[...], cos_ref[:, :kw], sin_ref[:, :kw], 1.0)
        o_ref[...] = jnp.concatenate([k, acc[:, kw:]], axis=-1).astype(o_ref.dtype)

    @pl.when(jnp.logical_not(jnp.logical_or(is_q, is_kv)))
    def _():
        o_ref[...] = acc.astype(o_ref.dtype)


def _odd_in_proj(x, g, w, qg, kg, cos, sin, *, tm, tn, seq):
    t, d = x.shape
    n = w.shape[1]
    n_q_tiles = (GQ_HEADS * GQ_HEAD_DIM) // tn
    kv_tile = n // tn - 1
    sb = seq // tm
    kern = functools.partial(_odd_in_kernel, n_q_tiles=n_q_tiles, kv_tile=kv_tile)
    return pl.pallas_call(
        kern,
        grid=(t // tm, n // tn),
        in_specs=[
            pl.BlockSpec((tm, d), lambda i, j: (i, 0)),
            pl.BlockSpec((1, d), lambda i, j: (0, 0)),
            pl.BlockSpec((d, tn), lambda i, j: (0, j)),
            pl.BlockSpec((1, tn), lambda i, j: (0, 0)),
            pl.BlockSpec((1, GQ_KV_HEADS * GQ_HEAD_DIM), lambda i, j: (0, 0)),
            pl.BlockSpec((tm, tn), lambda i, j: (i % sb, 0)),
            pl.BlockSpec((tm, tn), lambda i, j: (i % sb, 0)),
        ],
        out_specs=pl.BlockSpec((tm, tn), lambda i, j: (i, j)),
        out_shape=jax.ShapeDtypeStruct((t, n), BF16),
        scratch_shapes=[pltpu.VMEM((tm, d), BF16)],
        compiler_params=_params("parallel", "arbitrary"),
        name="odd_in_proj",
    )(x, g, w, qg, kg, cos, sin)


def _mlstm_kernel(qf_ref, kf_ref, vf_ref, gf_ref, qb_ref, kb_ref, vb_ref, gb_ref, bias_ref,
                  hf_ref, hb_ref, c_ref, n_ref, m_ref):
    L = qf_ref.shape[0]
    dh = ML_HEAD_DIM
    k_scale = dh ** -0.5

    @pl.when(pl.program_id(1) == 0)
    def _():
        c_ref[...] = jnp.zeros_like(c_ref)
        n_ref[...] = jnp.zeros_like(n_ref)
        m_ref[...] = jnp.zeros_like(m_ref)

    row = lax.broadcasted_iota(jnp.int32, (L, L), 0)
    col = lax.broadcasted_iota(jnp.int32, (L, L), 1)
    lower = row >= col
    upper = row <= col
    lane = lax.broadcasted_iota(jnp.int32, (L, LANES), 1)
    is_forget = ((lane >= ML_HEADS) & (lane < 2 * ML_HEADS)) | ((lane >= 3 * ML_HEADS) & (lane < 4 * ML_HEADS))

    def gate_tables(g_ref, tri):
        g = g_ref[...] + bias_ref[...]
        lf = jnp.where(is_forget, _log_sigmoid(g), 0.0)
        cum = jnp.dot(tri.astype(F32), lf, preferred_element_type=F32, precision=lax.Precision.HIGHEST)
        tab = jnp.where(is_forget, cum, g)
        return tab, tab.T

    tab_f, tab_ft = gate_tables(gf_ref, lower)
    tab_b, tab_bt = gate_tables(gb_ref, upper)

    def chain(idx, q, k, v, bcol, igcol, brow, igrow, g_tot, valid):
        c = c_ref[idx]
        n = n_ref[idx][0:1, :]
        m_prev = m_ref[idx][0:1, 0:1]
        d = jnp.where(valid, bcol - brow + igrow, MASK_VALUE)
        inter = bcol + m_prev
        m_t = jnp.maximum(inter, jnp.max(d, axis=-1, keepdims=True))
        w = jnp.exp(d - m_t) * (_dot_nt(q, k) * k_scale)
        decay = jnp.exp(inter - m_t)
        qf = q.astype(F32)
        num = decay * _dot(q, c.astype(BF16)) + _dot(w.astype(BF16), v)
        den = decay * jnp.sum(qf * n, axis=-1, keepdims=True) + jnp.sum(w, axis=-1, keepdims=True)
        h = num / jnp.maximum(jnp.abs(den), jnp.exp(-m_t))
        a = g_tot - bcol + igcol
        m_new = jnp.maximum(g_tot + m_prev, jnp.max(a, axis=0, keepdims=True))
        carry = jnp.exp(g_tot + m_prev - m_new)
        wk = jnp.exp(a - m_new)
        wkv = (wk * v.astype(F32)).astype(BF16)
        c_ref[idx] = carry * c + _dot_tn(k, wkv) * k_scale
        n_new = carry * n + jnp.sum(wk * k.astype(F32), axis=0, keepdims=True) * k_scale
        n_ref[idx] = jnp.broadcast_to(n_new, n_ref.shape[1:])
        m_ref[idx] = jnp.broadcast_to(m_new, m_ref.shape[1:])
        return h

    for hd in range(ML_HEADS):
        sl = slice(hd * dh, (hd + 1) * dh)
        fi, ff = hd, ML_HEADS + hd
        h_f = chain(hd, qf_ref[:, sl], kf_ref[:, sl], vf_ref[:, sl],
                    tab_f[:, ff:ff + 1], tab_f[:, fi:fi + 1], tab_ft[ff:ff + 1, :], tab_ft[fi:fi + 1, :],
                    tab_f[L - 1:L, ff:ff + 1], lower)
        hf_ref[:, sl] = h_f.astype(hf_ref.dtype)
        bi, bf = 2 * ML_HEADS + hd, 3 * ML_HEADS + hd
        h_b = chain(ML_HEADS + hd, qb_ref[:, sl], kb_ref[:, sl], vb_ref[:, sl],
                    tab_b[:, bf:bf + 1], tab_b[:, bi:bi + 1], tab_bt[bf:bf + 1, :], tab_bt[bi:bi + 1, :],
                    tab_b[0:1, bf:bf + 1], upper)
        hb_ref[:, sl] = h_b.astype(hb_ref.dtype)


def _mlstm(p, gates, bias, *, batch, seq):
    t = p.shape[0]
    L = ML_CHUNK
    nc = seq // L
    w = ML_HEADS * ML_HEAD_DIM

    def fwd(cb):
        return lambda b, c: (b * nc + c, cb)

    def bwd(cb):
        return lambda b, c: (b * nc + nc - 1 - c, cb)

    return pl.pallas_call(
        _mlstm_kernel,
        grid=(batch, nc),
        in_specs=[
            pl.BlockSpec((L, w), fwd(0)), pl.BlockSpec((L, w), fwd(1)), pl.BlockSpec((L, w), fwd(2)),
            pl.BlockSpec((L, LANES), fwd(0)),
            pl.BlockSpec((L, w), bwd(0)), pl.BlockSpec((L, w), bwd(1)), pl.BlockSpec((L, w), bwd(2)),
            pl.BlockSpec((L, LANES), bwd(0)),
            pl.BlockSpec((1, LANES), lambda b, c: (0, 0)),
        ],
        out_specs=[pl.BlockSpec((L, w), fwd(0)), pl.BlockSpec((L, w), bwd(0))],
        out_shape=[jax.ShapeDtypeStruct((t, w), BF16), jax.ShapeDtypeStruct((t, w), BF16)],
        scratch_shapes=[
            pltpu.VMEM((2 * ML_HEADS, ML_HEAD_DIM, ML_HEAD_DIM), F32),
            pltpu.VMEM((2 * ML_HEADS, SUBLANES, ML_HEAD_DIM), F32),
            pltpu.VMEM((2 * ML_HEADS, SUBLANES, LANES), F32),
        ],
        compiler_params=_params("parallel", "arbitrary"),
        name="mlstm",
    )(p, p, p, gates, p, p, p, gates, bias)


def _na_bias_table(rpb):
    nh = rpb.shape[0]
    c = np.arange(GRID_W)
    c0 = np.clip(c - NA_KW // 2, 0, GRID_W - NA_KW)
    kc = np.arange(GRID_W)
    valid = (kc[None, :] >= c0[:, None]) & (kc[None, :] < c0[:, None] + NA_KW)
    pad = GRID_W - NA_KW
    rp = jnp.pad(rpb.astype(F32), ((0, 0), (0, 0), (pad, pad)))
    cols = jnp.stack([rp[:, :, GRID_W - 1 - ci:2 * GRID_W - 1 - ci] for ci in range(GRID_W)], axis=2)
    cols = jnp.where(valid[None, None], cols * LOG2_E, MASK_VALUE)
    tbl = jnp.stack([cols[:, NA_KH - 1 - rel:2 * NA_KH - 1 - rel] for rel in range(NA_KH)], axis=1)
    tbl = tbl.transpose(0, 1, 3, 2, 4)
    return tbl.reshape(nh, NA_KH, GRID_W, NA_KH * GRID_W)


def _na_kernel(q_ref, k_ref, v_ref, z_ref, tbl_ref, o_ref, *, rows):
    rb = pl.program_id(2)
    win = NA_KH * GRID_W
    lane_q = lax.broadcasted_iota(jnp.int32, (GRID_W, LANES), 1)
    head0_q = lane_q < NA_HEAD_DIM
    starts, scores = [], []
    for i in range(NA_ROWS_PER_STEP):
        r = rb * NA_ROWS_PER_STEP + i
        r0 = jnp.clip(r - NA_KH // 2, 0, rows - NA_KH)
        rel = r - r0
        start = pl.multiple_of(r0 * GRID_W, GRID_W)
        kw = k_ref[pl.ds(start, win), :]
        q = q_ref[i * GRID_W:(i + 1) * GRID_W, :]
        zero = jnp.zeros_like(q)
        qs = jnp.concatenate([jnp.where(head0_q, q, zero), jnp.where(head0_q, zero, q)], axis=0)
        bias = jnp.concatenate([tbl_ref[0, rel], tbl_ref[1, rel]], axis=0)
        scores.append(_dot_nt(qs, kw) * (NA_HEAD_DIM ** -0.5 * LOG2_E) + bias)
        starts.append(start)
    for i in range(NA_ROWS_PER_STEP):
        s = scores[i]
        e = jnp.exp2(s - jnp.max(s, axis=-1, keepdims=True))
        inv = 1.0 / jnp.sum(e, axis=-1, keepdims=True)
        o = _dot(e.astype(BF16), v_ref[pl.ds(starts[i], win), :]) * inv
        out = jnp.where(head0_q, o[:GRID_W], o[GRID_W:])
        zz = z_ref[i * GRID_W:(i + 1) * GRID_W, :].astype(F32)
        o_ref[i * GRID_W:(i + 1) * GRID_W, :] = (out * _silu(zz)).astype(o_ref.dtype)


def _natten(p, tbl, *, batch, seq, col0):
    t = p.shape[0]
    rows = seq // GRID_W
    width = NA_HEADS * NA_HEAD_DIM
    pairs = width // LANES
    tq = NA_ROWS_PER_STEP * GRID_W
    nrb = seq // tq
    cb = col0 // LANES
    kern = functools.partial(_na_kernel, rows=rows)
    return pl.pallas_call(
        kern,
        grid=(batch, pairs, nrb),
        in_specs=[
            pl.BlockSpec((tq, LANES), lambda b, h, r: (b * nrb + r, cb + h)),
            pl.BlockSpec((seq, LANES), lambda b, h, r: (b, cb + pairs + h)),
            pl.BlockSpec((seq, LANES), lambda b, h, r: (b, cb + 2 * pairs + h)),
            pl.BlockSpec((tq, LANES), lambda b, h, r: (b * nrb + r, cb + 3 * pairs + h)),
            pl.BlockSpec((2, NA_KH, GRID_W, NA_KH * GRID_W), lambda b, h, r: (h, 0, 0, 0)),
        ],
        out_specs=pl.BlockSpec((tq, LANES), lambda b, h, r: (b * nrb + r, h)),
        out_shape=jax.ShapeDtypeStruct((t, width), BF16),
        compiler_params=_params("parallel", "parallel", "arbitrary"),
        name="natten",
    )(p, p, p, p, tbl)


def _even_out_kernel(hf_ref, hb_ref, o_ref, z_ref, na_ref, g_ref, w_ref, x_ref, out_ref):
    h = hf_ref[...].astype(F32) + hb_ref[...].astype(F32)
    hn = jnp.concatenate(
        [_rms(h[:, i * ML_HEAD_DIM:(i + 1) * ML_HEAD_DIM]) for i in range(ML_HEADS)], axis=-1)
    a = hn * g_ref[...] * _sigmoid(o_ref[...].astype(F32)) * _silu(z_ref[...].astype(F32))
    wa = ML_HEADS * ML_HEAD_DIM
    y = _dot(a.astype(BF16), w_ref[:wa, :]) + _dot(na_ref[...], w_ref[wa:, :])
    out_ref[...] = x_ref[...] + y


def _even_out(hf, hb, p, na, g, w, x, *, tm):
    t, d = x.shape
    wa = hf.shape[1]
    return pl.pallas_call(
        _even_out_kernel,
        grid=(t // tm,),
        in_specs=[
            pl.BlockSpec((tm, wa), lambda i: (i, 0)),
            pl.BlockSpec((tm, wa), lambda i: (i, 0)),
            pl.BlockSpec((tm, wa), lambda i: (i, 3)),
            pl.BlockSpec((tm, wa), lambda i: (i, 4)),
            pl.BlockSpec((tm, na.shape[1]), lambda i: (i, 0)),
            pl.BlockSpec((1, wa), lambda i: (0, 0)),
            pl.BlockSpec(w.shape, lambda i: (0, 0)),
            pl.BlockSpec((tm, d), lambda i: (i, 0)),
        ],
        out_specs=pl.BlockSpec((tm, d), lambda i: (i, 0)),
        out_shape=jax.ShapeDtypeStruct((t, d), F32),
        compiler_params=_params("parallel"),
        name="even_out_proj",
    )(hf, hb, p, p, na, g, w, x)


def _gqa_kernel(q_ref, k_ref, v_ref, o_ref, vaug_ref, m_ref, acc_ref, *, tk):
    tq = q_ref.shape[0]
    seq = k_ref.shape[0]
    dh = GQ_HEAD_DIM

    @pl.when(pl.program_id(2) == 0)
    def _():
        vaug_ref[:, :dh] = v_ref[...]
        vaug_ref[:, dh:] = jnp.ones((seq, dh), BF16)

    q = q_ref[...]
    qs = jnp.concatenate([q[:, g * dh:(g + 1) * dh] for g in range(GQ_GROUP)], axis=0)
    m_ref[...] = jnp.full_like(m_ref, -jnp.inf)
    acc_ref[...] = jnp.zeros_like(acc_ref)

    def scores(c):
        return _dot_nt(qs, k_ref[c * tk:(c + 1) * tk, :])

    def update(s, c):
        m_prev = m_ref[...]
        m_new = jnp.maximum(m_prev, jnp.max(s, axis=-1, keepdims=True))
        alpha = jnp.exp2(m_prev - m_new)
        p = jnp.exp2(s - jnp.concatenate([m_new] * (tk // LANES), axis=1))
        pv = _dot(p.astype(BF16), vaug_ref[c * tk:(c + 1) * tk, :])
        acc_ref[...] = jnp.concatenate([alpha, alpha], axis=1) * acc_ref[...] + pv
        m_ref[...] = m_new

    nk = seq // tk
    s_next = scores(0)
    for c in range(nk):
        s_cur = s_next
        if c + 1 < nk:
            s_next = scores(c + 1)
        update(s_cur, c)
    acc = acc_ref[...]
    out = acc[:, :dh] / acc[:, dh:]
    o_ref[...] = jnp.concatenate(
        [out[g * tq:(g + 1) * tq] for g in range(GQ_GROUP)], axis=-1).astype(o_ref.dtype)


def _gqa(p, *, batch, seq, k_col, v_col):
    t = p.shape[0]
    tq, tk = GQ_BLOCK_Q, GQ_BLOCK_K
    nq = seq // tq
    gw = GQ_GROUP * GQ_HEAD_DIM
    kb, vb = k_col // GQ_HEAD_DIM, v_col // GQ_HEAD_DIM
    kern = functools.partial(_gqa_kernel, tk=tk)
    return pl.pallas_call(
        kern,
        grid=(batch, GQ_KV_HEADS, nq),
        in_specs=[
            pl.BlockSpec((tq, gw), lambda b, h, i: (b * nq + i, h)),
            pl.BlockSpec((seq, GQ_HEAD_DIM), lambda b, h, i: (b, kb + h)),
            pl.BlockSpec((seq, GQ_HEAD_DIM), lambda b, h, i: (b, vb + h)),
        ],
        out_specs=pl.BlockSpec((tq, gw), lambda b, h, i: (b * nq + i, h)),
        out_shape=jax.ShapeDtypeStruct((t, GQ_HEADS * GQ_HEAD_DIM), BF16),
        scratch_shapes=[
            pltpu.VMEM((seq, 2 * GQ_HEAD_DIM), BF16),
            pltpu.VMEM((GQ_GROUP * tq, LANES), F32),
            pltpu.VMEM((GQ_GROUP * tq, 2 * GQ_HEAD_DIM), F32),
        ],
        compiler_params=_params("parallel", "parallel", "arbitrary"),
        name="gqa_attention",
    )(p, p, p)


def _odd_out_kernel(att_ref, gz_ref, cb_ref, cc_ref, cx_ref, cz_ref, ccp_ref, cxp_ref, ccn_ref, cxn_ref,
                    cw_ref, cbias_ref, w_ref, h_ref, fg_ref, out_ref, *, blocks_per_seq):
    tm = att_ref.shape[0]
    i = pl.program_id(0) % blocks_per_seq
    xc = cc_ref[...].astype(F32) * cx_ref[...].astype(F32)
    prev_row = ccp_ref[SUBLANES - 1:SUBLANES, :].astype(F32) * cxp_ref[SUBLANES - 1:SUBLANES, :].astype(F32)
    next_row = ccn_ref[0:1, :].astype(F32) * cxn_ref[0:1, :].astype(F32)
    prev_row = jnp.where(i == 0, 0.0, prev_row)
    next_row = jnp.where(i == blocks_per_seq - 1, 0.0, next_row)
    ridx = lax.broadcasted_iota(jnp.int32, xc.shape, 0)
    x_prev = jnp.where(ridx == 0, prev_row, pltpu.roll(xc, 1, 0))
    x_next = jnp.where(ridx == tm - 1, next_row, pltpu.roll(xc, tm - 1, 0))
    y = x_prev * cw_ref[0:1, :] + xc * cw_ref[1:2, :] + x_next * cw_ref[2:3, :] + cbias_ref[...]
    conv = cb_ref[...].astype(F32) * y * _silu(cz_ref[...].astype(F32))
    att = att_ref[...].astype(F32) * _silu(gz_ref[...].astype(F32))
    wa = att_ref.shape[1]
    hh = h_ref[...] + _dot(att.astype(BF16), w_ref[:wa, :]) + _dot(conv.astype(BF16), w_ref[wa:, :])
    out_ref[...] = _rms(hh) * fg_ref[...]


def _odd_out(att, p, cw, cbias, w, h, fg, *, tm, seq, cols):
    t, d = h.shape
    bps = seq // tm
    sub = tm // SUBLANES
    nsub = t // SUBLANES
    z_c, b_c, c_c, x_c, cz_c = cols

    def blk(cb):
        return pl.BlockSpec((tm, d), lambda i: (i, cb))

    def halo_prev(cb):
        return pl.BlockSpec((SUBLANES, d), lambda i: (jnp.maximum(i * sub - 1, 0), cb))

    def halo_next(cb):
        return pl.BlockSpec((SUBLANES, d), lambda i: (jnp.minimum((i + 1) * sub, nsub - 1), cb))

    kern = functools.partial(_odd_out_kernel, blocks_per_seq=bps)
    return pl.pallas_call(
        kern,
        grid=(t // tm,),
        in_specs=[
            pl.BlockSpec((tm, d), lambda i: (i, 0)),
            blk(z_c), blk(b_c), blk(c_c), blk(x_c), blk(cz_c),
            halo_prev(c_c), halo_prev(x_c), halo_next(c_c), halo_next(x_c),
            pl.BlockSpec(cw.shape, lambda i: (0, 0)),
            pl.BlockSpec((1, d), lambda i: (0, 0)),
            pl.BlockSpec(w.shape, lambda i: (0, 0)),
            pl.BlockSpec((tm, d), lambda i: (i, 0)),
            pl.BlockSpec((1, d), lambda i: (0, 0)),
        ],
        out_specs=pl.BlockSpec((tm, d), lambda i: (i, 0)),
        out_shape=jax.ShapeDtypeStruct((t, d), F32),
        compiler_params=_params("parallel"),
        name="odd_out_proj",
    )(att, p, p, p, p, p, p, p, p, p, cw, cbias, w, h, fg)


def _rope_tables(seq, heads):
    t = jnp.arange(seq)
    pos = jnp.stack([t // GRID_W, t % GRID_W], axis=-1).astype(F32)
    inv = ROPE_THETA ** (-jnp.arange(ROPE_HALF, dtype=F32) / ROPE_HALF)
    ang = pos[:, :, None] * inv
    cos, sin = jnp.cos(ang), jnp.sin(ang)
    cos_l = jnp.stack([cos, cos], axis=2).reshape(seq, GQ_HEAD_DIM)
    sin_l = jnp.stack([-sin, sin], axis=2).reshape(seq, GQ_HEAD_DIM)
    return jnp.tile(cos_l, (1, heads)), jnp.tile(sin_l, (1, heads))


def kernel(x, norm_g, final_g, ev_w_in, ev_gate_b, ev_w_out, ev_ml_norm_g, ev_na_rpb,
           od_w_in, od_w_out, od_q_norm_g, od_k_norm_g, od_conv_w, od_conv_b):
    batch, seq, d = x.shape
    assert d == D_MODEL and norm_g.shape[0] == 2 and seq % max(ML_CHUNK, 512) == 0
    t = batch * seq
    xf = x.reshape(t, d)
    tm_in = min(1024, seq)
    tm_out = 512

    n_gate = 4 * ML_HEADS
    g0 = 5 * D_MODEL
    w_in = ev_w_in[0]
    w_main = jnp.concatenate([w_in[:, :g0], w_in[:, g0 + n_gate:]], axis=1).astype(BF16)
    w_gate = jnp.pad(w_in[:, g0:g0 + n_gate], ((0, 0), (0, LANES - n_gate))).astype(BF16)
    p_ev, gates = _even_in_proj(xf, norm_g[0][None, :], w_main, w_gate, tm=tm_in, tn=1024)
    gate_bias = jnp.pad(ev_gate_b[0], (0, LANES - n_gate))[None, :]
    hf, hb = _mlstm(p_ev, gates, gate_bias, batch=batch, seq=seq)
    na = _natten(p_ev, _na_bias_table(ev_na_rpb[0]), batch=batch, seq=seq, col0=g0)
    h1 = _even_out(hf, hb, p_ev, na, ev_ml_norm_g[0][None, :], ev_w_out[0].astype(BF16), xf, tm=tm_out)

    w_in = od_w_in[0]
    qw, kvw = GQ_HEADS * GQ_HEAD_DIM, GQ_KV_HEADS * GQ_HEAD_DIM
    w_perm = jnp.concatenate([w_in[:, :qw], w_in[:, qw + 2 * kvw:], w_in[:, qw:qw + 2 * kvw]], axis=1).astype(BF16)
    tn_odd = 512
    cos, sin = _rope_tables(seq, tn_odd // GQ_HEAD_DIM)
    qg = jnp.tile(od_q_norm_g[0], tn_odd // GQ_HEAD_DIM)[None, :]
    kg = jnp.tile(od_k_norm_g[0], GQ_KV_HEADS)[None, :]
    p_od = _odd_in_proj(h1, norm_g[1][None, :], w_perm, qg, kg, cos, sin, tm=tm_in, tn=tn_odd, seq=seq)
    k_col = w_perm.shape[1] - 2 * kvw
    att = _gqa(p_od, batch=batch, seq=seq, k_col=k_col, v_col=k_col + kvw)
    out = _odd_out(att, p_od, od_conv_w[0], od_conv_b[0][None, :], od_w_out[0].astype(BF16), h1,
                   final_g[None, :], tm=tm_out, seq=seq, cols=(1, 2, 3, 4, 5))
    return out.reshape(batch, seq, d)
```

```python
import functools

import jax
import jax.numpy as jnp
import numpy as np
from jax import lax
from jax.experimental import pallas as pl
from jax.experimental.pallas import tpu as pltpu

F32 = jnp.float32
BF16 = jnp.bfloat16

D_MODEL = 1024
GRID_W = 64
RMS_EPS = 1e-6

ML_HEADS = 4
ML_HEAD_DIM = 256
ML_CHUNK = 256

NA_HEADS = 16
NA_HEAD_DIM = 64
NA_KH = 8
NA_KW = 16
NA_ROWS_PER_STEP = 8

GQ_HEADS = 8
GQ_HEAD_DIM = 128
GQ_KV_HEADS = 2
GQ_GROUP = GQ_HEADS // GQ_KV_HEADS
ROPE_THETA = 10000.0
ROPE_HALF = GQ_HEAD_DIM // 4
GQ_BLOCK_Q = 128
GQ_BLOCK_K = 512

CV_K = 3

LANES = 128
SUBLANES = 8
MASK_VALUE = -1e30
LOG2_E = 1.4426950408889634

VMEM_LIMIT = 56 * 1024 * 1024


def _params(*sem):
    return pltpu.CompilerParams(dimension_semantics=sem, vmem_limit_bytes=VMEM_LIMIT)


def _sigmoid(x):
    return 1.0 / (1.0 + jnp.exp(-x))


def _silu(x):
    return x * _sigmoid(x)


def _log_sigmoid(x):
    return jnp.minimum(x, 0.0) - jnp.log1p(jnp.exp(-jnp.abs(x)))


def _rms(x, eps=RMS_EPS):
    return x * lax.rsqrt(jnp.mean(x * x, axis=-1, keepdims=True) + eps)


def _dot(a, b):
    return jnp.dot(a, b, preferred_element_type=F32)


def _dot_nt(a, b):
    return lax.dot_general(a, b, (((1,), (1,)), ((), ())), preferred_element_type=F32)


def _dot_tn(a, b):
    return lax.dot_general(a, b, (((0,), (0,)), ((), ())), preferred_element_type=F32)


def _even_in_kernel(x_ref, g_ref, w_ref, wg_ref, o_ref, og_ref, u_ref):
    @pl.when(pl.program_id(1) == 0)
    def _():
        u = (_rms(x_ref[...]) * g_ref[...]).astype(BF16)
        u_ref[...] = u
        og_ref[...] = _dot(u, wg_ref[...])

    o_ref[...] = _dot(u_ref[...], w_ref[...]).astype(o_ref.dtype)


def _even_in_proj(x, g, w, *, tm, tn):
    t, d = x.shape
    n = w.shape[1] - LANES
    gate_block = n // LANES
    return pl.pallas_call(
        _even_in_kernel,
        grid=(t // tm, n // tn),
        in_specs=[
            pl.BlockSpec((tm, d), lambda i, j: (i, 0)),
            pl.BlockSpec((1, d), lambda i, j: (0, 0)),
            pl.BlockSpec((d, tn), lambda i, j: (0, j)),
            pl.BlockSpec((d, LANES), lambda i, j: (0, gate_block)),
        ],
        out_specs=[
            pl.BlockSpec((tm, tn), lambda i, j: (i, j)),
            pl.BlockSpec((tm, LANES), lambda i, j: (i, 0)),
        ],
        out_shape=[
            jax.ShapeDtypeStruct((t, n), BF16),
            jax.ShapeDtypeStruct((t, LANES), F32),
        ],
        scratch_shapes=[pltpu.VMEM((tm, d), BF16)],
        compiler_params=_params("parallel", "arbitrary"),
        name="even_in_proj",
    )(x, g, w, w)


def _rope_norm(acc, acc_sw, gain, gain_sw, cos, sin, scale):
    nh = acc.shape[1] // GQ_HEAD_DIM
    heads = []
    for h in range(nh):
        sl = slice(h * GQ_HEAD_DIM, (h + 1) * GQ_HEAD_DIM)
        a = acc[:, sl]
        r = lax.rsqrt(jnp.mean(a * a, axis=-1, keepdims=True) + RMS_EPS)
        y = a * r * gain[:, sl]
        partner = acc_sw[:, sl] * r * gain_sw[:, sl]
        heads.append((y * cos + partner * sin) * scale)
    return jnp.concatenate(heads, axis=-1)


def _odd_in_kernel(x_ref, g_ref, w_ref, wsw_ref, qg_ref, qgs_ref, kg_ref, kgs_ref, cos_ref, sin_ref,
                   o_ref, u_ref, *, n_q_tiles, kv_tile):
    j = pl.program_id(1)

    @pl.when(j == 0)
    def _():
        u_ref[...] = (_rms(x_ref[...]) * g_ref[...]).astype(BF16)

    is_q = j < n_q_tiles
    is_kv = j == kv_tile

    @pl.when(is_q)
    def _():
        u = u_ref[...]
        o_ref[...] = _rope_norm(_dot(u, w_ref[...]), _dot(u, wsw_ref[...]), qg_ref[...], qgs_ref[...],
                                cos_ref[...], sin_ref[...], GQ_HEAD_DIM ** -0.5 * LOG2_E).astype(o_ref.dtype)

    @pl.when(is_kv)
    def _():
        kw = GQ_KV_HEADS * GQ_HEAD_DIM
        u = u_ref[...]
        acc = _dot(u, w_ref[...])
        k = _rope_norm(acc[:, :kw], _dot(u, wsw_ref[:, :kw]), kg_ref[...], kgs_ref[...],
                       cos_ref[...], sin_ref[...], 1.0)
        o_ref[...] = jnp.concatenate([k, acc[:, kw:]], axis=-1).astype(o_ref.dtype)

    @pl.when(jnp.logical_not(jnp.logical_or(is_q, is_kv)))
    def _():
        o_ref[...] = _dot(u_ref[...], w_ref[...]).astype(o_ref.dtype)


def _odd_in_proj(x, g, w, wsw, qg, qgs, kg, kgs, cos, sin, *, tm, tn, seq):
    t, d = x.shape
    n = w.shape[1]
    n_q_tiles = (GQ_HEADS * GQ_HEAD_DIM) // tn
    kv_tile = n // tn - 1
    kvw = GQ_KV_HEADS * GQ_HEAD_DIM
    sb = seq // tm
    kern = functools.partial(_odd_in_kernel, n_q_tiles=n_q_tiles, kv_tile=kv_tile)

    def wsw_map(i, j):
        return (0, jnp.where(j < n_q_tiles, j, jnp.where(j == kv_tile, n_q_tiles, n_q_tiles - 1)))

    return pl.pallas_call(
        kern,
        grid=(t // tm, n // tn),
        in_specs=[
            pl.BlockSpec((tm, d), lambda i, j: (i, 0)),
            pl.BlockSpec((1, d), lambda i, j: (0, 0)),
            pl.BlockSpec((d, tn), lambda i, j: (0, j)),
            pl.BlockSpec((d, tn), wsw_map),
            pl.BlockSpec((1, tn), lambda i, j: (0, 0)),
            pl.BlockSpec((1, tn), lambda i, j: (0, 0)),
            pl.BlockSpec((1, kvw), lambda i, j: (0, 0)),
            pl.BlockSpec((1, kvw), lambda i, j: (0, 0)),
            pl.BlockSpec((tm, GQ_HEAD_DIM), lambda i, j: (i % sb, 0)),
            pl.BlockSpec((tm, GQ_HEAD_DIM), lambda i, j: (i % sb, 0)),
        ],
        out_specs=pl.BlockSpec((tm, tn), lambda i, j: (i, j)),
        out_shape=jax.ShapeDtypeStruct((t, n), BF16),
        scratch_shapes=[pltpu.VMEM((tm, d), BF16)],
        compiler_params=_params("parallel", "arbitrary"),
        name="odd_in_proj",
    )(x, g, w, wsw, qg, qgs, kg, kgs, cos, sin)


def _mlstm_kernel(qf_ref, kf_ref, vf_ref, gf_ref, qb_ref, kb_ref, vb_ref, gb_ref, bias_ref,
                  hf_ref, hb_ref, c_ref, n_ref, m_ref):
    L = qf_ref.shape[0]
    dh = ML_HEAD_DIM
    k_scale = dh ** -0.5

    @pl.when(pl.program_id(1) == 0)
    def _():
        c_ref[...] = jnp.zeros_like(c_ref)
        n_ref[...] = jnp.zeros_like(n_ref)
        m_ref[...] = jnp.zeros_like(m_ref)

    row = lax.broadcasted_iota(jnp.int32, (L, L), 0)
    col = lax.broadcasted_iota(jnp.int32, (L, L), 1)
    lower = row >= col
    upper = row <= col
    lane = lax.broadcasted_iota(jnp.int32, (L, LANES), 1)
    is_forget = ((lane >= ML_HEADS) & (lane < 2 * ML_HEADS)) | ((lane >= 3 * ML_HEADS) & (lane < 4 * ML_HEADS))

    def gate_tables(g_ref, tri):
        g = g_ref[...] + bias_ref[...]
        lf = jnp.where(is_forget, _log_sigmoid(g), 0.0)
        cum = jnp.dot(tri.astype(F32), lf, preferred_element_type=F32, precision=lax.Precision.HIGHEST)
        tab = jnp.where(is_forget, cum, g)
        return tab, tab.T

    tab_f, tab_ft = gate_tables(gf_ref, lower)
    tab_b, tab_bt = gate_tables(gb_ref, upper)

    def chain(idx, q, k, v, bcol, igcol, brow, igrow, g_tot, valid):
        c = c_ref[idx]
        n = n_ref[idx][0:1, :]
        m_prev = m_ref[idx][0:1, 0:1]
        d = jnp.where(valid, bcol - brow + igrow, MASK_VALUE)
        inter = bcol + m_prev
        m_t = jnp.maximum(inter, jnp.max(d, axis=-1, keepdims=True))
        w = jnp.exp(d - m_t) * (_dot_nt(q, k) * k_scale)
        decay = jnp.exp(inter - m_t)
        qf = q.astype(F32)
        num = decay * _dot(q, c.astype(BF16)) + _dot(w.astype(BF16), v)
        den = decay * jnp.sum(qf * n, axis=-1, keepdims=True) + jnp.sum(w, axis=-1, keepdims=True)
        h = num / jnp.maximum(jnp.abs(den), jnp.exp(-m_t))
        a = g_tot - bcol + igcol
        m_new = jnp.maximum(g_tot + m_prev, jnp.max(a, axis=0, keepdims=True))
        carry = jnp.exp(g_tot + m_prev - m_new)
        wk = jnp.exp(a - m_new)
        wkv = (wk * v.astype(F32)).astype(BF16)
        c_ref[idx] = carry * c + _dot_tn(k, wkv) * k_scale
        n_new = carry * n + jnp.sum(wk * k.astype(F32), axis=0, keepdims=True) * k_scale
        n_ref[idx] = jnp.broadcast_to(n_new, n_ref.shape[1:])
        m_ref[idx] = jnp.broadcast_to(m_new, m_ref.shape[1:])
        return h

    for hd in range(ML_HEADS):
        sl = slice(hd * dh, (hd + 1) * dh)
        fi, ff = hd, ML_HEADS + hd
        h_f = chain(hd, qf_ref[:, sl], kf_ref[:, sl], vf_ref[:, sl],
                    tab_f[:, ff:ff + 1], tab_f[:, fi:fi + 1], tab_ft[ff:ff + 1, :], tab_ft[fi:fi + 1, :],
                    tab_f[L - 1:L, ff:ff + 1], lower)
        hf_ref[:, sl] = h_f.astype(hf_ref.dtype)
        bi, bf = 2 * ML_HEADS + hd, 3 * ML_HEADS + hd
        h_b = chain(ML_HEADS + hd, qb_ref[:, sl], kb_ref[:, sl], vb_ref[:, sl],
                    tab_b[:, bf:bf + 1], tab_b[:, bi:bi + 1], tab_bt[bf:bf + 1, :], tab_bt[bi:bi + 1, :],
                    tab_b[0:1, bf:bf + 1], upper)
        hb_ref[:, sl] = h_b.astype(hb_ref.dtype)


def _mlstm(p, gates, bias, *, batch, seq):
    t = p.shape[0]
    L = ML_CHUNK
    nc = seq // L
    w = ML_HEADS * ML_HEAD_DIM

    def fwd(cb):
        return lambda b, c: (b * nc + c, cb)

    def bwd(cb):
        return lambda b, c: (b * nc + nc - 1 - c, cb)

    return pl.pallas_call(
        _mlstm_kernel,
        grid=(batch, nc),
        in_specs=[
            pl.BlockSpec((L, w), fwd(0)), pl.BlockSpec((L, w), fwd(1)), pl.BlockSpec((L, w), fwd(2)),
            pl.BlockSpec((L, LANES), fwd(0)),
            pl.BlockSpec((L, w), bwd(0)), pl.BlockSpec((L, w), bwd(1)), pl.BlockSpec((L, w), bwd(2)),
            pl.BlockSpec((L, LANES), bwd(0)),
            pl.BlockSpec((1, LANES), lambda b, c: (0, 0)),
        ],
        out_specs=[pl.BlockSpec((L, w), fwd(0)), pl.BlockSpec((L, w), bwd(0))],
        out_shape=[jax.ShapeDtypeStruct((t, w), BF16), jax.ShapeDtypeStruct((t, w), BF16)],
        scratch_shapes=[
            pltpu.VMEM((2 * ML_HEADS, ML_HEAD_DIM, ML_HEAD_DIM), F32),
            pltpu.VMEM((2 * ML_HEADS, SUBLANES, ML_HEAD_DIM), F32),
            pltpu.VMEM((2 * ML_HEADS, SUBLANES, LANES), F32),
        ],
        compiler_params=_params("parallel", "arbitrary"),
        name="mlstm",
    )(p, p, p, gates, p, p, p, gates, bias)


def _na_bias_table(rpb):
    nh = rpb.shape[0]
    c = np.arange(GRID_W)
    c0 = np.clip(c - NA_KW // 2, 0, GRID_W - NA_KW)
    kc = np.arange(GRID_W)
    valid = (kc[None, :] >= c0[:, None]) & (kc[None, :] < c0[:, None] + NA_KW)
    pad = GRID_W - NA_KW
    rp = jnp.pad(rpb.astype(F32) * LOG2_E, ((0, 0), (0, 0), (pad, pad)))
    cols = jnp.stack([rp[:, :, GRID_W - 1 - ci:2 * GRID_W - 1 - ci] for ci in range(GRID_W)], axis=1)
    cols = jnp.where(valid[None, :, None, :], cols, MASK_VALUE)
    return jnp.stack(
        [cols[:, :, NA_KH - 1 - rel:2 * NA_KH - 1 - rel, :].reshape(nh, GRID_W, NA_KH * GRID_W)
         for rel in range(NA_KH)], axis=1)


def _na_kernel(q_ref, k_ref, v_ref, z_ref, tbl_ref, o_ref, *, rows):
    rb = pl.program_id(2)
    win = NA_KH * GRID_W
    lane_q = lax.broadcasted_iota(jnp.int32, (GRID_W, LANES), 1)
    head0_q = lane_q < NA_HEAD_DIM
    starts, scores = [], []
    for i in range(NA_ROWS_PER_STEP):
        r = rb * NA_ROWS_PER_STEP + i
        r0 = jnp.clip(r - NA_KH // 2, 0, rows - NA_KH)
        rel = r - r0
        start = pl.multiple_of(r0 * GRID_W, GRID_W)
        kw = k_ref[pl.ds(start, win), :]
        q = q_ref[i * GRID_W:(i + 1) * GRID_W, :]
        zero = jnp.zeros_like(q)
        qs = jnp.concatenate([jnp.where(head0_q, q, zero), jnp.where(head0_q, zero, q)], axis=0)
        bias = jnp.concatenate([tbl_ref[0, rel], tbl_ref[1, rel]], axis=0)
        scores.append(_dot_nt(qs, kw) * (NA_HEAD_DIM ** -0.5 * LOG2_E) + bias)
        starts.append(start)
    for i in range(NA_ROWS_PER_STEP):
        s = scores[i]
        e = jnp.exp2(s - jnp.max(s, axis=-1, keepdims=True))
        inv = 1.0 / jnp.sum(e, axis=-1, keepdims=True)
        o = _dot(e.astype(BF16), v_ref[pl.ds(starts[i], win), :]) * inv
        out = jnp.where(head0_q, o[:GRID_W], o[GRID_W:])
        zz = z_ref[i * GRID_W:(i + 1) * GRID_W, :].astype(F32)
        o_ref[i * GRID_W:(i + 1) * GRID_W, :] = (out * _silu(zz)).astype(o_ref.dtype)


def _natten(p, tbl, *, batch, seq, col0):
    t = p.shape[0]
    rows = seq // GRID_W
    width = NA_HEADS * NA_HEAD_DIM
    pairs = width // LANES
    tq = NA_ROWS_PER_STEP * GRID_W
    nrb = seq // tq
    cb = col0 // LANES
    kern = functools.partial(_na_kernel, rows=rows)
    return pl.pallas_call(
        kern,
        grid=(batch, pairs, nrb),
        in_specs=[
            pl.BlockSpec((tq, LANES), lambda b, h, r: (b * nrb + r, cb + h)),
            pl.BlockSpec((seq, LANES), lambda b, h, r: (b, cb + pairs + h)),
            pl.BlockSpec((seq, LANES), lambda b, h, r: (b, cb + 2 * pairs + h)),
            pl.BlockSpec((tq, LANES), lambda b, h, r: (b * nrb + r, cb + 3 * pairs + h)),
            pl.BlockSpec((2, NA_KH, GRID_W, NA_KH * GRID_W), lambda b, h, r: (h, 0, 0, 0)),
        ],
        out_specs=pl.BlockSpec((tq, LANES), lambda b, h, r: (b * nrb + r, h)),
        out_shape=jax.ShapeDtypeStruct((t, width), BF16),
        compiler_params=_params("parallel", "parallel", "arbitrary"),
        name="natten",
    )(p, p, p, p, tbl)


def _even_out_kernel(hf_ref, hb_ref, o_ref, z_ref, na_ref, g_ref, w_ref, x_ref, out_ref):
    h = hf_ref[...].astype(F32) + hb_ref[...].astype(F32)
    hn = jnp.concatenate(
        [_rms(h[:, i * ML_HEAD_DIM:(i + 1) * ML_HEAD_DIM]) for i in range(ML_HEADS)], axis=-1)
    a = hn * g_ref[...] * _sigmoid(o_ref[...].astype(F32)) * _silu(z_ref[...].astype(F32))
    wa = ML_HEADS * ML_HEAD_DIM
    y = _dot(a.astype(BF16), w_ref[:wa, :]) + _dot(na_ref[...], w_ref[wa:, :])
    out_ref[...] = x_ref[...] + y


def _even_out(hf, hb, p, na, g, w, x, *, tm):
    t, d = x.shape
    wa = hf.shape[1]
    return pl.pallas_call(
        _even_out_kernel,
        grid=(t // tm,),
        in_specs=[
            pl.BlockSpec((tm, wa), lambda i: (i, 0)),
            pl.BlockSpec((tm, wa), lambda i: (i, 0)),
            pl.BlockSpec((tm, wa), lambda i: (i, 3)),
            pl.BlockSpec((tm, wa), lambda i: (i, 4)),
            pl.BlockSpec((tm, na.shape[1]), lambda i: (i, 0)),
            pl.BlockSpec((1, wa), lambda i: (0, 0)),
            pl.BlockSpec(w.shape, lambda i: (0, 0)),
            pl.BlockSpec((tm, d), lambda i: (i, 0)),
        ],
        out_specs=pl.BlockSpec((tm, d), lambda i: (i, 0)),
        out_shape=jax.ShapeDtypeStruct((t, d), F32),
        compiler_params=_params("parallel"),
        name="even_out_proj",
    )(hf, hb, p, p, na, g, w, x)


def _gqa_kernel(q_ref, k_ref, v_ref, o_ref, vaug_ref, m_ref, acc_ref, *, tk):
    tq = q_ref.shape[0]
    seq = k_ref.shape[0]
    dh = GQ_HEAD_DIM

    @pl.when(pl.program_id(2) == 0)
    def _():
        vaug_ref[:, :dh] = v_ref[...]
        vaug_ref[:, dh:] = jnp.ones((seq, dh), BF16)

    q = q_ref[...]
    qs = jnp.concatenate([q[:, g * dh:(g + 1) * dh] for g in range(GQ_GROUP)], axis=0)
    m_ref[...] = jnp.full_like(m_ref, -jnp.inf)
    acc_ref[...] = jnp.zeros_like(acc_ref)

    def scores(c):
        return _dot_nt(qs, k_ref[c * tk:(c + 1) * tk, :])

    def update(s, c):
        m_prev = m_ref[...]
        m_new = jnp.maximum(m_prev, jnp.max(s, axis=-1, keepdims=True))
        alpha = jnp.exp2(m_prev - m_new)
        p = jnp.exp2(s - jnp.concatenate([m_new] * (tk // LANES), axis=1))
        pv = _dot(p.astype(BF16), vaug_ref[c * tk:(c + 1) * tk, :])
        acc_ref[...] = jnp.concatenate([alpha, alpha], axis=1) * acc_ref[...] + pv
        m_ref[...] = m_new

    nk = seq // tk
    s_next = scores(0)
    for c in range(nk):
        s_cur = s_next
        if c + 1 < nk:
            s_next = scores(c + 1)
        update(s_cur, c)
    acc = acc_ref[...]
    out = acc[:, :dh] / acc[:, dh:]
    o_ref[...] = jnp.concatenate(
        [out[g * tq:(g + 1) * tq] for g in range(GQ_GROUP)], axis=-1).astype(o_ref.dtype)


def _gqa(p, *, batch, seq, k_col, v_col):
    t = p.shape[0]
    tq, tk = GQ_BLOCK_Q, GQ_BLOCK_K
    nq = seq // tq
    gw = GQ_GROUP * GQ_HEAD_DIM
    kb, vb = k_col // GQ_HEAD_DIM, v_col // GQ_HEAD_DIM
    kern = functools.partial(_gqa_kernel, tk=tk)
    return pl.pallas_call(
        kern,
        grid=(batch, GQ_KV_HEADS, nq),
        in_specs=[
            pl.BlockSpec((tq, gw), lambda b, h, i: (b * nq + i, h)),
            pl.BlockSpec((seq, GQ_HEAD_DIM), lambda b, h, i: (b, kb + h)),
            pl.BlockSpec((seq, GQ_HEAD_DIM), lambda b, h, i: (b, vb + h)),
        ],
        out_specs=pl.BlockSpec((tq, gw), lambda b, h, i: (b * nq + i, h)),
        out_shape=jax.ShapeDtypeStruct((t, GQ_HEADS * GQ_HEAD_DIM), BF16),
        scratch_shapes=[
            pltpu.VMEM((seq, 2 * GQ_HEAD_DIM), BF16),
            pltpu.VMEM((GQ_GROUP * tq, LANES), F32),
            pltpu.VMEM((GQ_GROUP * tq, 2 * GQ_HEAD_DIM), F32),
        ],
        compiler_params=_params("parallel", "parallel", "arbitrary"),
        name="gqa_attention",
    )(p, p, p)


def _odd_out_kernel(att_ref, gz_ref, cb_ref, cc_ref, cx_ref, cz_ref, ccp_ref, cxp_ref, ccn_ref, cxn_ref,
                    cw_ref, cbias_ref, w_ref, h_ref, fg_ref, out_ref, *, blocks_per_seq):
    tm = att_ref.shape[0]
    i = pl.program_id(0) % blocks_per_seq
    xc = cc_ref[...].astype(F32) * cx_ref[...].astype(F32)
    prev_row = ccp_ref[SUBLANES - 1:SUBLANES, :].astype(F32) * cxp_ref[SUBLANES - 1:SUBLANES, :].astype(F32)
    next_row = ccn_ref[0:1, :].astype(F32) * cxn_ref[0:1, :].astype(F32)
    prev_row = jnp.where(i == 0, 0.0, prev_row)
    next_row = jnp.where(i == blocks_per_seq - 1, 0.0, next_row)
    ridx = lax.broadcasted_iota(jnp.int32, xc.shape, 0)
    x_prev = jnp.where(ridx == 0, prev_row, pltpu.roll(xc, 1, 0))
    x_next = jnp.where(ridx == tm - 1, next_row, pltpu.roll(xc, tm - 1, 0))
    y = x_prev * cw_ref[0:1, :] + xc * cw_ref[1:2, :] + x_next * cw_ref[2:3, :] + cbias_ref[...]
    conv = cb_ref[...].astype(F32) * y * _silu(cz_ref[...].astype(F32))
    att = att_ref[...].astype(F32) * _silu(gz_ref[...].astype(F32))
    wa = att_ref.shape[1]
    hh = h_ref[...] + _dot(att.astype(BF16), w_ref[:wa, :]) + _dot(conv.astype(BF16), w_ref[wa:, :])
    out_ref[...] = _rms(hh) * fg_ref[...]


def _odd_out(att, p, cw, cbias, w, h, fg, *, tm, seq, cols):
    t, d = h.shape
    bps = seq // tm
    sub = tm // SUBLANES
    nsub = t // SUBLANES
    z_c, b_c, c_c, x_c, cz_c = cols

    def blk(cb):
        return pl.BlockSpec((tm, d), lambda i: (i, cb))

    def halo_prev(cb):
        return pl.BlockSpec((SUBLANES, d), lambda i: (jnp.maximum(i * sub - 1, 0), cb))

    def halo_next(cb):
        return pl.BlockSpec((SUBLANES, d), lambda i: (jnp.minimum((i + 1) * sub, nsub - 1), cb))

    kern = functools.partial(_odd_out_kernel, blocks_per_seq=bps)
    return pl.pallas_call(
        kern,
        grid=(t // tm,),
        in_specs=[
            pl.BlockSpec((tm, d), lambda i: (i, 0)),
            blk(z_c), blk(b_c), blk(c_c), blk(x_c), blk(cz_c),
            halo_prev(c_c), halo_prev(x_c), halo_next(c_c), halo_next(x_c),
            pl.BlockSpec(cw.shape, lambda i: (0, 0)),
            pl.BlockSpec((1, d), lambda i: (0, 0)),
            pl.BlockSpec(w.shape, lambda i: (0, 0)),
            pl.BlockSpec((tm, d), lambda i: (i, 0)),
            pl.BlockSpec((1, d), lambda i: (0, 0)),
        ],
        out_specs=pl.BlockSpec((tm, d), lambda i: (i, 0)),
        out_shape=jax.ShapeDtypeStruct((t, d), F32),
        compiler_params=_params("parallel"),
        name="odd_out_proj",
    )(att, p, p, p, p, p, p, p, p, p, cw, cbias, w, h, fg)


def _rope_tables(seq):
    t = jnp.arange(seq)
    pos = jnp.stack([t // GRID_W, t % GRID_W], axis=-1).astype(F32)
    inv = ROPE_THETA ** (-jnp.arange(ROPE_HALF, dtype=F32) / ROPE_HALF)
    ang = pos[:, :, None] * inv
    cos, sin = jnp.cos(ang), jnp.sin(ang)
    cos_l = jnp.stack([cos, cos], axis=1).reshape(seq, GQ_HEAD_DIM)
    sin_l = jnp.stack([-sin, sin], axis=1).reshape(seq, GQ_HEAD_DIM)
    return cos_l, sin_l


def _half_swap(a):
    lead = a.shape[:-1]
    a = a.reshape(*lead, -1, 2, GQ_HEAD_DIM // 2)
    return a[..., ::-1, :].reshape(*lead, -1)


def _rope_dim_order(a):
    lead = a.shape[:-1]
    a = a.reshape(*lead, -1, 2, 2, ROPE_HALF)
    return jnp.swapaxes(a, -3, -2).reshape(*lead, -1)


def kernel(x, norm_g, final_g, ev_w_in, ev_gate_b, ev_w_out, ev_ml_norm_g, ev_na_rpb,
           od_w_in, od_w_out, od_q_norm_g, od_k_norm_g, od_conv_w, od_conv_b):
    batch, seq, d = x.shape
    assert d == D_MODEL and norm_g.shape[0] == 2 and seq % max(ML_CHUNK, 512) == 0
    t = batch * seq
    xf = x.reshape(t, d)
    tm_in = min(1024, seq)
    tm_out = 512

    n_gate = 4 * ML_HEADS
    g0 = 5 * D_MODEL
    w_in = ev_w_in[0]
    w_cat = jnp.concatenate([w_in[:, :g0], w_in[:, g0 + n_gate:], w_in[:, g0:g0 + n_gate],
                             jnp.zeros((d, LANES - n_gate), w_in.dtype)], axis=1).astype(BF16)
    p_ev, gates = _even_in_proj(xf, norm_g[0][None, :], w_cat, tm=tm_in, tn=1024)
    gate_bias = jnp.pad(ev_gate_b[0], (0, LANES - n_gate))[None, :]
    hf, hb = _mlstm(p_ev, gates, gate_bias, batch=batch, seq=seq)
    na = _natten(p_ev, _na_bias_table(ev_na_rpb[0]), batch=batch, seq=seq, col0=g0)
    h1 = _even_out(hf, hb, p_ev, na, ev_ml_norm_g[0][None, :], ev_w_out[0].astype(BF16), xf, tm=tm_out)

    w_in = od_w_in[0]
    qw, kvw = GQ_HEADS * GQ_HEAD_DIM, GQ_KV_HEADS * GQ_HEAD_DIM
    wq, wk = _rope_dim_order(w_in[:, :qw]), _rope_dim_order(w_in[:, qw:qw + kvw])
    w_perm = jnp.concatenate([wq, w_in[:, qw + 2 * kvw:], wk, w_in[:, qw + kvw:qw + 2 * kvw]],
                             axis=1).astype(BF16)
    tn_odd = 512
    w_swap = jnp.concatenate([_half_swap(wq), _half_swap(wk), jnp.zeros((d, tn_odd - kvw), w_in.dtype)],
                             axis=1).astype(BF16)
    cos, sin = _rope_tables(seq)
    qg = jnp.tile(_rope_dim_order(od_q_norm_g[0]), tn_odd // GQ_HEAD_DIM)[None, :]
    kg = jnp.tile(_rope_dim_order(od_k_norm_g[0]), GQ_KV_HEADS)[None, :]
    p_od = _odd_in_proj(h1, norm_g[1][None, :], w_perm, w_swap, qg, _half_swap(qg), kg, _half_swap(kg),
                        cos, sin, tm=tm_in, tn=tn_odd, seq=seq)
    k_col = w_perm.shape[1] - 2 * kvw
    att = _gqa(p_od, batch=batch, seq=seq, k_col=k_col, v_col=k_col + kvw)
    out = _odd_out(att, p_od, od_conv_w[0], od_conv_b[0][None, :], od_w_out[0].astype(BF16), h1,
                   final_g[None, :], tm=tm_out, seq=seq, cols=(1, 2, 3, 4, 5))
    return out.reshape(batch, seq, d)
```

```python
import functools

import jax
import jax.numpy as jnp
from jax import lax
from jax.experimental import pallas as pl
from jax.experimental.pallas import tpu as pltpu

F32 = jnp.float32
BF16 = jnp.bfloat16

D_MODEL = 1024
GRID_W = 64
RMS_EPS = 1e-6

ML_HEADS = 4
ML_HEAD_DIM = 256
ML_CHUNK = 256

NA_HEADS = 16
NA_HEAD_DIM = 64
NA_KH = 8
NA_KW = 16
NA_ROWS_PER_STEP = 8

GQ_HEADS = 8
GQ_HEAD_DIM = 128
GQ_KV_HEADS = 2
GQ_GROUP = GQ_HEADS // GQ_KV_HEADS
ROPE_THETA = 10000.0
ROPE_HALF = GQ_HEAD_DIM // 4
GQ_BLOCK_Q = 256
GQ_BLOCK_K = 512

CV_K = 3

LANES = 128
SUBLANES = 8
MASK_VALUE = -1e30
LOG2_E = 1.4426950408889634

VMEM_LIMIT = 56 * 1024 * 1024


def _params(*sem):
    return pltpu.CompilerParams(dimension_semantics=sem, vmem_limit_bytes=VMEM_LIMIT)


def _sigmoid(x):
    return 1.0 / (1.0 + jnp.exp(-x))


def _silu(x):
    return x * _sigmoid(x)


def _log_sigmoid(x):
    return jnp.minimum(x, 0.0) - jnp.log1p(jnp.exp(-jnp.abs(x)))


def _rms(x, eps=RMS_EPS):
    return x * lax.rsqrt(jnp.mean(x * x, axis=-1, keepdims=True) + eps)


def _dot(a, b):
    return jnp.dot(a, b, preferred_element_type=F32)


def _dot_nt(a, b):
    return lax.dot_general(a, b, (((1,), (1,)), ((), ())), preferred_element_type=F32)


def _dot_tn(a, b):
    return lax.dot_general(a, b, (((0,), (0,)), ((), ())), preferred_element_type=F32)


def _even_in_kernel(x_ref, g_ref, w_ref, wg_ref, o_ref, og_ref, u_ref):
    @pl.when(pl.program_id(1) == 0)
    def _():
        u = (_rms(x_ref[...]) * g_ref[...]).astype(BF16)
        u_ref[...] = u
        og_ref[...] = _dot(u, wg_ref[...])

    o_ref[...] = _dot(u_ref[...], w_ref[...]).astype(o_ref.dtype)


def _even_in_proj(x, g, w, *, tm, tn):
    t, d = x.shape
    n = w.shape[1] - LANES
    gate_block = n // LANES
    return pl.pallas_call(
        _even_in_kernel,
        grid=(t // tm, n // tn),
        in_specs=[
            pl.BlockSpec((tm, d), lambda i, j: (i, 0)),
            pl.BlockSpec((1, d), lambda i, j: (0, 0)),
            pl.BlockSpec((d, tn), lambda i, j: (0, j)),
            pl.BlockSpec((d, LANES), lambda i, j: (0, gate_block)),
        ],
        out_specs=[
            pl.BlockSpec((tm, tn), lambda i, j: (i, j)),
            pl.BlockSpec((tm, LANES), lambda i, j: (i, 0)),
        ],
        out_shape=[
            jax.ShapeDtypeStruct((t, n), BF16),
            jax.ShapeDtypeStruct((t, LANES), F32),
        ],
        scratch_shapes=[pltpu.VMEM((tm, d), BF16)],
        compiler_params=_params("parallel", "arbitrary"),
        name="even_in_proj",
    )(x, g, w, w)


def _rope_norm(acc, acc_sw, gain, gain_sw, cos, sin, scale):
    nh = acc.shape[1] // GQ_HEAD_DIM
    heads = []
    for h in range(nh):
        sl = slice(h * GQ_HEAD_DIM, (h + 1) * GQ_HEAD_DIM)
        a = acc[:, sl]
        r = lax.rsqrt(jnp.mean(a * a, axis=-1, keepdims=True) + RMS_EPS)
        y = a * r * gain[:, sl]
        partner = acc_sw[:, sl] * r * gain_sw[:, sl]
        heads.append((y * cos + partner * sin) * scale)
    return jnp.concatenate(heads, axis=-1)


def _odd_in_kernel(x_ref, g_ref, w_ref, wsw_ref, qg_ref, qgs_ref, kg_ref, kgs_ref, cos_ref, sin_ref,
                   o_ref, u_ref, *, n_q_tiles, kv_tile):
    j = pl.program_id(1)

    @pl.when(j == 0)
    def _():
        u_ref[...] = (_rms(x_ref[...]) * g_ref[...]).astype(BF16)

    is_q = j < n_q_tiles
    is_kv = j == kv_tile

    @pl.when(is_q)
    def _():
        u = u_ref[...]
        o_ref[...] = _rope_norm(_dot(u, w_ref[...]), _dot(u, wsw_ref[...]), qg_ref[...], qgs_ref[...],
                                cos_ref[...], sin_ref[...], GQ_HEAD_DIM ** -0.5 * LOG2_E).astype(o_ref.dtype)

    @pl.when(is_kv)
    def _():
        kw = GQ_KV_HEADS * GQ_HEAD_DIM
        u = u_ref[...]
        acc = _dot(u, w_ref[...])
        k = _rope_norm(acc[:, :kw], _dot(u, wsw_ref[:, :kw]), kg_ref[...], kgs_ref[...],
                       cos_ref[...], sin_ref[...], 1.0)
        o_ref[...] = jnp.concatenate([k, acc[:, kw:]], axis=-1).astype(o_ref.dtype)

    @pl.when(jnp.logical_not(jnp.logical_or(is_q, is_kv)))
    def _():
        o_ref[...] = _dot(u_ref[...], w_ref[...]).astype(o_ref.dtype)


def _odd_in_proj(x, g, w, wsw, qg, qgs, kg, kgs, cos, sin, *, tm, tn, seq):
    t, d = x.shape
    n = w.shape[1]
    n_q_tiles = (GQ_HEADS * GQ_HEAD_DIM) // tn
    kv_tile = n // tn - 1
    kvw = GQ_KV_HEADS * GQ_HEAD_DIM
    sb = seq // tm
    kern = functools.partial(_odd_in_kernel, n_q_tiles=n_q_tiles, kv_tile=kv_tile)

    def wsw_map(i, j):
        return (0, jnp.where(j < n_q_tiles, j, jnp.where(j == kv_tile, n_q_tiles, n_q_tiles - 1)))

    return pl.pallas_call(
        kern,
        grid=(t // tm, n // tn),
        in_specs=[
            pl.BlockSpec((tm, d), lambda i, j: (i, 0)),
            pl.BlockSpec((1, d), lambda i, j: (0, 0)),
            pl.BlockSpec((d, tn), lambda i, j: (0, j)),
            pl.BlockSpec((d, tn), wsw_map),
            pl.BlockSpec((1, tn), lambda i, j: (0, 0)),
            pl.BlockSpec((1, tn), lambda i, j: (0, 0)),
            pl.BlockSpec((1, kvw), lambda i, j: (0, 0)),
            pl.BlockSpec((1, kvw), lambda i, j: (0, 0)),
            pl.BlockSpec((tm, GQ_HEAD_DIM), lambda i, j: (i % sb, 0)),
            pl.BlockSpec((tm, GQ_HEAD_DIM), lambda i, j: (i % sb, 0)),
        ],
        out_specs=pl.BlockSpec((tm, tn), lambda i, j: (i, j)),
        out_shape=jax.ShapeDtypeStruct((t, n), BF16),
        scratch_shapes=[pltpu.VMEM((tm, d), BF16)],
        compiler_params=_params("parallel", "arbitrary"),
        name="odd_in_proj",
    )(x, g, w, wsw, qg, qgs, kg, kgs, cos, sin)


def _mlstm_kernel(qf_ref, kf_ref, vf_ref, gf_ref, qb_ref, kb_ref, vb_ref, gb_ref, bias_ref,
                  hf_ref, hb_ref, c_ref, n_ref, m_ref):
    L = qf_ref.shape[0]
    dh = ML_HEAD_DIM
    k_scale = dh ** -0.5

    @pl.when(pl.program_id(1) == 0)
    def _():
        c_ref[...] = jnp.zeros_like(c_ref)
        n_ref[...] = jnp.zeros_like(n_ref)
        m_ref[...] = jnp.zeros_like(m_ref)

    row = lax.broadcasted_iota(jnp.int32, (L, L), 0)
    col = lax.broadcasted_iota(jnp.int32, (L, L), 1)
    lower = row >= col
    upper = row <= col
    lane = lax.broadcasted_iota(jnp.int32, (L, LANES), 1)
    is_forget = ((lane >= ML_HEADS) & (lane < 2 * ML_HEADS)) | ((lane >= 3 * ML_HEADS) & (lane < 4 * ML_HEADS))

    def gate_tables(g_ref, tri):
        g = g_ref[...] + bias_ref[...]
        lf = jnp.where(is_forget, _log_sigmoid(g), 0.0)
        cum = jnp.dot(tri.astype(F32), lf, preferred_element_type=F32, precision=lax.Precision.HIGHEST)
        tab = jnp.where(is_forget, cum, g)
        return tab, tab.T

    tab_f, tab_ft = gate_tables(gf_ref, lower)
    tab_b, tab_bt = gate_tables(gb_ref, upper)

    def chain(idx, q, k, v, bcol, igcol, brow, igrow, g_tot, valid):
        c = c_ref[idx]
        n = n_ref[idx][0:1, :]
        m_prev = m_ref[idx][0:1, 0:1]
        d = jnp.where(valid, bcol - brow + igrow, MASK_VALUE)
        inter = bcol + m_prev
        m_t = jnp.maximum(inter, jnp.max(d, axis=-1, keepdims=True))
        w = jnp.exp(d - m_t) * (_dot_nt(q, k) * k_scale)
        decay = jnp.exp(inter - m_t)
        qf = q.astype(F32)
        num = decay * _dot(q, c.astype(BF16)) + _dot(w.astype(BF16), v)
        den = decay * jnp.sum(qf * n, axis=-1, keepdims=True) + jnp.sum(w, axis=-1, keepdims=True)
        h = num / jnp.maximum(jnp.abs(den), jnp.exp(-m_t))
        a = g_tot - bcol + igcol
        m_new = jnp.maximum(g_tot + m_prev, jnp.max(a, axis=0, keepdims=True))
        carry = jnp.exp(g_tot + m_prev - m_new)
        wk = jnp.exp(a - m_new)
        wkv = (wk * v.astype(F32)).astype(BF16)
        c_ref[idx] = carry * c + _dot_tn(k, wkv) * k_scale
        n_new = carry * n + jnp.sum(wk * k.astype(F32), axis=0, keepdims=True) * k_scale
        n_ref[idx] = jnp.broadcast_to(n_new, n_ref.shape[1:])
        m_ref[idx] = jnp.broadcast_to(m_new, m_ref.shape[1:])
        return h

    for hd in range(ML_HEADS):
        sl = slice(hd * dh, (hd + 1) * dh)
        fi, ff = hd, ML_HEADS + hd
        h_f = chain(hd, qf_ref[:, sl], kf_ref[:, sl], vf_ref[:, sl],
                    tab_f[:, ff:ff + 1], tab_f[:, fi:fi + 1], tab_ft[ff:ff + 1, :], tab_ft[fi:fi + 1, :],
                    tab_f[L - 1:L, ff:ff + 1], lower)
        hf_ref[:, sl] = h_f.astype(hf_ref.dtype)
        bi, bf = 2 * ML_HEADS + hd, 3 * ML_HEADS + hd
        h_b = chain(ML_HEADS + hd, qb_ref[:, sl], kb_ref[:, sl], vb_ref[:, sl],
                    tab_b[:, bf:bf + 1], tab_b[:, bi:bi + 1], tab_bt[bf:bf + 1, :], tab_bt[bi:bi + 1, :],
                    tab_b[0:1, bf:bf + 1], upper)
        hb_ref[:, sl] = h_b.astype(hb_ref.dtype)


def _mlstm(p, gates, bias, *, batch, seq):
    t = p.shape[0]
    L = ML_CHUNK
    nc = seq // L
    w = ML_HEADS * ML_HEAD_DIM

    def fwd(cb):
        return lambda b, c: (b * nc + c, cb)

    def bwd(cb):
        return lambda b, c: (b * nc + nc - 1 - c, cb)

    return pl.pallas_call(
        _mlstm_kernel,
        grid=(batch, nc),
        in_specs=[
            pl.BlockSpec((L, w), fwd(0)), pl.BlockSpec((L, w), fwd(1)), pl.BlockSpec((L, w), fwd(2)),
            pl.BlockSpec((L, LANES), fwd(0)),
            pl.BlockSpec((L, w), bwd(0)), pl.BlockSpec((L, w), bwd(1)), pl.BlockSpec((L, w), bwd(2)),
            pl.BlockSpec((L, LANES), bwd(0)),
            pl.BlockSpec((1, LANES), lambda b, c: (0, 0)),
        ],
        out_specs=[pl.BlockSpec((L, w), fwd(0)), pl.BlockSpec((L, w), bwd(0))],
        out_shape=[jax.ShapeDtypeStruct((t, w), BF16), jax.ShapeDtypeStruct((t, w), BF16)],
        scratch_shapes=[
            pltpu.VMEM((2 * ML_HEADS, ML_HEAD_DIM, ML_HEAD_DIM), F32),
            pltpu.VMEM((2 * ML_HEADS, SUBLANES, ML_HEAD_DIM), F32),
            pltpu.VMEM((2 * ML_HEADS, SUBLANES, LANES), F32),
        ],
        compiler_params=_params("parallel", "arbitrary"),
        name="mlstm",
    )(p, p, p, gates, p, p, p, gates, bias)


def _na_table_kernel(rp_ref, o_ref):
    c = lax.broadcasted_iota(jnp.int32, (GRID_W, LANES), 0)
    lane = lax.broadcasted_iota(jnp.int32, (GRID_W, LANES), 1)
    c0 = jnp.clip(c - NA_KW // 2, 0, GRID_W - NA_KW)
    low_half = lane < GRID_W
    kc = jnp.where(low_half, lane, lane - GRID_W)
    in_window = (kc >= c0) & (kc < c0 + NA_KW)
    lo, hi = [], []
    for dr in range(2 * NA_KH - 1):
        row = jnp.broadcast_to(rp_ref[0, dr:dr + 1, :], (GRID_W, LANES))
        lo.append(jnp.where(in_window & low_half, pltpu.roll(row, GRID_W + 1, 1, stride=1, stride_axis=0),
                            MASK_VALUE))
        hi.append(jnp.where(in_window & jnp.logical_not(low_half),
                            pltpu.roll(row, 1, 1, stride=1, stride_axis=0), MASK_VALUE))
    for rel in range(NA_KH):
        pairs = [jnp.maximum(lo[2 * j - rel + NA_KH - 1], hi[2 * j - rel + NA_KH]) for j in range(NA_KH // 2)]
        o_ref[0, rel] = jnp.concatenate(pairs, axis=1)


def _na_bias_table(rpb):
    nh, ndr, ndc = rpb.shape
    lead = GRID_W - NA_KW
    rp = jnp.pad(rpb.astype(F32) * LOG2_E, ((0, 0), (0, 2 * NA_KH - ndr), (lead, LANES - lead - ndc)))
    return pl.pallas_call(
        _na_table_kernel,
        grid=(nh,),
        in_specs=[pl.BlockSpec((1, 2 * NA_KH, LANES), lambda h: (h, 0, 0))],
        out_specs=pl.BlockSpec((1, NA_KH, GRID_W, NA_KH * GRID_W), lambda h: (h, 0, 0, 0)),
        out_shape=jax.ShapeDtypeStruct((nh, NA_KH, GRID_W, NA_KH * GRID_W), F32),
        compiler_params=_params("parallel"),
        name="na_bias_table",
    )(rp)


def _na_kernel(q_ref, k_ref, v_ref, z_ref, tbl_ref, o_ref, *, rows):
    rb = pl.program_id(2)
    win = NA_KH * GRID_W
    lane_q = lax.broadcasted_iota(jnp.int32, (GRID_W, LANES), 1)
    head0_q = lane_q < NA_HEAD_DIM
    starts, scores = [], []
    for i in range(NA_ROWS_PER_STEP):
        r = rb * NA_ROWS_PER_STEP + i
        r0 = jnp.clip(r - NA_KH // 2, 0, rows - NA_KH)
        rel = r - r0
        start = pl.multiple_of(r0 * GRID_W, GRID_W)
        kw = k_ref[pl.ds(start, win), :]
        q = q_ref[i * GRID_W:(i + 1) * GRID_W, :]
        zero = jnp.zeros_like(q)
        qs = jnp.concatenate([jnp.where(head0_q, q, zero), jnp.where(head0_q, zero, q)], axis=0)
        bias = jnp.concatenate([tbl_ref[0, rel], tbl_ref[1, rel]], axis=0)
        scores.append(_dot_nt(qs, kw) * (NA_HEAD_DIM ** -0.5 * LOG2_E) + bias)
        starts.append(start)
    for i in range(NA_ROWS_PER_STEP):
        s = scores[i]
        e = jnp.exp2(s - jnp.max(s, axis=-1, keepdims=True))
        inv = 1.0 / jnp.sum(e, axis=-1, keepdims=True)
        o = _dot(e.astype(BF16), v_ref[pl.ds(starts[i], win), :]) * inv
        out = jnp.where(head0_q, o[:GRID_W], o[GRID_W:])
        zz = z_ref[i * GRID_W:(i + 1) * GRID_W, :].astype(F32)
        o_ref[i * GRID_W:(i + 1) * GRID_W, :] = (out * _silu(zz)).astype(o_ref.dtype)


def _natten(p, tbl, *, batch, seq, col0):
    t = p.shape[0]
    rows = seq // GRID_W
    width = NA_HEADS * NA_HEAD_DIM
    pairs = width // LANES
    tq = NA_ROWS_PER_STEP * GRID_W
    nrb = seq // tq
    cb = col0 // LANES
    kern = functools.partial(_na_kernel, rows=rows)
    return pl.pallas_call(
        kern,
        grid=(batch, pairs, nrb),
        in_specs=[
            pl.BlockSpec((tq, LANES), lambda b, h, r: (b * nrb + r, cb + h)),
            pl.BlockSpec((seq, LANES), lambda b, h, r: (b, cb + pairs + h)),
            pl.BlockSpec((seq, LANES), lambda b, h, r: (b, cb + 2 * pairs + h)),
            pl.BlockSpec((tq, LANES), lambda b, h, r: (b * nrb + r, cb + 3 * pairs + h)),
            pl.BlockSpec((2, NA_KH, GRID_W, NA_KH * GRID_W), lambda b, h, r: (h, 0, 0, 0)),
        ],
        out_specs=pl.BlockSpec((tq, LANES), lambda b, h, r: (b * nrb + r, h)),
        out_shape=jax.ShapeDtypeStruct((t, width), BF16),
        compiler_params=_params("parallel", "parallel", "arbitrary"),
        name="natten",
    )(p, p, p, p, tbl)


def _even_out_kernel(hf_ref, hb_ref, o_ref, z_ref, na_ref, g_ref, w_ref, x_ref, out_ref):
    h = hf_ref[...].astype(F32) + hb_ref[...].astype(F32)
    hn = jnp.concatenate(
        [_rms(h[:, i * ML_HEAD_DIM:(i + 1) * ML_HEAD_DIM]) for i in range(ML_HEADS)], axis=-1)
    a = hn * g_ref[...] * _sigmoid(o_ref[...].astype(F32)) * _silu(z_ref[...].astype(F32))
    wa = ML_HEADS * ML_HEAD_DIM
    y = _dot(a.astype(BF16), w_ref[:wa, :]) + _dot(na_ref[...], w_ref[wa:, :])
    out_ref[...] = x_ref[...] + y


def _even_out(hf, hb, p, na, g, w, x, *, tm):
    t, d = x.shape
    wa = hf.shape[1]
    return pl.pallas_call(
        _even_out_kernel,
        grid=(t // tm,),
        in_specs=[
            pl.BlockSpec((tm, wa), lambda i: (i, 0)),
            pl.BlockSpec((tm, wa), lambda i: (i, 0)),
            pl.BlockSpec((tm, wa), lambda i: (i, 3)),
            pl.BlockSpec((tm, wa), lambda i: (i, 4)),
            pl.BlockSpec((tm, na.shape[1]), lambda i: (i, 0)),
            pl.BlockSpec((1, wa), lambda i: (0, 0)),
            pl.BlockSpec(w.shape, lambda i: (0, 0)),
            pl.BlockSpec((tm, d), lambda i: (i, 0)),
        ],
        out_specs=pl.BlockSpec((tm, d), lambda i: (i, 0)),
        out_shape=jax.ShapeDtypeStruct((t, d), F32),
        compiler_params=_params("parallel"),
        name="even_out_proj",
    )(hf, hb, p, p, na, g, w, x)


def _gqa_kernel(q_ref, k_ref, v_ref, o_ref, vaug_ref, m_ref, acc_ref, *, tk):
    tq = q_ref.shape[0]
    seq = k_ref.shape[0]
    dh = GQ_HEAD_DIM

    @pl.when(pl.program_id(2) == 0)
    def _():
        vaug_ref[:, :dh] = v_ref[...]
        vaug_ref[:, dh:] = jnp.ones((seq, dh), BF16)

    q = q_ref[...]
    qs = jnp.concatenate([q[:, g * dh:(g + 1) * dh] for g in range(GQ_GROUP)], axis=0)
    m_ref[...] = jnp.full_like(m_ref, -jnp.inf)
    acc_ref[...] = jnp.zeros_like(acc_ref)

    def scores(c):
        return _dot_nt(qs, k_ref[c * tk:(c + 1) * tk, :])

    def update(s, c):
        m_prev = m_ref[...]
        m_new = jnp.maximum(m_prev, jnp.max(s, axis=-1, keepdims=True))
        alpha = jnp.exp2(m_prev - m_new)
        p = jnp.exp2(s - jnp.concatenate([m_new] * (tk // LANES), axis=1))
        pv = _dot(p.astype(BF16), vaug_ref[c * tk:(c + 1) * tk, :])
        acc_ref[...] = jnp.concatenate([alpha, alpha], axis=1) * acc_ref[...] + pv
        m_ref[...] = m_new

    nk = seq // tk
    s_next = scores(0)
    for c in range(nk):
        s_cur = s_next
        if c + 1 < nk:
            s_next = scores(c + 1)
        update(s_cur, c)
    acc = acc_ref[...]
    out = acc[:, :dh] / acc[:, dh:]
    o_ref[...] = jnp.concatenate(
        [out[g * tq:(g + 1) * tq] for g in range(GQ_GROUP)], axis=-1).astype(o_ref.dtype)


def _gqa(p, *, batch, seq, k_col, v_col):
    t = p.shape[0]
    tq, tk = GQ_BLOCK_Q, GQ_BLOCK_K
    nq = seq // tq
    gw = GQ_GROUP * GQ_HEAD_DIM
    kb, vb = k_col // GQ_HEAD_DIM, v_col // GQ_HEAD_DIM
    kern = functools.partial(_gqa_kernel, tk=tk)
    return pl.pallas_call(
        kern,
        grid=(batch, GQ_KV_HEADS, nq),
        in_specs=[
            pl.BlockSpec((tq, gw), lambda b, h, i: (b * nq + i, h)),
            pl.BlockSpec((seq, GQ_HEAD_DIM), lambda b, h, i: (b, kb + h)),
            pl.BlockSpec((seq, GQ_HEAD_DIM), lambda b, h, i: (b, vb + h)),
        ],
        out_specs=pl.BlockSpec((tq, gw), lambda b, h, i: (b * nq + i, h)),
        out_shape=jax.ShapeDtypeStruct((t, GQ_HEADS * GQ_HEAD_DIM), BF16),
        scratch_shapes=[
            pltpu.VMEM((seq, 2 * GQ_HEAD_DIM), BF16),
            pltpu.VMEM((GQ_GROUP * tq, LANES), F32),
            pltpu.VMEM((GQ_GROUP * tq, 2 * GQ_HEAD_DIM), F32),
        ],
        compiler_params=_params("parallel", "parallel", "arbitrary"),
        name="gqa_attention",
    )(p, p, p)


def _odd_out_kernel(att_ref, gz_ref, cb_ref, cc_ref, cx_ref, cz_ref, ccp_ref, cxp_ref, ccn_ref, cxn_ref,
                    cw_ref, cbias_ref, w_ref, h_ref, fg_ref, out_ref, *, blocks_per_seq):
    tm = att_ref.shape[0]
    i = pl.program_id(0) % blocks_per_seq
    xc = cc_ref[...].astype(F32) * cx_ref[...].astype(F32)
    prev_row = ccp_ref[SUBLANES - 1:SUBLANES, :].astype(F32) * cxp_ref[SUBLANES - 1:SUBLANES, :].astype(F32)
    next_row = ccn_ref[0:1, :].astype(F32) * cxn_ref[0:1, :].astype(F32)
    prev_row = jnp.where(i == 0, 0.0, prev_row)
    next_row = jnp.where(i == blocks_per_seq - 1, 0.0, next_row)
    ridx = lax.broadcasted_iota(jnp.int32, xc.shape, 0)
    x_prev = jnp.where(ridx == 0, prev_row, pltpu.roll(xc, 1, 0))
    x_next = jnp.where(ridx == tm - 1, next_row, pltpu.roll(xc, tm - 1, 0))
    y = x_prev * cw_ref[0:1, :] + xc * cw_ref[1:2, :] + x_next * cw_ref[2:3, :] + cbias_ref[...]
    conv = cb_ref[...].astype(F32) * y * _silu(cz_ref[...].astype(F32))
    att = att_ref[...].astype(F32) * _silu(gz_ref[...].astype(F32))
    wa = att_ref.shape[1]
    hh = h_ref[...] + _dot(att.astype(BF16), w_ref[:wa, :]) + _dot(conv.astype(BF16), w_ref[wa:, :])
    out_ref[...] = _rms(hh) * fg_ref[...]


def _odd_out(att, p, cw, cbias, w, h, fg, *, tm, seq, cols):
    t, d = h.shape
    bps = seq // tm
    sub = tm // SUBLANES
    nsub = t // SUBLANES
    z_c, b_c, c_c, x_c, cz_c = cols

    def blk(cb):
        return pl.BlockSpec((tm, d), lambda i: (i, cb))

    def halo_prev(cb):
        return pl.BlockSpec((SUBLANES, d), lambda i: (jnp.maximum(i * sub - 1, 0), cb))

    def halo_next(cb):
        return pl.BlockSpec((SUBLANES, d), lambda i: (jnp.minimum((i + 1) * sub, nsub - 1), cb))

    kern = functools.partial(_odd_out_kernel, blocks_per_seq=bps)
    return pl.pallas_call(
        kern,
        grid=(t // tm,),
        in_specs=[
            pl.BlockSpec((tm, d), lambda i: (i, 0)),
            blk(z_c), blk(b_c), blk(c_c), blk(x_c), blk(cz_c),
            halo_prev(c_c), halo_prev(x_c), halo_next(c_c), halo_next(x_c),
            pl.BlockSpec(cw.shape, lambda i: (0, 0)),
            pl.BlockSpec((1, d), lambda i: (0, 0)),
            pl.BlockSpec(w.shape, lambda i: (0, 0)),
            pl.BlockSpec((tm, d), lambda i: (i, 0)),
            pl.BlockSpec((1, d), lambda i: (0, 0)),
        ],
        out_specs=pl.BlockSpec((tm, d), lambda i: (i, 0)),
        out_shape=jax.ShapeDtypeStruct((t, d), F32),
        compiler_params=_params("parallel"),
        name="odd_out_proj",
    )(att, p, p, p, p, p, p, p, p, p, cw, cbias, w, h, fg)


def _rope_tables(seq):
    t = jnp.arange(seq)
    pos = jnp.stack([t // GRID_W, t % GRID_W], axis=-1).astype(F32)
    inv = ROPE_THETA ** (-jnp.arange(ROPE_HALF, dtype=F32) / ROPE_HALF)
    ang = pos[:, :, None] * inv
    cos, sin = jnp.cos(ang), jnp.sin(ang)
    cos_l = jnp.stack([cos, cos], axis=1).reshape(seq, GQ_HEAD_DIM)
    sin_l = jnp.stack([-sin, sin], axis=1).reshape(seq, GQ_HEAD_DIM)
    return cos_l, sin_l


def _half_swap(a):
    lead = a.shape[:-1]
    a = a.reshape(*lead, -1, 2, GQ_HEAD_DIM // 2)
    return a[..., ::-1, :].reshape(*lead, -1)


def _rope_dim_order(a):
    lead = a.shape[:-1]
    a = a.reshape(*lead, -1, 2, 2, ROPE_HALF)
    return jnp.swapaxes(a, -3, -2).reshape(*lead, -1)


def kernel(x, norm_g, final_g, ev_w_in, ev_gate_b, ev_w_out, ev_ml_norm_g, ev_na_rpb,
           od_w_in, od_w_out, od_q_norm_g, od_k_norm_g, od_conv_w, od_conv_b):
    batch, seq, d = x.shape
    assert d == D_MODEL and norm_g.shape[0] == 2 and seq % max(ML_CHUNK, 512) == 0
    t = batch * seq
    xf = x.reshape(t, d)
    tm_in = min(1024, seq)
    tm_out = 512

    n_gate = 4 * ML_HEADS
    g0 = 5 * D_MODEL
    w_in = ev_w_in[0].astype(BF16)
    w_cat = jnp.concatenate([w_in[:, :g0], w_in[:, g0 + n_gate:], w_in[:, g0:g0 + n_gate],
                             jnp.zeros((d, LANES - n_gate), BF16)], axis=1)
    p_ev, gates = _even_in_proj(xf, norm_g[0][None, :], w_cat, tm=tm_in, tn=1024)
    gate_bias = jnp.pad(ev_gate_b[0], (0, LANES - n_gate))[None, :]
    hf, hb = _mlstm(p_ev, gates, gate_bias, batch=batch, seq=seq)
    na = _natten(p_ev, _na_bias_table(ev_na_rpb[0]), batch=batch, seq=seq, col0=g0)
    h1 = _even_out(hf, hb, p_ev, na, ev_ml_norm_g[0][None, :], ev_w_out[0].astype(BF16), xf, tm=tm_out)

    w_in = od_w_in[0].astype(BF16)
    qw, kvw = GQ_HEADS * GQ_HEAD_DIM, GQ_KV_HEADS * GQ_HEAD_DIM
    wq, wk = _rope_dim_order(w_in[:, :qw]), _rope_dim_order(w_in[:, qw:qw + kvw])
    w_perm = jnp.concatenate([wq, w_in[:, qw + 2 * kvw:], wk, w_in[:, qw + kvw:qw + 2 * kvw]], axis=1)
    tn_odd = 512
    w_swap = jnp.concatenate([_half_swap(wq), _half_swap(wk), jnp.zeros((d, tn_odd - kvw), BF16)], axis=1)
    cos, sin = _rope_tables(seq)
    qg = jnp.tile(_rope_dim_order(od_q_norm_g[0]), tn_odd // GQ_HEAD_DIM)[None, :]
    kg = jnp.tile(_rope_dim_order(od_k_norm_g[0]), GQ_KV_HEADS)[None, :]
    p_od = _odd_in_proj(h1, norm_g[1][None, :], w_perm, w_swap, qg, _half_swap(qg), kg, _half_swap(kg),
                        cos, sin, tm=tm_in, tn=tn_odd, seq=seq)
    k_col = w_perm.shape[1] - 2 * kvw
    att = _gqa(p_od, batch=batch, seq=seq, k_col=k_col, v_col=k_col + kvw)
    out = _odd_out(att, p_od, od_conv_w[0], od_conv_b[0][None, :], od_w_out[0].astype(BF16), h1,
                   final_g[None, :], tm=tm_out, seq=seq, cols=(1, 2, 3, 4, 5))
    return out.reshape(batch, seq, d)
```

```python
import functools

import jax
import jax.numpy as jnp
from jax import lax
from jax.experimental import pallas as pl
from jax.experimental.pallas import tpu as pltpu

F32 = jnp.float32
BF16 = jnp.bfloat16

D_MODEL = 1024
GRID_W = 64
RMS_EPS = 1e-6

ML_HEADS = 4
ML_HEAD_DIM = 256
ML_CHUNK = 256

NA_HEADS = 16
NA_HEAD_DIM = 64
NA_KH = 8
NA_KW = 16
NA_ROWS_PER_STEP = 16

GQ_HEADS = 8
GQ_HEAD_DIM = 128
GQ_KV_HEADS = 2
GQ_GROUP = GQ_HEADS // GQ_KV_HEADS
ROPE_THETA = 10000.0
ROPE_HALF = GQ_HEAD_DIM // 4
GQ_BLOCK_Q = 256
GQ_BLOCK_K = 512

CV_K = 3

LANES = 128
SUBLANES = 8
MASK_VALUE = -1e30
LOG2_E = 1.4426950408889634

VMEM_LIMIT = 56 * 1024 * 1024


def _params(*sem):
    return pltpu.CompilerParams(dimension_semantics=sem, vmem_limit_bytes=VMEM_LIMIT)


def _sigmoid(x):
    return 1.0 / (1.0 + jnp.exp(-x))


def _silu(x):
    return x * _sigmoid(x)


def _log_sigmoid(x):
    return jnp.minimum(x, 0.0) - jnp.log1p(jnp.exp(-jnp.abs(x)))


def _rms(x, eps=RMS_EPS):
    return x * lax.rsqrt(jnp.mean(x * x, axis=-1, keepdims=True) + eps)


def _dot(a, b):
    return jnp.dot(a, b, preferred_element_type=F32)


def _dot_nt(a, b):
    return lax.dot_general(a, b, (((1,), (1,)), ((), ())), preferred_element_type=F32)


def _dot_tn(a, b):
    return lax.dot_general(a, b, (((0,), (0,)), ((), ())), preferred_element_type=F32)


def _even_in_kernel(x_ref, g_ref, wa_ref, wb_ref, wg_ref, o_ref, og_ref, u_ref, *, n_a_tiles):
    j = pl.program_id(1)

    @pl.when(j == 0)
    def _():
        u = (_rms(x_ref[...]) * g_ref[...]).astype(BF16)
        u_ref[...] = u
        og_ref[...] = _dot(u, wg_ref[...])

    @pl.when(j < n_a_tiles)
    def _():
        o_ref[...] = _dot(u_ref[...], wa_ref[...]).astype(o_ref.dtype)

    @pl.when(j >= n_a_tiles)
    def _():
        o_ref[...] = _dot(u_ref[...], wb_ref[...]).astype(o_ref.dtype)


def _even_in_proj(x, g, wa, wb, wg, *, tm, tn):
    t, d = x.shape
    na, nb = wa.shape[1] // tn, wb.shape[1] // tn
    n = wa.shape[1] + wb.shape[1]
    kern = functools.partial(_even_in_kernel, n_a_tiles=na)
    return pl.pallas_call(
        kern,
        grid=(t // tm, na + nb),
        in_specs=[
            pl.BlockSpec((tm, d), lambda i, j: (i, 0)),
            pl.BlockSpec((1, d), lambda i, j: (0, 0)),
            pl.BlockSpec((d, tn), lambda i, j: (0, jnp.minimum(j, na - 1))),
            pl.BlockSpec((d, tn), lambda i, j: (0, jnp.maximum(j - na, 0))),
            pl.BlockSpec((d, LANES), lambda i, j: (0, 0)),
        ],
        out_specs=[
            pl.BlockSpec((tm, tn), lambda i, j: (i, j)),
            pl.BlockSpec((tm, LANES), lambda i, j: (i, 0)),
        ],
        out_shape=[
            jax.ShapeDtypeStruct((t, n), BF16),
            jax.ShapeDtypeStruct((t, LANES), F32),
        ],
        scratch_shapes=[pltpu.VMEM((tm, d), BF16)],
        compiler_params=_params("parallel", "arbitrary"),
        name="even_in_proj",
    )(x, g, wa, wb, wg)


def _rope_norm(acc, acc_sw, gain, gain_sw, cos, sin, scale):
    nh = acc.shape[1] // GQ_HEAD_DIM
    heads = []
    for h in range(nh):
        sl = slice(h * GQ_HEAD_DIM, (h + 1) * GQ_HEAD_DIM)
        a = acc[:, sl]
        r = lax.rsqrt(jnp.mean(a * a, axis=-1, keepdims=True) + RMS_EPS)
        y = a * r * gain[:, sl]
        partner = acc_sw[:, sl] * r * gain_sw[:, sl]
        heads.append((y * cos + partner * sin) * scale)
    return jnp.concatenate(heads, axis=-1)


def _odd_in_kernel(x_ref, g_ref, wa_ref, wsw_ref, wb_ref, qg_ref, qgs_ref, kg_ref, kgs_ref, cos_ref, sin_ref,
                   o_ref, u_ref, *, n_q_tiles, kv_tile):
    j = pl.program_id(1)

    @pl.when(j == 0)
    def _():
        u_ref[...] = (_rms(x_ref[...]) * g_ref[...]).astype(BF16)

    is_q = j < n_q_tiles
    is_kv = j == kv_tile

    @pl.when(is_q)
    def _():
        u = u_ref[...]
        o_ref[...] = _rope_norm(_dot(u, wa_ref[...]), _dot(u, wsw_ref[...]), qg_ref[...], qgs_ref[...],
                                cos_ref[...], sin_ref[...], GQ_HEAD_DIM ** -0.5 * LOG2_E).astype(o_ref.dtype)

    @pl.when(is_kv)
    def _():
        kw = GQ_KV_HEADS * GQ_HEAD_DIM
        u = u_ref[...]
        acc = _dot(u, wa_ref[...])
        k = _rope_norm(acc[:, :kw], _dot(u, wsw_ref[:, :kw]), kg_ref[...], kgs_ref[...],
                       cos_ref[...], sin_ref[...], 1.0)
        o_ref[...] = jnp.concatenate([k, acc[:, kw:]], axis=-1).astype(o_ref.dtype)

    @pl.when(jnp.logical_not(jnp.logical_or(is_q, is_kv)))
    def _():
        o_ref[...] = _dot(u_ref[...], wb_ref[...]).astype(o_ref.dtype)


def _odd_in_proj(x, g, wa, wsw, wb, qg, qgs, kg, kgs, cos, sin, *, tm, tn, seq):
    t, d = x.shape
    n_q_tiles = (GQ_HEADS * GQ_HEAD_DIM) // tn
    n_plain = wb.shape[1] // tn
    kv_tile = n_q_tiles + n_plain
    n = (kv_tile + 1) * tn
    kvw = GQ_KV_HEADS * GQ_HEAD_DIM
    sb = seq // tm
    kern = functools.partial(_odd_in_kernel, n_q_tiles=n_q_tiles, kv_tile=kv_tile)

    def qkv_map(i, j):
        return (0, jnp.where(j < n_q_tiles, j, jnp.where(j == kv_tile, n_q_tiles, n_q_tiles - 1)))

    return pl.pallas_call(
        kern,
        grid=(t // tm, kv_tile + 1),
        in_specs=[
            pl.BlockSpec((tm, d), lambda i, j: (i, 0)),
            pl.BlockSpec((1, d), lambda i, j: (0, 0)),
            pl.BlockSpec((d, tn), qkv_map),
            pl.BlockSpec((d, tn), qkv_map),
            pl.BlockSpec((d, tn), lambda i, j: (0, jnp.clip(j - n_q_tiles, 0, n_plain - 1))),
            pl.BlockSpec((1, tn), lambda i, j: (0, 0)),
            pl.BlockSpec((1, tn), lambda i, j: (0, 0)),
            pl.BlockSpec((1, kvw), lambda i, j: (0, 0)),
            pl.BlockSpec((1, kvw), lambda i, j: (0, 0)),
            pl.BlockSpec((tm, GQ_HEAD_DIM), lambda i, j: (i % sb, 0)),
            pl.BlockSpec((tm, GQ_HEAD_DIM), lambda i, j: (i % sb, 0)),
        ],
        out_specs=pl.BlockSpec((tm, tn), lambda i, j: (i, j)),
        out_shape=jax.ShapeDtypeStruct((t, n), BF16),
        scratch_shapes=[pltpu.VMEM((tm, d), BF16)],
        compiler_params=_params("parallel", "arbitrary"),
        name="odd_in_proj",
    )(x, g, wa, wsw, wb, qg, qgs, kg, kgs, cos, sin)


def _split3(a):
    hi = a.astype(BF16)
    r = a - hi.astype(F32)
    mid = r.astype(BF16)
    return hi, mid, (r - mid.astype(F32)).astype(BF16)


def _dot_exact_lhs(a, b):
    return sum(_dot(p, b) for p in _split3(a))


def _dot_exact_rhs(a, b):
    return sum(_dot(a, p) for p in _split3(b))


def _mlstm_kernel(qf_ref, kf_ref, vf_ref, gf_ref, qb_ref, kb_ref, vb_ref, gb_ref, bias_ref, sel_ref,
                  hf_ref, hb_ref, c_ref, n_ref, m_ref):
    L = qf_ref.shape[0]
    dh = ML_HEAD_DIM
    k_scale = dh ** -0.5

    @pl.when(pl.program_id(1) == 0)
    def _():
        c_ref[...] = jnp.zeros_like(c_ref)
        n_ref[...] = jnp.zeros_like(n_ref)
        m_ref[...] = jnp.zeros_like(m_ref)

    row = lax.broadcasted_iota(jnp.int32, (L, L), 0)
    col = lax.broadcasted_iota(jnp.int32, (L, L), 1)
    lower = row >= col
    upper = row <= col
    lane = lax.broadcasted_iota(jnp.int32, (L, LANES), 1)
    is_forget = ((lane >= ML_HEADS) & (lane < 2 * ML_HEADS)) | ((lane >= 3 * ML_HEADS) & (lane < 4 * ML_HEADS))

    def wide(x):
        return jnp.concatenate([x] * (dh // LANES), axis=1)

    def gate_tables(g_ref, tri, first_col):
        g = g_ref[...] + bias_ref[...]
        lf = jnp.where(is_forget, _log_sigmoid(g), 0.0)
        cum = _dot_exact_rhs(tri.astype(BF16), lf)
        tab = jnp.where(is_forget, cum, g) * LOG2_E
        ncol = 2 * ML_HEADS * LANES
        rep = _dot_exact_lhs(tab, sel_ref[:, first_col * LANES:first_col * LANES + ncol])
        return rep, tab.T

    rep_f, tab_ft = gate_tables(gf_ref, lower, 0)
    rep_b, tab_bt = gate_tables(gb_ref, upper, 2 * ML_HEADS)

    def chain(idx, q, k, v, b128, ig128, brow, igrow, g128, valid):
        c = c_ref[idx]
        n = n_ref[idx][0:1, :]
        m_prev = m_ref[idx][0:1, :]
        d = jnp.where(valid, wide(b128) + (igrow - brow), MASK_VALUE)
        inter = b128 + m_prev
        m_t = jnp.maximum(inter, jnp.max(d, axis=-1, keepdims=True))
        w = jnp.exp2(d - wide(m_t)) * _dot_nt(q * jnp.asarray(k_scale, q.dtype), k)
        decay = jnp.exp2(inter - m_t)
        num = wide(decay) * _dot(q, c.astype(BF16)) + _dot(w.astype(BF16), v)
        qn = jnp.sum(q.astype(F32) * n, axis=-1, keepdims=True)
        den = decay * qn + jnp.sum(w, axis=-1, keepdims=True)
        h = num * wide(1.0 / jnp.maximum(jnp.abs(den), jnp.exp2(-m_t)))
        a = g128 - b128 + ig128
        m_new = jnp.maximum(g128 + m_prev, jnp.max(a, axis=0, keepdims=True))
        carry = wide(jnp.exp2(g128 + m_prev - m_new))
        wk = wide(jnp.exp2(a - m_new) * k_scale)
        wkv = (wk * v.astype(F32)).astype(BF16)
        c_ref[idx] = carry * c + _dot_tn(k, wkv)
        n_new = carry * n + jnp.sum(wk * k.astype(F32), axis=0, keepdims=True)
        n_ref[idx] = jnp.broadcast_to(n_new, n_ref.shape[1:])
        m_ref[idx] = jnp.broadcast_to(m_new, m_ref.shape[1:])
        return h

    def lanes(rep, j):
        return rep[:, j * LANES:(j + 1) * LANES]

    for hd in range(ML_HEADS):
        sl = slice(hd * dh, (hd + 1) * dh)
        fi, ff = hd, ML_HEADS + hd
        b128 = lanes(rep_f, ff)
        h_f = chain(hd, qf_ref[:, sl], kf_ref[:, sl], vf_ref[:, sl], b128, lanes(rep_f, fi),
                    tab_ft[ff:ff + 1, :], tab_ft[fi:fi + 1, :], b128[L - 1:L, :], lower)
        hf_ref[:, sl] = h_f.astype(hf_ref.dtype)
        b128 = lanes(rep_b, ff)
        go = 2 * ML_HEADS
        h_b = chain(ML_HEADS + hd, qb_ref[:, sl], kb_ref[:, sl], vb_ref[:, sl], b128, lanes(rep_b, fi),
                    tab_bt[go + ff:go + ff + 1, :], tab_bt[go + fi:go + fi + 1, :], b128[0:1, :], upper)
        hb_ref[:, sl] = h_b.astype(hb_ref.dtype)


def _mlstm(p, gates, bias, *, batch, seq):
    t = p.shape[0]
    L = ML_CHUNK
    nc = seq // L
    w = ML_HEADS * ML_HEAD_DIM
    n_gate = 4 * ML_HEADS
    sel = (jnp.arange(LANES)[:, None] == (jnp.arange(n_gate * LANES) // LANES)[None, :]).astype(BF16)

    def fwd(cb):
        return lambda b, c: (b * nc + c, cb)

    def bwd(cb):
        return lambda b, c: (b * nc + nc - 1 - c, cb)

    return pl.pallas_call(
        _mlstm_kernel,
        grid=(batch, nc),
        in_specs=[
            pl.BlockSpec((L, w), fwd(0)), pl.BlockSpec((L, w), fwd(1)), pl.BlockSpec((L, w), fwd(2)),
            pl.BlockSpec((L, LANES), fwd(0)),
            pl.BlockSpec((L, w), bwd(0)), pl.BlockSpec((L, w), bwd(1)), pl.BlockSpec((L, w), bwd(2)),
            pl.BlockSpec((L, LANES), bwd(0)),
            pl.BlockSpec((1, LANES), lambda b, c: (0, 0)),
            pl.BlockSpec(sel.shape, lambda b, c: (0, 0)),
        ],
        out_specs=[pl.BlockSpec((L, w), fwd(0)), pl.BlockSpec((L, w), bwd(0))],
        out_shape=[jax.ShapeDtypeStruct((t, w), BF16), jax.ShapeDtypeStruct((t, w), BF16)],
        scratch_shapes=[
            pltpu.VMEM((2 * ML_HEADS, ML_HEAD_DIM, ML_HEAD_DIM), F32),
            pltpu.VMEM((2 * ML_HEADS, SUBLANES, ML_HEAD_DIM), F32),
            pltpu.VMEM((2 * ML_HEADS, SUBLANES, LANES), F32),
        ],
        compiler_params=_params("parallel", "arbitrary"),
        name="mlstm",
    )(p, p, p, gates, p, p, p, gates, bias, sel)


def _na_table_kernel(rp_ref, o_ref):
    c = lax.broadcasted_iota(jnp.int32, (GRID_W, LANES), 0)
    lane = lax.broadcasted_iota(jnp.int32, (GRID_W, LANES), 1)
    c0 = jnp.clip(c - NA_KW // 2, 0, GRID_W - NA_KW)
    low_half = lane < GRID_W
    kc = jnp.where(low_half, lane, lane - GRID_W)
    in_window = (kc >= c0) & (kc < c0 + NA_KW)
    lo, hi = [], []
    for dr in range(2 * NA_KH - 1):
        row = jnp.broadcast_to(rp_ref[0, dr:dr + 1, :], (GRID_W, LANES))
        lo.append(jnp.where(in_window & low_half, pltpu.roll(row, GRID_W + 1, 1, stride=1, stride_axis=0),
                            MASK_VALUE))
        hi.append(jnp.where(in_window & jnp.logical_not(low_half),
                            pltpu.roll(row, 1, 1, stride=1, stride_axis=0), MASK_VALUE))
    for rel in range(NA_KH):
        pairs = [jnp.maximum(lo[2 * j - rel + NA_KH - 1], hi[2 * j - rel + NA_KH]) for j in range(NA_KH // 2)]
        o_ref[0, rel] = jnp.concatenate(pairs, axis=1)


def _na_bias_table(rpb):
    nh, ndr, ndc = rpb.shape
    lead = GRID_W - NA_KW
    rp = jnp.pad(rpb.astype(F32) * LOG2_E, ((0, 0), (0, 2 * NA_KH - ndr), (lead, LANES - lead - ndc)))
    return pl.pallas_call(
        _na_table_kernel,
        grid=(nh,),
        in_specs=[pl.BlockSpec((1, 2 * NA_KH, LANES), lambda h: (h, 0, 0))],
        out_specs=pl.BlockSpec((1, NA_KH, GRID_W, NA_KH * GRID_W), lambda h: (h, 0, 0, 0)),
        out_shape=jax.ShapeDtypeStruct((nh, NA_KH, GRID_W, NA_KH * GRID_W), F32),
        compiler_params=_params("parallel"),
        name="na_bias_table",
    )(rp)


def _na_kernel(q_ref, k_ref, v_ref, z_ref, tbl_ref, o_ref, *, rows):
    rb = pl.program_id(2)
    win = NA_KH * GRID_W
    lane_q = lax.broadcasted_iota(jnp.int32, (GRID_W, LANES), 1)
    head0_q = lane_q < NA_HEAD_DIM
    starts, scores = [], []
    for i in range(NA_ROWS_PER_STEP):
        r = rb * NA_ROWS_PER_STEP + i
        r0 = jnp.clip(r - NA_KH // 2, 0, rows - NA_KH)
        rel = r - r0
        start = pl.multiple_of(r0 * GRID_W, GRID_W)
        kw = k_ref[pl.ds(start, win), :]
        q = q_ref[i * GRID_W:(i + 1) * GRID_W, :]
        zero = jnp.zeros_like(q)
        qs = jnp.concatenate([jnp.where(head0_q, q, zero), jnp.where(head0_q, zero, q)], axis=0)
        bias = jnp.concatenate([tbl_ref[0, rel], tbl_ref[1, rel]], axis=0)
        scores.append(_dot_nt(qs, kw) * (NA_HEAD_DIM ** -0.5 * LOG2_E) + bias)
        starts.append(start)
    for i in range(NA_ROWS_PER_STEP):
        s = scores[i]
        e = jnp.exp2(s - jnp.max(s, axis=-1, keepdims=True))
        inv = 1.0 / jnp.sum(e, axis=-1, keepdims=True)
        o = _dot(e.astype(BF16), v_ref[pl.ds(starts[i], win), :]) * inv
        out = jnp.where(head0_q, o[:GRID_W], o[GRID_W:])
        zz = z_ref[i * GRID_W:(i + 1) * GRID_W, :].astype(F32)
        o_ref[i * GRID_W:(i + 1) * GRID_W, :] = (out * _silu(zz)).astype(o_ref.dtype)


def _natten(p, tbl, *, batch, seq, col0):
    t = p.shape[0]
    rows = seq // GRID_W
    width = NA_HEADS * NA_HEAD_DIM
    pairs = width // LANES
    tq = NA_ROWS_PER_STEP * GRID_W
    nrb = seq // tq
    cb = col0 // LANES
    kern = functools.partial(_na_kernel, rows=rows)
    return pl.pallas_call(
        kern,
        grid=(batch, pairs, nrb),
        in_specs=[
            pl.BlockSpec((tq, LANES), lambda b, h, r: (b * nrb + r, cb + h)),
            pl.BlockSpec((seq, LANES), lambda b, h, r: (b, cb + pairs + h)),
            pl.BlockSpec((seq, LANES), lambda b, h, r: (b, cb + 2 * pairs + h)),
            pl.BlockSpec((tq, LANES), lambda b, h, r: (b * nrb + r, cb + 3 * pairs + h)),
            pl.BlockSpec((2, NA_KH, GRID_W, NA_KH * GRID_W), lambda b, h, r: (h, 0, 0, 0)),
        ],
        out_specs=pl.BlockSpec((tq, LANES), lambda b, h, r: (b * nrb + r, h)),
        out_shape=jax.ShapeDtypeStruct((t, width), BF16),
        compiler_params=_params("parallel", "parallel", "arbitrary"),
        name="natten",
    )(p, p, p, p, tbl)


def _even_out_kernel(hf_ref, hb_ref, o_ref, z_ref, na_ref, g_ref, w_ref, x_ref, out_ref):
    h = hf_ref[...].astype(F32) + hb_ref[...].astype(F32)
    hn = jnp.concatenate(
        [_rms(h[:, i * ML_HEAD_DIM:(i + 1) * ML_HEAD_DIM]) for i in range(ML_HEADS)], axis=-1)
    a = hn * g_ref[...] * _sigmoid(o_ref[...].astype(F32)) * _silu(z_ref[...].astype(F32))
    wa = ML_HEADS * ML_HEAD_DIM
    y = _dot(a.astype(BF16), w_ref[:wa, :]) + _dot(na_ref[...], w_ref[wa:, :])
    out_ref[...] = x_ref[...] + y


def _even_out(hf, hb, p, na, g, w, x, *, tm):
    t, d = x.shape
    wa = hf.shape[1]
    return pl.pallas_call(
        _even_out_kernel,
        grid=(t // tm,),
        in_specs=[
            pl.BlockSpec((tm, wa), lambda i: (i, 0)),
            pl.BlockSpec((tm, wa), lambda i: (i, 0)),
            pl.BlockSpec((tm, wa), lambda i: (i, 3)),
            pl.BlockSpec((tm, wa), lambda i: (i, 4)),
            pl.BlockSpec((tm, na.shape[1]), lambda i: (i, 0)),
            pl.BlockSpec((1, wa), lambda i: (0, 0)),
            pl.BlockSpec(w.shape, lambda i: (0, 0)),
            pl.BlockSpec((tm, d), lambda i: (i, 0)),
        ],
        out_specs=pl.BlockSpec((tm, d), lambda i: (i, 0)),
        out_shape=jax.ShapeDtypeStruct((t, d), F32),
        compiler_params=_params("parallel"),
        name="even_out_proj",
    )(hf, hb, p, p, na, g, w, x)


def _gqa_kernel(q_ref, k_ref, v_ref, o_ref, vaug_ref, m_ref, acc_ref, *, tk):
    tq = q_ref.shape[0]
    seq = k_ref.shape[0]
    dh = GQ_HEAD_DIM

    @pl.when(pl.program_id(2) == 0)
    def _():
        vaug_ref[:, :dh] = v_ref[...]
        vaug_ref[:, dh:] = jnp.ones((seq, dh), BF16)

    q = q_ref[...]
    qs = jnp.concatenate([q[:, g * dh:(g + 1) * dh] for g in range(GQ_GROUP)], axis=0)
    m_ref[...] = jnp.full_like(m_ref, -jnp.inf)
    acc_ref[...] = jnp.zeros_like(acc_ref)

    def scores(c):
        return _dot_nt(qs, k_ref[c * tk:(c + 1) * tk, :])

    def update(s, c):
        m_prev = m_ref[...]
        m_new = jnp.maximum(m_prev, jnp.max(s, axis=-1, keepdims=True))
        alpha = jnp.exp2(m_prev - m_new)
        p = jnp.exp2(s - jnp.concatenate([m_new] * (tk // LANES), axis=1))
        pv = _dot(p.astype(BF16), vaug_ref[c * tk:(c + 1) * tk, :])
        acc_ref[...] = jnp.concatenate([alpha, alpha], axis=1) * acc_ref[...] + pv
        m_ref[...] = m_new

    nk = seq // tk
    s_next = scores(0)
    for c in range(nk):
        s_cur = s_next
        if c + 1 < nk:
            s_next = scores(c + 1)
        update(s_cur, c)
    acc = acc_ref[...]
    out = acc[:, :dh] / acc[:, dh:]
    o_ref[...] = jnp.concatenate(
        [out[g * tq:(g + 1) * tq] for g in range(GQ_GROUP)], axis=-1).astype(o_ref.dtype)


def _gqa(p, *, batch, seq, k_col, v_col):
    t = p.shape[0]
    tq, tk = GQ_BLOCK_Q, GQ_BLOCK_K
    nq = seq // tq
    gw = GQ_GROUP * GQ_HEAD_DIM
    kb, vb = k_col // GQ_HEAD_DIM, v_col // GQ_HEAD_DIM
    kern = functools.partial(_gqa_kernel, tk=tk)
    return pl.pallas_call(
        kern,
        grid=(batch, GQ_KV_HEADS, nq),
        in_specs=[
            pl.BlockSpec((tq, gw), lambda b, h, i: (b * nq + i, h)),
            pl.BlockSpec((seq, GQ_HEAD_DIM), lambda b, h, i: (b, kb + h)),
            pl.BlockSpec((seq, GQ_HEAD_DIM), lambda b, h, i: (b, vb + h)),
        ],
        out_specs=pl.BlockSpec((tq, gw), lambda b, h, i: (b * nq + i, h)),
        out_shape=jax.ShapeDtypeStruct((t, GQ_HEADS * GQ_HEAD_DIM), BF16),
        scratch_shapes=[
            pltpu.VMEM((seq, 2 * GQ_HEAD_DIM), BF16),
            pltpu.VMEM((GQ_GROUP * tq, LANES), F32),
            pltpu.VMEM((GQ_GROUP * tq, 2 * GQ_HEAD_DIM), F32),
        ],
        compiler_params=_params("parallel", "parallel", "arbitrary"),
        name="gqa_attention",
    )(p, p, p)


def _odd_out_kernel(att_ref, gz_ref, cb_ref, cc_ref, cx_ref, cz_ref, ccp_ref, cxp_ref, ccn_ref, cxn_ref,
                    cw_ref, cbias_ref, w_ref, h_ref, fg_ref, out_ref, *, blocks_per_seq):
    tm = att_ref.shape[0]
    i = pl.program_id(0) % blocks_per_seq
    xc = cc_ref[...].astype(F32) * cx_ref[...].astype(F32)
    prev_row = ccp_ref[SUBLANES - 1:SUBLANES, :].astype(F32) * cxp_ref[SUBLANES - 1:SUBLANES, :].astype(F32)
    next_row = ccn_ref[0:1, :].astype(F32) * cxn_ref[0:1, :].astype(F32)
    prev_row = jnp.where(i == 0, 0.0, prev_row)
    next_row = jnp.where(i == blocks_per_seq - 1, 0.0, next_row)
    ridx = lax.broadcasted_iota(jnp.int32, xc.shape, 0)
    x_prev = jnp.where(ridx == 0, prev_row, pltpu.roll(xc, 1, 0))
    x_next = jnp.where(ridx == tm - 1, next_row, pltpu.roll(xc, tm - 1, 0))
    y = x_prev * cw_ref[0:1, :] + xc * cw_ref[1:2, :] + x_next * cw_ref[2:3, :] + cbias_ref[...]
    conv = cb_ref[...].astype(F32) * y * _silu(cz_ref[...].astype(F32))
    att = att_ref[...].astype(F32) * _silu(gz_ref[...].astype(F32))
    wa = att_ref.shape[1]
    hh = h_ref[...] + _dot(att.astype(BF16), w_ref[:wa, :]) + _dot(conv.astype(BF16), w_ref[wa:, :])
    out_ref[...] = _rms(hh) * fg_ref[...]


def _odd_out(att, p, cw, cbias, w, h, fg, *, tm, seq, cols):
    t, d = h.shape
    bps = seq // tm
    sub = tm // SUBLANES
    nsub = t // SUBLANES
    z_c, b_c, c_c, x_c, cz_c = cols

    def blk(cb):
        return pl.BlockSpec((tm, d), lambda i: (i, cb))

    def halo_prev(cb):
        return pl.BlockSpec((SUBLANES, d), lambda i: (jnp.maximum(i * sub - 1, 0), cb))

    def halo_next(cb):
        return pl.BlockSpec((SUBLANES, d), lambda i: (jnp.minimum((i + 1) * sub, nsub - 1), cb))

    kern = functools.partial(_odd_out_kernel, blocks_per_seq=bps)
    return pl.pallas_call(
        kern,
        grid=(t // tm,),
        in_specs=[
            pl.BlockSpec((tm, d), lambda i: (i, 0)),
            blk(z_c), blk(b_c), blk(c_c), blk(x_c), blk(cz_c),
            halo_prev(c_c), halo_prev(x_c), halo_next(c_c), halo_next(x_c),
            pl.BlockSpec(cw.shape, lambda i: (0, 0)),
            pl.BlockSpec((1, d), lambda i: (0, 0)),
            pl.BlockSpec(w.shape, lambda i: (0, 0)),
            pl.BlockSpec((tm, d), lambda i: (i, 0)),
            pl.BlockSpec((1, d), lambda i: (0, 0)),
        ],
        out_specs=pl.BlockSpec((tm, d), lambda i: (i, 0)),
        out_shape=jax.ShapeDtypeStruct((t, d), F32),
        compiler_params=_params("parallel"),
        name="odd_out_proj",
    )(att, p, p, p, p, p, p, p, p, p, cw, cbias, w, h, fg)


def _rope_tables(seq):
    t = jnp.arange(seq)
    pos = jnp.stack([t // GRID_W, t % GRID_W], axis=-1).astype(F32)
    inv = ROPE_THETA ** (-jnp.arange(ROPE_HALF, dtype=F32) / ROPE_HALF)
    ang = pos[:, :, None] * inv
    cos, sin = jnp.cos(ang), jnp.sin(ang)
    cos_l = jnp.stack([cos, cos], axis=1).reshape(seq, GQ_HEAD_DIM)
    sin_l = jnp.stack([-sin, sin], axis=1).reshape(seq, GQ_HEAD_DIM)
    return cos_l, sin_l


def _half_swap(a):
    lead = a.shape[:-1]
    a = a.reshape(*lead, -1, 2, GQ_HEAD_DIM // 2)
    return a[..., ::-1, :].reshape(*lead, -1)


def _rope_dim_order(a):
    lead = a.shape[:-1]
    a = a.reshape(*lead, -1, 2, 2, ROPE_HALF)
    return jnp.swapaxes(a, -3, -2).reshape(*lead, -1)


def kernel(x, norm_g, final_g, ev_w_in, ev_gate_b, ev_w_out, ev_ml_norm_g, ev_na_rpb,
           od_w_in, od_w_out, od_q_norm_g, od_k_norm_g, od_conv_w, od_conv_b):
    batch, seq, d = x.shape
    assert d == D_MODEL and norm_g.shape[0] == 2 and seq % max(ML_CHUNK, 512) == 0
    t = batch * seq
    xf = x.reshape(t, d)
    tm_in = min(1024, seq)
    tm_out = 512

    n_gate = 4 * ML_HEADS
    g0 = 5 * D_MODEL
    w_in = ev_w_in[0]
    w_gate = jnp.pad(w_in[:, g0:g0 + n_gate], ((0, 0), (0, LANES - n_gate))).astype(BF16)
    p_ev, gates = _even_in_proj(xf, norm_g[0][None, :], w_in[:, :g0].astype(BF16),
                                w_in[:, g0 + n_gate:].astype(BF16), w_gate, tm=tm_in, tn=1024)
    gate_bias = jnp.pad(ev_gate_b[0], (0, LANES - n_gate))[None, :]
    hf, hb = _mlstm(p_ev, gates, gate_bias, batch=batch, seq=seq)
    na = _natten(p_ev, _na_bias_table(ev_na_rpb[0]), batch=batch, seq=seq, col0=g0)
    h1 = _even_out(hf, hb, p_ev, na, ev_ml_norm_g[0][None, :], ev_w_out[0].astype(BF16), xf, tm=tm_out)

    w_in = od_w_in[0]
    qw, kvw = GQ_HEADS * GQ_HEAD_DIM, GQ_KV_HEADS * GQ_HEAD_DIM
    tn_odd = 512
    wq, wk = _rope_dim_order(w_in[:, :qw]), _rope_dim_order(w_in[:, qw:qw + kvw])
    w_qkv = jnp.concatenate([wq, wk, w_in[:, qw + kvw:qw + 2 * kvw]], axis=1).astype(BF16)
    w_swap = jnp.concatenate([_half_swap(wq), _half_swap(wk), jnp.zeros((d, tn_odd - kvw), w_in.dtype)],
                             axis=1).astype(BF16)
    w_plain = w_in[:, qw + 2 * kvw:].astype(BF16)
    cos, sin = _rope_tables(seq)
    qg = jnp.tile(_rope_dim_order(od_q_norm_g[0]), tn_odd // GQ_HEAD_DIM)[None, :]
    kg = jnp.tile(_rope_dim_order(od_k_norm_g[0]), GQ_KV_HEADS)[None, :]
    p_od = _odd_in_proj(h1, norm_g[1][None, :], w_qkv, w_swap, w_plain, qg, _half_swap(qg), kg, _half_swap(kg),
                        cos, sin, tm=tm_in, tn=tn_odd, seq=seq)
    k_col = p_od.shape[1] - 2 * kvw
    att = _gqa(p_od, batch=batch, seq=seq, k_col=k_col, v_col=k_col + kvw)
    out = _odd_out(att, p_od, od_conv_w[0], od_conv_b[0][None, :], od_w_out[0].astype(BF16), h1,
                   final_g[None, :], tm=tm_out, seq=seq, cols=(1, 2, 3, 4, 5))
    return out.reshape(batch, seq, d)
```

```python
import functools

import jax
import jax.numpy as jnp
from jax import lax
from jax.experimental import pallas as pl
from jax.experimental.pallas import tpu as pltpu

F32 = jnp.float32
BF16 = jnp.bfloat16

D_MODEL = 1024
GRID_W = 64
RMS_EPS = 1e-6

ML_HEADS = 4
ML_HEAD_DIM = 256
ML_CHUNK = 256

NA_HEADS = 16
NA_HEAD_DIM = 64
NA_KH = 8
NA_KW = 16
NA_ROWS_PER_STEP = 16

GQ_HEADS = 8
GQ_HEAD_DIM = 128
GQ_KV_HEADS = 2
GQ_GROUP = GQ_HEADS // GQ_KV_HEADS
ROPE_THETA = 10000.0
ROPE_HALF = GQ_HEAD_DIM // 4
GQ_BLOCK_Q = 256
GQ_BLOCK_K = 512

CV_K = 3

LANES = 128
SUBLANES = 8
MASK_VALUE = -1e30
LOG2_E = 1.4426950408889634

VMEM_LIMIT = 56 * 1024 * 1024


def _params(*sem):
    return pltpu.CompilerParams(dimension_semantics=sem, vmem_limit_bytes=VMEM_LIMIT)


def _sigmoid(x):
    return 0.5 * jnp.tanh(0.5 * x) + 0.5


def _silu(x):
    return x * _sigmoid(x)


def _log_sigmoid(x):
    return jnp.minimum(x, 0.0) - jnp.log1p(jnp.exp(-jnp.abs(x)))


def _rms(x, eps=RMS_EPS):
    return x * lax.rsqrt(jnp.mean(x * x, axis=-1, keepdims=True) + eps)


def _dot(a, b):
    return jnp.dot(a, b, preferred_element_type=F32)


def _dot_nt(a, b):
    return lax.dot_general(a, b, (((1,), (1,)), ((), ())), preferred_element_type=F32)


def _dot_tn(a, b):
    return lax.dot_general(a, b, (((0,), (0,)), ((), ())), preferred_element_type=F32)


def _even_in_kernel(x_ref, g_ref, wa_ref, wb_ref, wg_ref, o_ref, og_ref, u_ref, *, n_a_tiles):
    j = pl.program_id(1)

    @pl.when(j == 0)
    def _():
        u = (_rms(x_ref[...]) * g_ref[...]).astype(BF16)
        u_ref[...] = u
        og_ref[...] = _dot(u, wg_ref[...])

    @pl.when(j < n_a_tiles)
    def _():
        o_ref[...] = _dot(u_ref[...], wa_ref[...]).astype(o_ref.dtype)

    @pl.when(j >= n_a_tiles)
    def _():
        o_ref[...] = _dot(u_ref[...], wb_ref[...]).astype(o_ref.dtype)


def _even_in_proj(x, g, wa, wb, wg, *, na, tm, tn):
    t, d = x.shape
    nb = wb.shape[1] // tn
    n = (na + nb) * tn
    kern = functools.partial(_even_in_kernel, n_a_tiles=na)
    return pl.pallas_call(
        kern,
        grid=(t // tm, na + nb),
        in_specs=[
            pl.BlockSpec((tm, d), lambda i, j: (i, 0)),
            pl.BlockSpec((1, d), lambda i, j: (0, 0)),
            pl.BlockSpec((d, tn), lambda i, j: (0, jnp.minimum(j, na - 1))),
            pl.BlockSpec((d, tn), lambda i, j: (0, jnp.maximum(j - na, 0))),
            pl.BlockSpec((d, LANES), lambda i, j: (0, 0)),
        ],
        out_specs=[
            pl.BlockSpec((tm, tn), lambda i, j: (i, j)),
            pl.BlockSpec((tm, LANES), lambda i, j: (i, 0)),
        ],
        out_shape=[
            jax.ShapeDtypeStruct((t, n), BF16),
            jax.ShapeDtypeStruct((t, LANES), F32),
        ],
        scratch_shapes=[pltpu.VMEM((tm, d), BF16)],
        compiler_params=_params("parallel", "arbitrary"),
        name="even_in_proj",
    )(x, g, wa, wb, wg)


def _rope_norm(acc, acc_sw, gain, gain_sw, cos, sin, scale):
    nh = acc.shape[1] // GQ_HEAD_DIM
    heads = []
    for h in range(nh):
        sl = slice(h * GQ_HEAD_DIM, (h + 1) * GQ_HEAD_DIM)
        a = acc[:, sl]
        r = lax.rsqrt(jnp.mean(a * a, axis=-1, keepdims=True) + RMS_EPS)
        y = a * r * gain[:, sl]
        partner = acc_sw[:, sl] * r * gain_sw[:, sl]
        heads.append((y * cos + partner * sin) * scale)
    return jnp.concatenate(heads, axis=-1)


def _odd_in_kernel(x_ref, g_ref, wa_ref, wsw_ref, wb_ref, qg_ref, qgs_ref, kg_ref, kgs_ref, cos_ref, sin_ref,
                   o_ref, u_ref, *, n_q_tiles, kv_tile):
    j = pl.program_id(1)

    @pl.when(j == 0)
    def _():
        u_ref[...] = (_rms(x_ref[...]) * g_ref[...]).astype(BF16)

    is_q = j < n_q_tiles
    is_kv = j == kv_tile

    @pl.when(is_q)
    def _():
        u = u_ref[...]
        o_ref[...] = _rope_norm(_dot(u, wa_ref[...]), _dot(u, wsw_ref[...]), qg_ref[...], qgs_ref[...],
                                cos_ref[...], sin_ref[...], GQ_HEAD_DIM ** -0.5 * LOG2_E).astype(o_ref.dtype)

    @pl.when(is_kv)
    def _():
        kw = GQ_KV_HEADS * GQ_HEAD_DIM
        u = u_ref[...]
        acc = _dot(u, wa_ref[...])
        k = _rope_norm(acc[:, :kw], _dot(u, wsw_ref[:, :kw]), kg_ref[...], kgs_ref[...],
                       cos_ref[...], sin_ref[...], 1.0)
        o_ref[...] = jnp.concatenate([k, acc[:, kw:]], axis=-1).astype(o_ref.dtype)

    @pl.when(jnp.logical_not(jnp.logical_or(is_q, is_kv)))
    def _():
        o_ref[...] = _dot(u_ref[...], wb_ref[...]).astype(o_ref.dtype)


def _odd_in_proj(x, g, wa, wsw, wb, qg, qgs, kg, kgs, cos, sin, *, plain0, n_plain, tm, tn, seq):
    t, d = x.shape
    n_q_tiles = (GQ_HEADS * GQ_HEAD_DIM) // tn
    kv_tile = n_q_tiles + n_plain
    n = (kv_tile + 1) * tn
    kvw = GQ_KV_HEADS * GQ_HEAD_DIM
    sb = seq // tm
    kern = functools.partial(_odd_in_kernel, n_q_tiles=n_q_tiles, kv_tile=kv_tile)

    def qkv_map(i, j):
        return (0, jnp.where(j < n_q_tiles, j, jnp.where(j == kv_tile, n_q_tiles, n_q_tiles - 1)))

    return pl.pallas_call(
        kern,
        grid=(t // tm, kv_tile + 1),
        in_specs=[
            pl.BlockSpec((tm, d), lambda i, j: (i, 0)),
            pl.BlockSpec((1, d), lambda i, j: (0, 0)),
            pl.BlockSpec((d, tn), qkv_map),
            pl.BlockSpec((d, tn), qkv_map),
            pl.BlockSpec((d, tn), lambda i, j: (0, plain0 + jnp.clip(j - n_q_tiles, 0, n_plain - 1))),
            pl.BlockSpec((1, tn), lambda i, j: (0, 0)),
            pl.BlockSpec((1, tn), lambda i, j: (0, 0)),
            pl.BlockSpec((1, kvw), lambda i, j: (0, 0)),
            pl.BlockSpec((1, kvw), lambda i, j: (0, 0)),
            pl.BlockSpec((tm, GQ_HEAD_DIM), lambda i, j: (i % sb, 0)),
            pl.BlockSpec((tm, GQ_HEAD_DIM), lambda i, j: (i % sb, 0)),
        ],
        out_specs=pl.BlockSpec((tm, tn), lambda i, j: (i, j)),
        out_shape=jax.ShapeDtypeStruct((t, n), BF16),
        scratch_shapes=[pltpu.VMEM((tm, d), BF16)],
        compiler_params=_params("parallel", "arbitrary"),
        name="odd_in_proj",
    )(x, g, wa, wsw, wb, qg, qgs, kg, kgs, cos, sin)


def _split3(a):
    hi = a.astype(BF16)
    r = a - hi.astype(F32)
    mid = r.astype(BF16)
    return hi, mid, (r - mid.astype(F32)).astype(BF16)


def _dot_exact_lhs(a, b):
    return sum(_dot(p, b) for p in _split3(a))


def _dot_exact_rhs(a, b):
    return sum(_dot(a, p) for p in _split3(b))


def _mlstm_kernel(qf_ref, kf_ref, vf_ref, gf_ref, qb_ref, kb_ref, vb_ref, gb_ref, bias_ref, sel_ref,
                  hf_ref, hb_ref, c_ref, n_ref, m_ref):
    L = qf_ref.shape[0]
    dh = ML_HEAD_DIM
    k_scale = dh ** -0.5

    @pl.when(pl.program_id(1) == 0)
    def _():
        c_ref[...] = jnp.zeros_like(c_ref)
        n_ref[...] = jnp.zeros_like(n_ref)
        m_ref[...] = jnp.zeros_like(m_ref)

    row = lax.broadcasted_iota(jnp.int32, (L, L), 0)
    col = lax.broadcasted_iota(jnp.int32, (L, L), 1)
    lower = row >= col
    upper = row <= col
    lane = lax.broadcasted_iota(jnp.int32, (L, LANES), 1)
    is_forget = ((lane >= ML_HEADS) & (lane < 2 * ML_HEADS)) | ((lane >= 3 * ML_HEADS) & (lane < 4 * ML_HEADS))

    def wide(x):
        return jnp.concatenate([x] * (dh // LANES), axis=1)

    def gate_tables(g_ref, tri, first_col):
        g = g_ref[...] + bias_ref[...]
        lf = jnp.where(is_forget, _log_sigmoid(g), 0.0)
        cum = _dot_exact_rhs(tri.astype(BF16), lf)
        tab = jnp.where(is_forget, cum, g) * LOG2_E
        ncol = 2 * ML_HEADS * LANES
        rep = _dot_exact_lhs(tab, sel_ref[:, first_col * LANES:first_col * LANES + ncol])
        return rep, tab.T

    rep_f, tab_ft = gate_tables(gf_ref, lower, 0)
    rep_b, tab_bt = gate_tables(gb_ref, upper, 2 * ML_HEADS)

    def chain(idx, q, k, v, b128, ig128, brow, igrow, g128, valid):
        c = c_ref[idx]
        n = n_ref[idx][0:1, :]
        m_prev = m_ref[idx][0:1, :]
        d = jnp.where(valid, wide(b128) + (igrow - brow), MASK_VALUE)
        inter = b128 + m_prev
        m_t = jnp.maximum(inter, jnp.max(d, axis=-1, keepdims=True))
        w = jnp.exp2(d - wide(m_t)) * _dot_nt(q * jnp.asarray(k_scale, q.dtype), k)
        decay = jnp.exp2(inter - m_t)
        num = wide(decay) * _dot(q, c.astype(BF16)) + _dot(w.astype(BF16), v)
        qn = jnp.sum(q.astype(F32) * n, axis=-1, keepdims=True)
        den = decay * qn + jnp.sum(w, axis=-1, keepdims=True)
        h = num * wide(1.0 / jnp.maximum(jnp.abs(den), jnp.exp2(-m_t)))
        a = g128 - b128 + ig128
        m_new = jnp.maximum(g128 + m_prev, jnp.max(a, axis=0, keepdims=True))
        carry = wide(jnp.exp2(g128 + m_prev - m_new))
        wk = wide(jnp.exp2(a - m_new) * k_scale)
        wkv = (wk * v.astype(F32)).astype(BF16)
        c_ref[idx] = carry * c + _dot_tn(k, wkv)
        n_new = carry * n + jnp.sum(wk * k.astype(F32), axis=0, keepdims=True)
        n_ref[idx] = jnp.broadcast_to(n_new, n_ref.shape[1:])
        m_ref[idx] = jnp.broadcast_to(m_new, m_ref.shape[1:])
        return h

    def lanes(rep, j):
        return rep[:, j * LANES:(j + 1) * LANES]

    for hd in range(ML_HEADS):
        sl = slice(hd * dh, (hd + 1) * dh)
        fi, ff = hd, ML_HEADS + hd
        b128 = lanes(rep_f, ff)
        h_f = chain(hd, qf_ref[:, sl], kf_ref[:, sl], vf_ref[:, sl], b128, lanes(rep_f, fi),
                    tab_ft[ff:ff + 1, :], tab_ft[fi:fi + 1, :], b128[L - 1:L, :], lower)
        hf_ref[:, sl] = h_f.astype(hf_ref.dtype)
        b128 = lanes(rep_b, ff)
        go = 2 * ML_HEADS
        h_b = chain(ML_HEADS + hd, qb_ref[:, sl], kb_ref[:, sl], vb_ref[:, sl], b128, lanes(rep_b, fi),
                    tab_bt[go + ff:go + ff + 1, :], tab_bt[go + fi:go + fi + 1, :], b128[0:1, :], upper)
        hb_ref[:, sl] = h_b.astype(hb_ref.dtype)


def _mlstm(p, gates, bias, *, batch, seq):
    t = p.shape[0]
    L = ML_CHUNK
    nc = seq // L
    w = ML_HEADS * ML_HEAD_DIM
    n_gate = 4 * ML_HEADS
    sel = (jnp.arange(LANES)[:, None] == (jnp.arange(n_gate * LANES) // LANES)[None, :]).astype(BF16)

    def fwd(cb):
        return lambda b, c: (b * nc + c, cb)

    def bwd(cb):
        return lambda b, c: (b * nc + nc - 1 - c, cb)

    return pl.pallas_call(
        _mlstm_kernel,
        grid=(batch, nc),
        in_specs=[
            pl.BlockSpec((L, w), fwd(0)), pl.BlockSpec((L, w), fwd(1)), pl.BlockSpec((L, w), fwd(2)),
            pl.BlockSpec((L, LANES), fwd(0)),
            pl.BlockSpec((L, w), bwd(0)), pl.BlockSpec((L, w), bwd(1)), pl.BlockSpec((L, w), bwd(2)),
            pl.BlockSpec((L, LANES), bwd(0)),
            pl.BlockSpec((1, LANES), lambda b, c: (0, 0)),
            pl.BlockSpec(sel.shape, lambda b, c: (0, 0)),
        ],
        out_specs=[pl.BlockSpec((L, w), fwd(0)), pl.BlockSpec((L, w), bwd(0))],
        out_shape=[jax.ShapeDtypeStruct((t, w), BF16), jax.ShapeDtypeStruct((t, w), BF16)],
        scratch_shapes=[
            pltpu.VMEM((2 * ML_HEADS, ML_HEAD_DIM, ML_HEAD_DIM), F32),
            pltpu.VMEM((2 * ML_HEADS, SUBLANES, ML_HEAD_DIM), F32),
            pltpu.VMEM((2 * ML_HEADS, SUBLANES, LANES), F32),
        ],
        compiler_params=_params("parallel", "arbitrary"),
        name="mlstm",
    )(p, p, p, gates, p, p, p, gates, bias, sel)


def _na_table_kernel(rp_ref, o_ref):
    c = lax.broadcasted_iota(jnp.int32, (GRID_W, LANES), 0)
    lane = lax.broadcasted_iota(jnp.int32, (GRID_W, LANES), 1)
    c0 = jnp.clip(c - NA_KW // 2, 0, GRID_W - NA_KW)
    low_half = lane < GRID_W
    kc = jnp.where(low_half, lane, lane - GRID_W)
    in_window = (kc >= c0) & (kc < c0 + NA_KW)
    lo, hi = [], []
    for dr in range(2 * NA_KH - 1):
        row = jnp.broadcast_to(rp_ref[0, dr:dr + 1, :], (GRID_W, LANES))
        lo.append(jnp.where(in_window & low_half, pltpu.roll(row, GRID_W + 1, 1, stride=1, stride_axis=0),
                            MASK_VALUE))
        hi.append(jnp.where(in_window & jnp.logical_not(low_half),
                            pltpu.roll(row, 1, 1, stride=1, stride_axis=0), MASK_VALUE))
    for rel in range(NA_KH):
        pairs = [jnp.maximum(lo[2 * j - rel + NA_KH - 1], hi[2 * j - rel + NA_KH]) for j in range(NA_KH // 2)]
        o_ref[0, rel] = jnp.concatenate(pairs, axis=1)


def _na_bias_table(rpb):
    nh, ndr, ndc = rpb.shape
    lead = GRID_W - NA_KW
    rp = jnp.pad(rpb.astype(F32) * LOG2_E, ((0, 0), (0, 2 * NA_KH - ndr), (lead, LANES - lead - ndc)))
    return pl.pallas_call(
        _na_table_kernel,
        grid=(nh,),
        in_specs=[pl.BlockSpec((1, 2 * NA_KH, LANES), lambda h: (h, 0, 0))],
        out_specs=pl.BlockSpec((1, NA_KH, GRID_W, NA_KH * GRID_W), lambda h: (h, 0, 0, 0)),
        out_shape=jax.ShapeDtypeStruct((nh, NA_KH, GRID_W, NA_KH * GRID_W), F32),
        compiler_params=_params("parallel"),
        name="na_bias_table",
    )(rp)


def _na_kernel(q_ref, k_ref, v_ref, z_ref, tbl_ref, o_ref, *, rows):
    rb = pl.program_id(2)
    win = NA_KH * GRID_W
    lane_q = lax.broadcasted_iota(jnp.int32, (GRID_W, LANES), 1)
    head0_q = lane_q < NA_HEAD_DIM
    starts, scores = [], []
    for i in range(NA_ROWS_PER_STEP):
        r = rb * NA_ROWS_PER_STEP + i
        r0 = jnp.clip(r - NA_KH // 2, 0, rows - NA_KH)
        rel = r - r0
        start = pl.multiple_of(r0 * GRID_W, GRID_W)
        kw = k_ref[pl.ds(start, win), :]
        q = q_ref[i * GRID_W:(i + 1) * GRID_W, :]
        zero = jnp.zeros_like(q)
        qs = jnp.concatenate([jnp.where(head0_q, q, zero), jnp.where(head0_q, zero, q)], axis=0)
        bias = jnp.concatenate([tbl_ref[0, rel], tbl_ref[1, rel]], axis=0)
        scores.append(_dot_nt(qs, kw) * (NA_HEAD_DIM ** -0.5 * LOG2_E) + bias)
        starts.append(start)
    for i in range(NA_ROWS_PER_STEP):
        s = scores[i]
        e = jnp.exp2(s - jnp.max(s, axis=-1, keepdims=True))
        inv = 1.0 / jnp.sum(e, axis=-1, keepdims=True)
        o = _dot(e.astype(BF16), v_ref[pl.ds(starts[i], win), :]) * inv
        out = jnp.where(head0_q, o[:GRID_W], o[GRID_W:])
        zz = z_ref[i * GRID_W:(i + 1) * GRID_W, :].astype(F32)
        o_ref[i * GRID_W:(i + 1) * GRID_W, :] = (out * _silu(zz)).astype(o_ref.dtype)


def _natten(p, tbl, *, batch, seq, col0):
    t = p.shape[0]
    rows = seq // GRID_W
    width = NA_HEADS * NA_HEAD_DIM
    pairs = width // LANES
    tq = NA_ROWS_PER_STEP * GRID_W
    nrb = seq // tq
    cb = col0 // LANES
    kern = functools.partial(_na_kernel, rows=rows)
    return pl.pallas_call(
        kern,
        grid=(batch, pairs, nrb),
        in_specs=[
            pl.BlockSpec((tq, LANES), lambda b, h, r: (b * nrb + r, cb + h)),
            pl.BlockSpec((seq, LANES), lambda b, h, r: (b, cb + pairs + h)),
            pl.BlockSpec((seq, LANES), lambda b, h, r: (b, cb + 2 * pairs + h)),
            pl.BlockSpec((tq, LANES), lambda b, h, r: (b * nrb + r, cb + 3 * pairs + h)),
            pl.BlockSpec((2, NA_KH, GRID_W, NA_KH * GRID_W), lambda b, h, r: (h, 0, 0, 0)),
        ],
        out_specs=pl.BlockSpec((tq, LANES), lambda b, h, r: (b * nrb + r, h)),
        out_shape=jax.ShapeDtypeStruct((t, width), BF16),
        compiler_params=_params("parallel", "parallel", "arbitrary"),
        name="natten",
    )(p, p, p, p, tbl)


def _even_out_kernel(hf_ref, hb_ref, o_ref, z_ref, na_ref, g_ref, w_ref, x_ref, out_ref):
    h = hf_ref[...].astype(F32) + hb_ref[...].astype(F32)
    hn = jnp.concatenate(
        [_rms(h[:, i * ML_HEAD_DIM:(i + 1) * ML_HEAD_DIM]) for i in range(ML_HEADS)], axis=-1)
    a = hn * g_ref[...] * _sigmoid(o_ref[...].astype(F32)) * _silu(z_ref[...].astype(F32))
    wa = ML_HEADS * ML_HEAD_DIM
    y = _dot(a.astype(BF16), w_ref[:wa, :]) + _dot(na_ref[...], w_ref[wa:, :])
    out_ref[...] = x_ref[...] + y


def _even_out(hf, hb, p, na, g, w, x, *, tm):
    t, d = x.shape
    wa = hf.shape[1]
    return pl.pallas_call(
        _even_out_kernel,
        grid=(t // tm,),
        in_specs=[
            pl.BlockSpec((tm, wa), lambda i: (i, 0)),
            pl.BlockSpec((tm, wa), lambda i: (i, 0)),
            pl.BlockSpec((tm, wa), lambda i: (i, 3)),
            pl.BlockSpec((tm, wa), lambda i: (i, 4)),
            pl.BlockSpec((tm, na.shape[1]), lambda i: (i, 0)),
            pl.BlockSpec((1, wa), lambda i: (0, 0)),
            pl.BlockSpec(w.shape, lambda i: (0, 0)),
            pl.BlockSpec((tm, d), lambda i: (i, 0)),
        ],
        out_specs=pl.BlockSpec((tm, d), lambda i: (i, 0)),
        out_shape=jax.ShapeDtypeStruct((t, d), F32),
        compiler_params=_params("parallel"),
        name="even_out_proj",
    )(hf, hb, p, p, na, g, w, x)


def _gqa_kernel(q_ref, k_ref, v_ref, o_ref, vaug_ref, m_ref, acc_ref, *, tk):
    tq = q_ref.shape[0]
    seq = k_ref.shape[0]
    dh = GQ_HEAD_DIM

    @pl.when(pl.program_id(2) == 0)
    def _():
        vaug_ref[:, :dh] = v_ref[...]
        vaug_ref[:, dh:] = jnp.ones((seq, dh), BF16)

    q = q_ref[...]
    qs = jnp.concatenate([q[:, g * dh:(g + 1) * dh] for g in range(GQ_GROUP)], axis=0)
    m_ref[...] = jnp.full_like(m_ref, -jnp.inf)
    acc_ref[...] = jnp.zeros_like(acc_ref)

    def scores(c):
        return _dot_nt(qs, k_ref[c * tk:(c + 1) * tk, :])

    def update(s, c):
        m_prev = m_ref[...]
        m_new = jnp.maximum(m_prev, jnp.max(s, axis=-1, keepdims=True))
        alpha = jnp.exp2(m_prev - m_new)
        p = jnp.exp2(s - jnp.concatenate([m_new] * (tk // LANES), axis=1))
        pv = _dot(p.astype(BF16), vaug_ref[c * tk:(c + 1) * tk, :])
        acc_ref[...] = jnp.concatenate([alpha, alpha], axis=1) * acc_ref[...] + pv
        m_ref[...] = m_new

    nk = seq // tk
    s_next = scores(0)
    for c in range(nk):
        s_cur = s_next
        if c + 1 < nk:
            s_next = scores(c + 1)
        update(s_cur, c)
    acc = acc_ref[...]
    out = acc[:, :dh] / acc[:, dh:]
    o_ref[...] = jnp.concatenate(
        [out[g * tq:(g + 1) * tq] for g in range(GQ_GROUP)], axis=-1).astype(o_ref.dtype)


def _gqa(p, *, batch, seq, k_col, v_col):
    t = p.shape[0]
    tq, tk = GQ_BLOCK_Q, GQ_BLOCK_K
    nq = seq // tq
    gw = GQ_GROUP * GQ_HEAD_DIM
    kb, vb = k_col // GQ_HEAD_DIM, v_col // GQ_HEAD_DIM
    kern = functools.partial(_gqa_kernel, tk=tk)
    return pl.pallas_call(
        kern,
        grid=(batch, GQ_KV_HEADS, nq),
        in_specs=[
            pl.BlockSpec((tq, gw), lambda b, h, i: (b * nq + i, h)),
            pl.BlockSpec((seq, GQ_HEAD_DIM), lambda b, h, i: (b, kb + h)),
            pl.BlockSpec((seq, GQ_HEAD_DIM), lambda b, h, i: (b, vb + h)),
        ],
        out_specs=pl.BlockSpec((tq, gw), lambda b, h, i: (b * nq + i, h)),
        out_shape=jax.ShapeDtypeStruct((t, GQ_HEADS * GQ_HEAD_DIM), BF16),
        scratch_shapes=[
            pltpu.VMEM((seq, 2 * GQ_HEAD_DIM), BF16),
            pltpu.VMEM((GQ_GROUP * tq, LANES), F32),
            pltpu.VMEM((GQ_GROUP * tq, 2 * GQ_HEAD_DIM), F32),
        ],
        compiler_params=_params("parallel", "parallel", "arbitrary"),
        name="gqa_attention",
    )(p, p, p)


def _odd_out_kernel(att_ref, gz_ref, cb_ref, cc_ref, cx_ref, cz_ref, ccp_ref, cxp_ref, ccn_ref, cxn_ref,
                    cw_ref, cbias_ref, w_ref, h_ref, fg_ref, out_ref, *, blocks_per_seq):
    tm = att_ref.shape[0]
    i = pl.program_id(0) % blocks_per_seq
    xc = cc_ref[...].astype(F32) * cx_ref[...].astype(F32)
    prev_row = ccp_ref[SUBLANES - 1:SUBLANES, :].astype(F32) * cxp_ref[SUBLANES - 1:SUBLANES, :].astype(F32)
    next_row = ccn_ref[0:1, :].astype(F32) * cxn_ref[0:1, :].astype(F32)
    prev_row = jnp.where(i == 0, 0.0, prev_row)
    next_row = jnp.where(i == blocks_per_seq - 1, 0.0, next_row)
    ridx = lax.broadcasted_iota(jnp.int32, xc.shape, 0)
    x_prev = jnp.where(ridx == 0, prev_row, pltpu.roll(xc, 1, 0))
    x_next = jnp.where(ridx == tm - 1, next_row, pltpu.roll(xc, tm - 1, 0))
    y = x_prev * cw_ref[0:1, :] + xc * cw_ref[1:2, :] + x_next * cw_ref[2:3, :] + cbias_ref[...]
    conv = cb_ref[...].astype(F32) * y * _silu(cz_ref[...].astype(F32))
    att = att_ref[...].astype(F32) * _silu(gz_ref[...].astype(F32))
    wa = att_ref.shape[1]
    hh = h_ref[...] + _dot(att.astype(BF16), w_ref[:wa, :]) + _dot(conv.astype(BF16), w_ref[wa:, :])
    out_ref[...] = _rms(hh) * fg_ref[...]


def _odd_out(att, p, cw, cbias, w, h, fg, *, tm, seq, cols):
    t, d = h.shape
    bps = seq // tm
    sub = tm // SUBLANES
    nsub = t // SUBLANES
    z_c, b_c, c_c, x_c, cz_c = cols

    def blk(cb):
        return pl.BlockSpec((tm, d), lambda i: (i, cb))

    def halo_prev(cb):
        return pl.BlockSpec((SUBLANES, d), lambda i: (jnp.maximum(i * sub - 1, 0), cb))

    def halo_next(cb):
        return pl.BlockSpec((SUBLANES, d), lambda i: (jnp.minimum((i + 1) * sub, nsub - 1), cb))

    kern = functools.partial(_odd_out_kernel, blocks_per_seq=bps)
    return pl.pallas_call(
        kern,
        grid=(t // tm,),
        in_specs=[
            pl.BlockSpec((tm, d), lambda i: (i, 0)),
            blk(z_c), blk(b_c), blk(c_c), blk(x_c), blk(cz_c),
            halo_prev(c_c), halo_prev(x_c), halo_next(c_c), halo_next(x_c),
            pl.BlockSpec(cw.shape, lambda i: (0, 0)),
            pl.BlockSpec((1, d), lambda i: (0, 0)),
            pl.BlockSpec(w.shape, lambda i: (0, 0)),
            pl.BlockSpec((tm, d), lambda i: (i, 0)),
            pl.BlockSpec((1, d), lambda i: (0, 0)),
        ],
        out_specs=pl.BlockSpec((tm, d), lambda i: (i, 0)),
        out_shape=jax.ShapeDtypeStruct((t, d), F32),
        compiler_params=_params("parallel"),
        name="odd_out_proj",
    )(att, p, p, p, p, p, p, p, p, p, cw, cbias, w, h, fg)


def _rope_tables(seq):
    t = jnp.arange(seq)
    pos = jnp.stack([t // GRID_W, t % GRID_W], axis=-1).astype(F32)
    inv = ROPE_THETA ** (-jnp.arange(ROPE_HALF, dtype=F32) / ROPE_HALF)
    ang = pos[:, :, None] * inv
    cos, sin = jnp.cos(ang), jnp.sin(ang)
    cos_l = jnp.stack([cos, cos], axis=1).reshape(seq, GQ_HEAD_DIM)
    sin_l = jnp.stack([-sin, sin], axis=1).reshape(seq, GQ_HEAD_DIM)
    return cos_l, sin_l


def _half_swap(a):
    lead = a.shape[:-1]
    a = a.reshape(*lead, -1, 2, GQ_HEAD_DIM // 2)
    return a[..., ::-1, :].reshape(*lead, -1)


def _rope_dim_order(a):
    lead = a.shape[:-1]
    a = a.reshape(*lead, -1, 2, 2, ROPE_HALF)
    return jnp.swapaxes(a, -3, -2).reshape(*lead, -1)


def kernel(x, norm_g, final_g, ev_w_in, ev_gate_b, ev_w_out, ev_ml_norm_g, ev_na_rpb,
           od_w_in, od_w_out, od_q_norm_g, od_k_norm_g, od_conv_w, od_conv_b):
    batch, seq, d = x.shape
    assert d == D_MODEL and norm_g.shape[0] == 2 and seq % max(ML_CHUNK, 512) == 0
    t = batch * seq
    xf = x.reshape(t, d)
    tm_in = min(2048, seq)
    tm_out = 512

    n_gate = 4 * ML_HEADS
    g0 = 5 * D_MODEL
    w_in = ev_w_in[0].astype(BF16)
    w_gate = jnp.pad(w_in[:, g0:g0 + n_gate], ((0, 0), (0, LANES - n_gate)))
    tn_even = 1024
    p_ev, gates = _even_in_proj(xf, norm_g[0][None, :], w_in, w_in[:, g0 + n_gate:], w_gate,
                                na=g0 // tn_even, tm=tm_in, tn=tn_even)
    gate_bias = jnp.pad(ev_gate_b[0], (0, LANES - n_gate))[None, :]
    hf, hb = _mlstm(p_ev, gates, gate_bias, batch=batch, seq=seq)
    na = _natten(p_ev, _na_bias_table(ev_na_rpb[0]), batch=batch, seq=seq, col0=g0)
    h1 = _even_out(hf, hb, p_ev, na, ev_ml_norm_g[0][None, :], ev_w_out[0].astype(BF16), xf, tm=tm_out)

    w_in = od_w_in[0]
    qw, kvw = GQ_HEADS * GQ_HEAD_DIM, GQ_KV_HEADS * GQ_HEAD_DIM
    tn_odd = 512
    wq, wk = _rope_dim_order(w_in[:, :qw]), _rope_dim_order(w_in[:, qw:qw + kvw])
    w_qkv = jnp.concatenate([wq, wk, w_in[:, qw + kvw:qw + 2 * kvw]], axis=1).astype(BF16)
    w_swap = jnp.concatenate([_half_swap(wq), _half_swap(wk), jnp.zeros((d, tn_odd - kvw), w_in.dtype)],
                             axis=1).astype(BF16)
    cos, sin = _rope_tables(seq)
    qg = jnp.tile(_rope_dim_order(od_q_norm_g[0]), tn_odd // GQ_HEAD_DIM)[None, :]
    kg = jnp.tile(_rope_dim_order(od_k_norm_g[0]), GQ_KV_HEADS)[None, :]
    plain0 = (qw + 2 * kvw) // tn_odd
    p_od = _odd_in_proj(h1, norm_g[1][None, :], w_qkv, w_swap, w_in.astype(BF16), qg, _half_swap(qg),
                        kg, _half_swap(kg), cos, sin, plain0=plain0, n_plain=w_in.shape[1] // tn_odd - plain0,
                        tm=tm_in, tn=tn_odd, seq=seq)
    k_col = p_od.shape[1] - 2 * kvw
    att = _gqa(p_od, batch=batch, seq=seq, k_col=k_col, v_col=k_col + kvw)
    out = _odd_out(att, p_od, od_conv_w[0], od_conv_b[0][None, :], od_w_out[0].astype(BF16), h1,
                   final_g[None, :], tm=tm_out, seq=seq, cols=(1, 2, 3, 4, 5))
    return out.reshape(batch, seq, d)
```

```python
import functools

import jax
import jax.numpy as jnp
from jax import lax
from jax.experimental import pallas as pl
from jax.experimental.pallas import tpu as pltpu

F32 = jnp.float32
BF16 = jnp.bfloat16

D_MODEL = 1024
GRID_W = 64
RMS_EPS = 1e-6

ML_HEADS = 4
ML_HEAD_DIM = 256
ML_CHUNK = 256

NA_HEADS = 16
NA_HEAD_DIM = 64
NA_KH = 8
NA_KW = 16
NA_ROWS_PER_STEP = 32

GQ_HEADS = 8
GQ_HEAD_DIM = 128
GQ_KV_HEADS = 2
GQ_GROUP = GQ_HEADS // GQ_KV_HEADS
ROPE_THETA = 10000.0
ROPE_HALF = GQ_HEAD_DIM // 4
GQ_BLOCK_Q = 256
GQ_BLOCK_K = 512

CV_K = 3

LANES = 128
SUBLANES = 8
MASK_VALUE = -1e30
LOG2_E = 1.4426950408889634

VMEM_LIMIT = 56 * 1024 * 1024


def _params(*sem):
    return pltpu.CompilerParams(dimension_semantics=sem, vmem_limit_bytes=VMEM_LIMIT)


def _sigmoid(x):
    return 0.5 * jnp.tanh(0.5 * x) + 0.5


def _silu(x):
    return x * _sigmoid(x)


def _log_sigmoid(x):
    return jnp.minimum(x, 0.0) - jnp.log1p(jnp.exp(-jnp.abs(x)))


def _rms(x, eps=RMS_EPS):
    return x * lax.rsqrt(jnp.mean(x * x, axis=-1, keepdims=True) + eps)


def _dot(a, b):
    return jnp.dot(a, b, preferred_element_type=F32)


def _dot_nt(a, b):
    return lax.dot_general(a, b, (((1,), (1,)), ((), ())), preferred_element_type=F32)


def _dot_tn(a, b):
    return lax.dot_general(a, b, (((0,), (0,)), ((), ())), preferred_element_type=F32)


def _even_in_kernel(x_ref, g_ref, wa_ref, wb_ref, wg_ref, o_ref, og_ref, u_ref, *, n_a_tiles):
    j = pl.program_id(1)

    @pl.when(j == 0)
    def _():
        u = (_rms(x_ref[...]) * g_ref[...]).astype(BF16)
        u_ref[...] = u
        og_ref[...] = _dot(u, wg_ref[...])

    @pl.when(j < n_a_tiles)
    def _():
        o_ref[...] = _dot(u_ref[...], wa_ref[...]).astype(o_ref.dtype)

    @pl.when(j >= n_a_tiles)
    def _():
        o_ref[...] = _dot(u_ref[...], wb_ref[...]).astype(o_ref.dtype)


def _even_in_proj(x, g, wa, wb, wg, *, na, tm, tn):
    t, d = x.shape
    nb = wb.shape[1] // tn
    n = (na + nb) * tn
    kern = functools.partial(_even_in_kernel, n_a_tiles=na)
    return pl.pallas_call(
        kern,
        grid=(t // tm, na + nb),
        in_specs=[
            pl.BlockSpec((tm, d), lambda i, j: (i, 0)),
            pl.BlockSpec((1, d), lambda i, j: (0, 0)),
            pl.BlockSpec((d, tn), lambda i, j: (0, jnp.minimum(j, na - 1))),
            pl.BlockSpec((d, tn), lambda i, j: (0, jnp.maximum(j - na, 0))),
            pl.BlockSpec((d, LANES), lambda i, j: (0, 0)),
        ],
        out_specs=[
            pl.BlockSpec((tm, tn), lambda i, j: (i, j)),
            pl.BlockSpec((tm, LANES), lambda i, j: (i, 0)),
        ],
        out_shape=[
            jax.ShapeDtypeStruct((t, n), BF16),
            jax.ShapeDtypeStruct((t, LANES), F32),
        ],
        scratch_shapes=[pltpu.VMEM((tm, d), BF16)],
        compiler_params=_params("parallel", "arbitrary"),
        name="even_in_proj",
    )(x, g, wa, wb, wg)


def _rope_norm(acc, acc_sw, gain, gain_sw, cos, sin, scale):
    nh = acc.shape[1] // GQ_HEAD_DIM
    heads = []
    for h in range(nh):
        sl = slice(h * GQ_HEAD_DIM, (h + 1) * GQ_HEAD_DIM)
        a = acc[:, sl]
        r = lax.rsqrt(jnp.mean(a * a, axis=-1, keepdims=True) + RMS_EPS)
        y = a * r * gain[:, sl]
        partner = acc_sw[:, sl] * r * gain_sw[:, sl]
        heads.append((y * cos + partner * sin) * scale)
    return jnp.concatenate(heads, axis=-1)


def _odd_in_kernel(x_ref, g_ref, wa_ref, wsw_ref, wb_ref, qg_ref, qgs_ref, kg_ref, kgs_ref, cos_ref, sin_ref,
                   o_ref, u_ref, *, n_q_tiles, kv_tile):
    j = pl.program_id(1)

    @pl.when(j == 0)
    def _():
        u_ref[...] = (_rms(x_ref[...]) * g_ref[...]).astype(BF16)

    is_q = j < n_q_tiles
    is_kv = j == kv_tile

    @pl.when(is_q)
    def _():
        u = u_ref[...]
        o_ref[...] = _rope_norm(_dot(u, wa_ref[...]), _dot(u, wsw_ref[...]), qg_ref[...], qgs_ref[...],
                                cos_ref[...], sin_ref[...], GQ_HEAD_DIM ** -0.5 * LOG2_E).astype(o_ref.dtype)

    @pl.when(is_kv)
    def _():
        kw = GQ_KV_HEADS * GQ_HEAD_DIM
        u = u_ref[...]
        acc = _dot(u, wa_ref[...])
        k = _rope_norm(acc[:, :kw], _dot(u, wsw_ref[:, :kw]), kg_ref[...], kgs_ref[...],
                       cos_ref[...], sin_ref[...], 1.0)
        o_ref[...] = jnp.concatenate([k, acc[:, kw:]], axis=-1).astype(o_ref.dtype)

    @pl.when(jnp.logical_not(jnp.logical_or(is_q, is_kv)))
    def _():
        o_ref[...] = _dot(u_ref[...], wb_ref[...]).astype(o_ref.dtype)


def _odd_in_proj(x, g, wa, wsw, wb, qg, qgs, kg, kgs, cos, sin, *, plain0, n_plain, tm, tn, seq):
    t, d = x.shape
    n_q_tiles = (GQ_HEADS * GQ_HEAD_DIM) // tn
    kv_tile = n_q_tiles + n_plain
    n = (kv_tile + 1) * tn
    kvw = GQ_KV_HEADS * GQ_HEAD_DIM
    sb = seq // tm
    kern = functools.partial(_odd_in_kernel, n_q_tiles=n_q_tiles, kv_tile=kv_tile)

    def qkv_map(i, j):
        return (0, jnp.where(j < n_q_tiles, j, jnp.where(j == kv_tile, n_q_tiles, n_q_tiles - 1)))

    return pl.pallas_call(
        kern,
        grid=(t // tm, kv_tile + 1),
        in_specs=[
            pl.BlockSpec((tm, d), lambda i, j: (i, 0)),
            pl.BlockSpec((1, d), lambda i, j: (0, 0)),
            pl.BlockSpec((d, tn), qkv_map),
            pl.BlockSpec((d, tn), qkv_map),
            pl.BlockSpec((d, tn), lambda i, j: (0, plain0 + jnp.clip(j - n_q_tiles, 0, n_plain - 1))),
            pl.BlockSpec((1, tn), lambda i, j: (0, 0)),
            pl.BlockSpec((1, tn), lambda i, j: (0, 0)),
            pl.BlockSpec((1, kvw), lambda i, j: (0, 0)),
            pl.BlockSpec((1, kvw), lambda i, j: (0, 0)),
            pl.BlockSpec((tm, GQ_HEAD_DIM), lambda i, j: (i % sb, 0)),
            pl.BlockSpec((tm, GQ_HEAD_DIM), lambda i, j: (i % sb, 0)),
        ],
        out_specs=pl.BlockSpec((tm, tn), lambda i, j: (i, j)),
        out_shape=jax.ShapeDtypeStruct((t, n), BF16),
        scratch_shapes=[pltpu.VMEM((tm, d), BF16)],
        compiler_params=_params("parallel", "arbitrary"),
        name="odd_in_proj",
    )(x, g, wa, wsw, wb, qg, qgs, kg, kgs, cos, sin)


def _split3(a):
    hi = a.astype(BF16)
    r = a - hi.astype(F32)
    mid = r.astype(BF16)
    return hi, mid, (r - mid.astype(F32)).astype(BF16)


def _dot_exact_lhs(a, b):
    return sum(_dot(p, b) for p in _split3(a))


def _dot_exact_rhs(a, b):
    return sum(_dot(a, p) for p in _split3(b))


def _mlstm_kernel(qf_ref, kf_ref, vf_ref, gf_ref, qb_ref, kb_ref, vb_ref, gb_ref, bias_ref, sel_ref,
                  hf_ref, hb_ref, c_ref, n_ref, m_ref):
    L = qf_ref.shape[0]
    dh = ML_HEAD_DIM
    k_scale = dh ** -0.5

    @pl.when(pl.program_id(1) == 0)
    def _():
        c_ref[...] = jnp.zeros_like(c_ref)
        n_ref[...] = jnp.zeros_like(n_ref)
        m_ref[...] = jnp.zeros_like(m_ref)

    row = lax.broadcasted_iota(jnp.int32, (L, L), 0)
    col = lax.broadcasted_iota(jnp.int32, (L, L), 1)
    lower = row >= col
    upper = row <= col
    lane = lax.broadcasted_iota(jnp.int32, (L, LANES), 1)
    is_forget = ((lane >= ML_HEADS) & (lane < 2 * ML_HEADS)) | ((lane >= 3 * ML_HEADS) & (lane < 4 * ML_HEADS))

    def wide(x):
        return jnp.concatenate([x] * (dh // LANES), axis=1)

    def gate_tables(g_ref, tri, first_col):
        g = g_ref[...] + bias_ref[...]
        lf = jnp.where(is_forget, _log_sigmoid(g), 0.0)
        cum = _dot_exact_rhs(tri.astype(BF16), lf)
        tab = jnp.where(is_forget, cum, g) * LOG2_E
        ncol = 2 * ML_HEADS * LANES
        rep = _dot_exact_lhs(tab, sel_ref[:, first_col * LANES:first_col * LANES + ncol])
        return rep, tab.T

    rep_f, tab_ft = gate_tables(gf_ref, lower, 0)
    rep_b, tab_bt = gate_tables(gb_ref, upper, 2 * ML_HEADS)

    def chain(idx, q, k, v, b128, ig128, brow, igrow, g128, valid):
        c = c_ref[idx]
        n = n_ref[idx][0:1, :]
        m_prev = m_ref[idx][0:1, :]
        d = jnp.where(valid, wide(b128) + (igrow - brow), MASK_VALUE)
        inter = b128 + m_prev
        m_t = jnp.maximum(inter, jnp.max(d, axis=-1, keepdims=True))
        w = jnp.exp2(d - wide(m_t)) * _dot_nt(q * jnp.asarray(k_scale, q.dtype), k)
        decay = jnp.exp2(inter - m_t)
        num = wide(decay) * _dot(q, c.astype(BF16)) + _dot(w.astype(BF16), v)
        qn = jnp.sum(q.astype(F32) * n, axis=-1, keepdims=True)
        den = decay * qn + jnp.sum(w, axis=-1, keepdims=True)
        h = num * wide(1.0 / jnp.maximum(jnp.abs(den), jnp.exp2(-m_t)))
        a = g128 - b128 + ig128
        m_new = jnp.maximum(g128 + m_prev, jnp.max(a, axis=0, keepdims=True))
        carry = wide(jnp.exp2(g128 + m_prev - m_new))
        wk = wide(jnp.exp2(a - m_new) * k_scale)
        wkv = (wk * v.astype(F32)).astype(BF16)
        c_ref[idx] = carry * c + _dot_tn(k, wkv)
        n_new = carry * n + jnp.sum(wk * k.astype(F32), axis=0, keepdims=True)
        n_ref[idx] = jnp.broadcast_to(n_new, n_ref.shape[1:])
        m_ref[idx] = jnp.broadcast_to(m_new, m_ref.shape[1:])
        return h

    def lanes(rep, j):
        return rep[:, j * LANES:(j + 1) * LANES]

    for hd in range(ML_HEADS):
        sl = slice(hd * dh, (hd + 1) * dh)
        fi, ff = hd, ML_HEADS + hd
        b128 = lanes(rep_f, ff)
        h_f = chain(hd, qf_ref[:, sl], kf_ref[:, sl], vf_ref[:, sl], b128, lanes(rep_f, fi),
                    tab_ft[ff:ff + 1, :], tab_ft[fi:fi + 1, :], b128[L - 1:L, :], lower)
        hf_ref[:, sl] = h_f.astype(hf_ref.dtype)
        b128 = lanes(rep_b, ff)
        go = 2 * ML_HEADS
        h_b = chain(ML_HEADS + hd, qb_ref[:, sl], kb_ref[:, sl], vb_ref[:, sl], b128, lanes(rep_b, fi),
                    tab_bt[go + ff:go + ff + 1, :], tab_bt[go + fi:go + fi + 1, :], b128[0:1, :], upper)
        hb_ref[:, sl] = h_b.astype(hb_ref.dtype)


def _mlstm(p, gates, bias, *, batch, seq):
    t = p.shape[0]
    L = ML_CHUNK
    nc = seq // L
    w = ML_HEADS * ML_HEAD_DIM
    n_gate = 4 * ML_HEADS
    sel = (jnp.arange(LANES)[:, None] == (jnp.arange(n_gate * LANES) // LANES)[None, :]).astype(BF16)

    def fwd(cb):
        return lambda b, c: (b * nc + c, cb)

    def bwd(cb):
        return lambda b, c: (b * nc + nc - 1 - c, cb)

    return pl.pallas_call(
        _mlstm_kernel,
        grid=(batch, nc),
        in_specs=[
            pl.BlockSpec((L, w), fwd(0)), pl.BlockSpec((L, w), fwd(1)), pl.BlockSpec((L, w), fwd(2)),
            pl.BlockSpec((L, LANES), fwd(0)),
            pl.BlockSpec((L, w), bwd(0)), pl.BlockSpec((L, w), bwd(1)), pl.BlockSpec((L, w), bwd(2)),
            pl.BlockSpec((L, LANES), bwd(0)),
            pl.BlockSpec((1, LANES), lambda b, c: (0, 0)),
            pl.BlockSpec(sel.shape, lambda b, c: (0, 0)),
        ],
        out_specs=[pl.BlockSpec((L, w), fwd(0)), pl.BlockSpec((L, w), bwd(0))],
        out_shape=[jax.ShapeDtypeStruct((t, w), BF16), jax.ShapeDtypeStruct((t, w), BF16)],
        scratch_shapes=[
            pltpu.VMEM((2 * ML_HEADS, ML_HEAD_DIM, ML_HEAD_DIM), F32),
            pltpu.VMEM((2 * ML_HEADS, SUBLANES, ML_HEAD_DIM), F32),
            pltpu.VMEM((2 * ML_HEADS, SUBLANES, LANES), F32),
        ],
        compiler_params=_params("parallel", "arbitrary"),
        name="mlstm",
    )(p, p, p, gates, p, p, p, gates, bias, sel)


def _na_table_kernel(rp_ref, o_ref):
    c = lax.broadcasted_iota(jnp.int32, (GRID_W, LANES), 0)
    lane = lax.broadcasted_iota(jnp.int32, (GRID_W, LANES), 1)
    c0 = jnp.clip(c - NA_KW // 2, 0, GRID_W - NA_KW)
    low_half = lane < GRID_W
    kc = jnp.where(low_half, lane, lane - GRID_W)
    in_window = (kc >= c0) & (kc < c0 + NA_KW)
    lo, hi = [], []
    for dr in range(2 * NA_KH - 1):
        row = jnp.broadcast_to(rp_ref[0, dr:dr + 1, :], (GRID_W, LANES))
        lo.append(jnp.where(in_window & low_half, pltpu.roll(row, GRID_W + 1, 1, stride=1, stride_axis=0),
                            MASK_VALUE))
        hi.append(jnp.where(in_window & jnp.logical_not(low_half),
                            pltpu.roll(row, 1, 1, stride=1, stride_axis=0), MASK_VALUE))
    for rel in range(NA_KH):
        pairs = [jnp.maximum(lo[2 * j - rel + NA_KH - 1], hi[2 * j - rel + NA_KH]) for j in range(NA_KH // 2)]
        o_ref[0, rel] = jnp.concatenate(pairs, axis=1)


def _na_bias_table(rpb):
    nh, ndr, ndc = rpb.shape
    lead = GRID_W - NA_KW
    rp = jnp.pad(rpb.astype(F32) * LOG2_E, ((0, 0), (0, 2 * NA_KH - ndr), (lead, LANES - lead - ndc)))
    return pl.pallas_call(
        _na_table_kernel,
        grid=(nh,),
        in_specs=[pl.BlockSpec((1, 2 * NA_KH, LANES), lambda h: (h, 0, 0))],
        out_specs=pl.BlockSpec((1, NA_KH, GRID_W, NA_KH * GRID_W), lambda h: (h, 0, 0, 0)),
        out_shape=jax.ShapeDtypeStruct((nh, NA_KH, GRID_W, NA_KH * GRID_W), F32),
        compiler_params=_params("parallel"),
        name="na_bias_table",
    )(rp)


def _na_kernel(q_ref, k_ref, v_ref, z_ref, tbl_ref, o_ref, *, rows):
    rb = pl.program_id(2)
    win = NA_KH * GRID_W
    lane_q = lax.broadcasted_iota(jnp.int32, (GRID_W, LANES), 1)
    head0_q = lane_q < NA_HEAD_DIM
    starts, scores = [], []
    for i in range(NA_ROWS_PER_STEP):
        r = rb * NA_ROWS_PER_STEP + i
        r0 = jnp.clip(r - NA_KH // 2, 0, rows - NA_KH)
        rel = r - r0
        start = pl.multiple_of(r0 * GRID_W, GRID_W)
        kw = k_ref[pl.ds(start, win), :]
        q = q_ref[i * GRID_W:(i + 1) * GRID_W, :]
        zero = jnp.zeros_like(q)
        qs = jnp.concatenate([jnp.where(head0_q, q, zero), jnp.where(head0_q, zero, q)], axis=0)
        bias = jnp.concatenate([tbl_ref[0, rel], tbl_ref[1, rel]], axis=0)
        scores.append(_dot_nt(qs, kw) * (NA_HEAD_DIM ** -0.5 * LOG2_E) + bias)
        starts.append(start)
    for i in range(NA_ROWS_PER_STEP):
        s = scores[i]
        e = jnp.exp2(s - jnp.max(s, axis=-1, keepdims=True))
        inv = 1.0 / jnp.sum(e, axis=-1, keepdims=True)
        o = _dot(e.astype(BF16), v_ref[pl.ds(starts[i], win), :]) * inv
        out = jnp.where(head0_q, o[:GRID_W], o[GRID_W:])
        zz = z_ref[i * GRID_W:(i + 1) * GRID_W, :].astype(F32)
        o_ref[i * GRID_W:(i + 1) * GRID_W, :] = (out * _silu(zz)).astype(o_ref.dtype)


def _natten(p, tbl, *, batch, seq, col0):
    t = p.shape[0]
    rows = seq // GRID_W
    width = NA_HEADS * NA_HEAD_DIM
    pairs = width // LANES
    tq = NA_ROWS_PER_STEP * GRID_W
    nrb = seq // tq
    cb = col0 // LANES
    kern = functools.partial(_na_kernel, rows=rows)
    return pl.pallas_call(
        kern,
        grid=(batch, pairs, nrb),
        in_specs=[
            pl.BlockSpec((tq, LANES), lambda b, h, r: (b * nrb + r, cb + h)),
            pl.BlockSpec((seq, LANES), lambda b, h, r: (b, cb + pairs + h)),
            pl.BlockSpec((seq, LANES), lambda b, h, r: (b, cb + 2 * pairs + h)),
            pl.BlockSpec((tq, LANES), lambda b, h, r: (b * nrb + r, cb + 3 * pairs + h)),
            pl.BlockSpec((2, NA_KH, GRID_W, NA_KH * GRID_W), lambda b, h, r: (h, 0, 0, 0)),
        ],
        out_specs=pl.BlockSpec((tq, LANES), lambda b, h, r: (b * nrb + r, h)),
        out_shape=jax.ShapeDtypeStruct((t, width), BF16),
        compiler_params=_params("parallel", "parallel", "arbitrary"),
        name="natten",
    )(p, p, p, p, tbl)


def _even_out_kernel(hf_ref, hb_ref, o_ref, z_ref, na_ref, g_ref, w_ref, x_ref, out_ref):
    wa = ML_HEADS * ML_HEAD_DIM
    y = _dot(na_ref[...], w_ref[wa:, :])
    for i in range(ML_HEADS):
        sl = slice(i * ML_HEAD_DIM, (i + 1) * ML_HEAD_DIM)
        h = hf_ref[:, sl].astype(F32) + hb_ref[:, sl].astype(F32)
        a = _rms(h) * g_ref[:, sl] * _sigmoid(o_ref[:, sl].astype(F32)) * _silu(z_ref[:, sl].astype(F32))
        y = y + _dot(a.astype(BF16), w_ref[sl, :])
    out_ref[...] = x_ref[...] + y


def _even_out(hf, hb, p, na, g, w, x, *, tm):
    t, d = x.shape
    wa = hf.shape[1]
    return pl.pallas_call(
        _even_out_kernel,
        grid=(t // tm,),
        in_specs=[
            pl.BlockSpec((tm, wa), lambda i: (i, 0)),
            pl.BlockSpec((tm, wa), lambda i: (i, 0)),
            pl.BlockSpec((tm, wa), lambda i: (i, 3)),
            pl.BlockSpec((tm, wa), lambda i: (i, 4)),
            pl.BlockSpec((tm, na.shape[1]), lambda i: (i, 0)),
            pl.BlockSpec((1, wa), lambda i: (0, 0)),
            pl.BlockSpec(w.shape, lambda i: (0, 0)),
            pl.BlockSpec((tm, d), lambda i: (i, 0)),
        ],
        out_specs=pl.BlockSpec((tm, d), lambda i: (i, 0)),
        out_shape=jax.ShapeDtypeStruct((t, d), F32),
        compiler_params=_params("parallel"),
        name="even_out_proj",
    )(hf, hb, p, p, na, g, w, x)


def _gqa_kernel(q_ref, k_ref, v_ref, o_ref, vaug_ref, m_ref, acc_ref, *, tk):
    tq = q_ref.shape[0]
    seq = k_ref.shape[0]
    dh = GQ_HEAD_DIM

    @pl.when(pl.program_id(2) == 0)
    def _():
        vaug_ref[:, :dh] = v_ref[...]
        vaug_ref[:, dh:] = jnp.ones((seq, dh), BF16)

    q = q_ref[...]
    qs = jnp.concatenate([q[:, g * dh:(g + 1) * dh] for g in range(GQ_GROUP)], axis=0)
    m_ref[...] = jnp.full_like(m_ref, -jnp.inf)
    acc_ref[...] = jnp.zeros_like(acc_ref)

    def scores(c):
        return _dot_nt(qs, k_ref[c * tk:(c + 1) * tk, :])

    def update(s, c):
        m_prev = m_ref[...]
        m_new = jnp.maximum(m_prev, jnp.max(s, axis=-1, keepdims=True))
        alpha = jnp.exp2(m_prev - m_new)
        p = jnp.exp2(s - jnp.concatenate([m_new] * (tk // LANES), axis=1))
        pv = _dot(p.astype(BF16), vaug_ref[c * tk:(c + 1) * tk, :])
        acc_ref[...] = jnp.concatenate([alpha, alpha], axis=1) * acc_ref[...] + pv
        m_ref[...] = m_new

    nk = seq // tk
    s_next = scores(0)
    for c in range(nk):
        s_cur = s_next
        if c + 1 < nk:
            s_next = scores(c + 1)
        update(s_cur, c)
    acc = acc_ref[...]
    out = acc[:, :dh] / acc[:, dh:]
    o_ref[...] = jnp.concatenate(
        [out[g * tq:(g + 1) * tq] for g in range(GQ_GROUP)], axis=-1).astype(o_ref.dtype)


def _gqa(p, *, batch, seq, k_col, v_col):
    t = p.shape[0]
    tq, tk = GQ_BLOCK_Q, GQ_BLOCK_K
    nq = seq // tq
    gw = GQ_GROUP * GQ_HEAD_DIM
    kb, vb = k_col // GQ_HEAD_DIM, v_col // GQ_HEAD_DIM
    kern = functools.partial(_gqa_kernel, tk=tk)
    return pl.pallas_call(
        kern,
        grid=(batch, GQ_KV_HEADS, nq),
        in_specs=[
            pl.BlockSpec((tq, gw), lambda b, h, i: (b * nq + i, h)),
            pl.BlockSpec((seq, GQ_HEAD_DIM), lambda b, h, i: (b, kb + h)),
            pl.BlockSpec((seq, GQ_HEAD_DIM), lambda b, h, i: (b, vb + h)),
        ],
        out_specs=pl.BlockSpec((tq, gw), lambda b, h, i: (b * nq + i, h)),
        out_shape=jax.ShapeDtypeStruct((t, GQ_HEADS * GQ_HEAD_DIM), BF16),
        scratch_shapes=[
            pltpu.VMEM((seq, 2 * GQ_HEAD_DIM), BF16),
            pltpu.VMEM((GQ_GROUP * tq, LANES), F32),
            pltpu.VMEM((GQ_GROUP * tq, 2 * GQ_HEAD_DIM), F32),
        ],
        compiler_params=_params("parallel", "parallel", "arbitrary"),
        name="gqa_attention",
    )(p, p, p)


def _odd_out_kernel(att_ref, gz_ref, cb_ref, cc_ref, cx_ref, cz_ref, ccp_ref, cxp_ref, ccn_ref, cxn_ref,
                    cw_ref, cbias_ref, w_ref, h_ref, fg_ref, out_ref, *, blocks_per_seq):
    tm, wa = att_ref.shape
    slab = 2 * LANES
    i = pl.program_id(0) % blocks_per_seq
    ridx = lax.broadcasted_iota(jnp.int32, (tm, slab), 0)
    hh = h_ref[...]
    for s in range(wa // slab):
        sl = slice(s * slab, (s + 1) * slab)
        att = att_ref[:, sl].astype(F32) * _silu(gz_ref[:, sl].astype(F32))
        hh = hh + _dot(att.astype(BF16), w_ref[sl, :])
    for s in range(wa // slab):
        sl = slice(s * slab, (s + 1) * slab)
        xc = cc_ref[:, sl].astype(F32) * cx_ref[:, sl].astype(F32)
        prev_row = ccp_ref[SUBLANES - 1:SUBLANES, sl].astype(F32) * cxp_ref[SUBLANES - 1:SUBLANES, sl].astype(F32)
        next_row = ccn_ref[0:1, sl].astype(F32) * cxn_ref[0:1, sl].astype(F32)
        prev_row = jnp.where(i == 0, 0.0, prev_row)
        next_row = jnp.where(i == blocks_per_seq - 1, 0.0, next_row)
        x_prev = jnp.where(ridx == 0, prev_row, pltpu.roll(xc, 1, 0))
        x_next = jnp.where(ridx == tm - 1, next_row, pltpu.roll(xc, tm - 1, 0))
        y = x_prev * cw_ref[0:1, sl] + xc * cw_ref[1:2, sl] + x_next * cw_ref[2:3, sl] + cbias_ref[:, sl]
        conv = cb_ref[:, sl].astype(F32) * y * _silu(cz_ref[:, sl].astype(F32))
        hh = hh + _dot(conv.astype(BF16), w_ref[wa + s * slab:wa + (s + 1) * slab, :])
    out_ref[...] = _rms(hh) * fg_ref[...]


def _odd_out(att, p, cw, cbias, w, h, fg, *, tm, seq, cols):
    t, d = h.shape
    bps = seq // tm
    sub = tm // SUBLANES
    nsub = t // SUBLANES
    z_c, b_c, c_c, x_c, cz_c = cols

    def blk(cb):
        return pl.BlockSpec((tm, d), lambda i: (i, cb))

    def halo_prev(cb):
        return pl.BlockSpec((SUBLANES, d), lambda i: (jnp.maximum(i * sub - 1, 0), cb))

    def halo_next(cb):
        return pl.BlockSpec((SUBLANES, d), lambda i: (jnp.minimum((i + 1) * sub, nsub - 1), cb))

    kern = functools.partial(_odd_out_kernel, blocks_per_seq=bps)
    return pl.pallas_call(
        kern,
        grid=(t // tm,),
        in_specs=[
            pl.BlockSpec((tm, d), lambda i: (i, 0)),
            blk(z_c), blk(b_c), blk(c_c), blk(x_c), blk(cz_c),
            halo_prev(c_c), halo_prev(x_c), halo_next(c_c), halo_next(x_c),
            pl.BlockSpec(cw.shape, lambda i: (0, 0)),
            pl.BlockSpec((1, d), lambda i: (0, 0)),
            pl.BlockSpec(w.shape, lambda i: (0, 0)),
            pl.BlockSpec((tm, d), lambda i: (i, 0)),
            pl.BlockSpec((1, d), lambda i: (0, 0)),
        ],
        out_specs=pl.BlockSpec((tm, d), lambda i: (i, 0)),
        out_shape=jax.ShapeDtypeStruct((t, d), F32),
        compiler_params=_params("parallel"),
        name="odd_out_proj",
    )(att, p, p, p, p, p, p, p, p, p, cw, cbias, w, h, fg)


def _rope_tables(seq):
    t = jnp.arange(seq)
    pos = jnp.stack([t // GRID_W, t % GRID_W], axis=-1).astype(F32)
    inv = ROPE_THETA ** (-jnp.arange(ROPE_HALF, dtype=F32) / ROPE_HALF)
    ang = pos[:, :, None] * inv
    cos, sin = jnp.cos(ang), jnp.sin(ang)
    cos_l = jnp.stack([cos, cos], axis=1).reshape(seq, GQ_HEAD_DIM)
    sin_l = jnp.stack([-sin, sin], axis=1).reshape(seq, GQ_HEAD_DIM)
    return cos_l, sin_l


def _half_swap(a):
    lead = a.shape[:-1]
    a = a.reshape(*lead, -1, 2, GQ_HEAD_DIM // 2)
    return a[..., ::-1, :].reshape(*lead, -1)


def _rope_dim_order(a):
    lead = a.shape[:-1]
    a = a.reshape(*lead, -1, 2, 2, ROPE_HALF)
    return jnp.swapaxes(a, -3, -2).reshape(*lead, -1)


def kernel(x, norm_g, final_g, ev_w_in, ev_gate_b, ev_w_out, ev_ml_norm_g, ev_na_rpb,
           od_w_in, od_w_out, od_q_norm_g, od_k_norm_g, od_conv_w, od_conv_b):
    batch, seq, d = x.shape
    assert d == D_MODEL and norm_g.shape[0] == 2
    assert seq % max(ML_CHUNK, GQ_BLOCK_K, GQ_BLOCK_Q, NA_ROWS_PER_STEP * GRID_W, 512) == 0
    t = batch * seq
    xf = x.reshape(t, d)
    tm_in = min(2048, seq)
    tm_out = 512

    n_gate = 4 * ML_HEADS
    g0 = 5 * D_MODEL
    w_in = ev_w_in[0].astype(BF16)
    w_gate = jnp.pad(w_in[:, g0:g0 + n_gate], ((0, 0), (0, LANES - n_gate)))
    tn_even = 1024
    p_ev, gates = _even_in_proj(xf, norm_g[0][None, :], w_in, w_in[:, g0 + n_gate:], w_gate,
                                na=g0 // tn_even, tm=tm_in, tn=tn_even)
    gate_bias = jnp.pad(ev_gate_b[0], (0, LANES - n_gate))[None, :]
    hf, hb = _mlstm(p_ev, gates, gate_bias, batch=batch, seq=seq)
    na = _natten(p_ev, _na_bias_table(ev_na_rpb[0]), batch=batch, seq=seq, col0=g0)
    h1 = _even_out(hf, hb, p_ev, na, ev_ml_norm_g[0][None, :], ev_w_out[0].astype(BF16), xf, tm=tm_out)

    w_in = od_w_in[0]
    qw, kvw = GQ_HEADS * GQ_HEAD_DIM, GQ_KV_HEADS * GQ_HEAD_DIM
    tn_odd = 512
    wq, wk = _rope_dim_order(w_in[:, :qw]), _rope_dim_order(w_in[:, qw:qw + kvw])
    w_qkv = jnp.concatenate([wq, wk, w_in[:, qw + kvw:qw + 2 * kvw]], axis=1).astype(BF16)
    w_swap = jnp.concatenate([_half_swap(wq), _half_swap(wk), jnp.zeros((d, tn_odd - kvw), w_in.dtype)],
                             axis=1).astype(BF16)
    cos, sin = _rope_tables(seq)
    qg = jnp.tile(_rope_dim_order(od_q_norm_g[0]), tn_odd // GQ_HEAD_DIM)[None, :]
    kg = jnp.tile(_rope_dim_order(od_k_norm_g[0]), GQ_KV_HEADS)[None, :]
    plain0 = (qw + 2 * kvw) // tn_odd
    p_od = _odd_in_proj(h1, norm_g[1][None, :], w_qkv, w_swap, w_in.astype(BF16), qg, _half_swap(qg),
                        kg, _half_swap(kg), cos, sin, plain0=plain0, n_plain=w_in.shape[1] // tn_odd - plain0,
                        tm=tm_in, tn=tn_odd, seq=seq)
    k_col = p_od.shape[1] - 2 * kvw
    att = _gqa(p_od, batch=batch, seq=seq, k_col=k_col, v_col=k_col + kvw)
    out = _odd_out(att, p_od, od_conv_w[0], od_conv_b[0][None, :], od_w_out[0].astype(BF16), h1,
                   final_g[None, :], tm=tm_out, seq=seq, cols=(1, 2, 3, 4, 5))
    return out.reshape(batch, seq, d)
```

```python
import functools

import jax
import jax.numpy as jnp
from jax import lax
from jax.experimental import pallas as pl
from jax.experimental.pallas import tpu as pltpu

F32 = jnp.float32
BF16 = jnp.bfloat16

D_MODEL = 1024
GRID_W = 64
RMS_EPS = 1e-6

ML_HEADS = 4
ML_HEAD_DIM = 256
ML_CHUNK = 256

NA_HEADS = 16
NA_HEAD_DIM = 64
NA_KH = 8
NA_KW = 16
NA_ROWS_PER_STEP = 32

GQ_HEADS = 8
GQ_HEAD_DIM = 128
GQ_KV_HEADS = 2
GQ_GROUP = GQ_HEADS // GQ_KV_HEADS
ROPE_THETA = 10000.0
ROPE_HALF = GQ_HEAD_DIM // 4
GQ_BLOCK_Q = 256
GQ_BLOCK_K = 512

CV_K = 3

LANES = 128
SUBLANES = 8
MASK_VALUE = -1e30
LOG2_E = 1.4426950408889634

VMEM_LIMIT = 56 * 1024 * 1024


def _params(*sem):
    return pltpu.CompilerParams(dimension_semantics=sem, vmem_limit_bytes=VMEM_LIMIT)


def _sigmoid(x):
    return 0.5 * jnp.tanh(0.5 * x) + 0.5


def _silu(x):
    return x * _sigmoid(x)


def _log_sigmoid(x):
    return jnp.minimum(x, 0.0) - jnp.log1p(jnp.exp(-jnp.abs(x)))


def _rms(x, eps=RMS_EPS):
    return x * lax.rsqrt(jnp.mean(x * x, axis=-1, keepdims=True) + eps)


def _dot(a, b):
    return jnp.dot(a, b, preferred_element_type=F32)


def _dot_nt(a, b):
    return lax.dot_general(a, b, (((1,), (1,)), ((), ())), preferred_element_type=F32)


def _dot_tn(a, b):
    return lax.dot_general(a, b, (((0,), (0,)), ((), ())), preferred_element_type=F32)


def _even_in_kernel(x_ref, g_ref, wa_ref, wb_ref, wg_ref, o_ref, og_ref, u_ref, *, n_a_tiles):
    j = pl.program_id(1)

    @pl.when(j == 0)
    def _():
        u = (_rms(x_ref[...]) * g_ref[...]).astype(BF16)
        u_ref[...] = u
        og_ref[...] = _dot(u, wg_ref[...])

    @pl.when(j < n_a_tiles)
    def _():
        o_ref[...] = _dot(u_ref[...], wa_ref[...]).astype(o_ref.dtype)

    @pl.when(j >= n_a_tiles)
    def _():
        o_ref[...] = _dot(u_ref[...], wb_ref[...]).astype(o_ref.dtype)


def _even_in_proj(x, g, wa, wb, wg, *, na, tm, tn):
    t, d = x.shape
    nb = wb.shape[1] // tn
    n = (na + nb) * tn
    kern = functools.partial(_even_in_kernel, n_a_tiles=na)
    return pl.pallas_call(
        kern,
        grid=(t // tm, na + nb),
        in_specs=[
            pl.BlockSpec((tm, d), lambda i, j: (i, 0)),
            pl.BlockSpec((1, d), lambda i, j: (0, 0)),
            pl.BlockSpec((d, tn), lambda i, j: (0, jnp.minimum(j, na - 1))),
            pl.BlockSpec((d, tn), lambda i, j: (0, jnp.maximum(j - na, 0))),
            pl.BlockSpec((d, LANES), lambda i, j: (0, 0)),
        ],
        out_specs=[
            pl.BlockSpec((tm, tn), lambda i, j: (i, j)),
            pl.BlockSpec((tm, LANES), lambda i, j: (i, 0)),
        ],
        out_shape=[
            jax.ShapeDtypeStruct((t, n), BF16),
            jax.ShapeDtypeStruct((t, LANES), F32),
        ],
        scratch_shapes=[pltpu.VMEM((tm, d), BF16)],
        compiler_params=_params("parallel", "arbitrary"),
        name="even_in_proj",
    )(x, g, wa, wb, wg)


def _rope_norm(acc, acc_sw, gain, gain_sw, cos, sin, scale):
    nh = acc.shape[1] // GQ_HEAD_DIM
    heads = []
    for h in range(nh):
        sl = slice(h * GQ_HEAD_DIM, (h + 1) * GQ_HEAD_DIM)
        a = acc[:, sl]
        r = lax.rsqrt(jnp.mean(a * a, axis=-1, keepdims=True) + RMS_EPS)
        y = a * r * gain[:, sl]
        partner = acc_sw[:, sl] * r * gain_sw[:, sl]
        heads.append((y * cos + partner * sin) * scale)
    return jnp.concatenate(heads, axis=-1)


def _odd_in_kernel(x_ref, g_ref, wa_ref, wsw_ref, wb_ref, qg_ref, qgs_ref, kg_ref, kgs_ref, cos_ref, sin_ref,
                   o_ref, u_ref, *, n_q_tiles, kv_tile):
    j = pl.program_id(1)

    @pl.when(j == 0)
    def _():
        u_ref[...] = (_rms(x_ref[...]) * g_ref[...]).astype(BF16)

    is_q = j < n_q_tiles
    is_kv = j == kv_tile

    @pl.when(is_q)
    def _():
        u = u_ref[...]
        o_ref[...] = _rope_norm(_dot(u, wa_ref[...]), _dot(u, wsw_ref[...]), qg_ref[...], qgs_ref[...],
                                cos_ref[...], sin_ref[...], GQ_HEAD_DIM ** -0.5 * LOG2_E).astype(o_ref.dtype)

    @pl.when(is_kv)
    def _():
        kw = GQ_KV_HEADS * GQ_HEAD_DIM
        u = u_ref[...]
        acc = _dot(u, wa_ref[...])
        k = _rope_norm(acc[:, :kw], _dot(u, wsw_ref[:, :kw]), kg_ref[...], kgs_ref[...],
                       cos_ref[...], sin_ref[...], 1.0)
        o_ref[...] = jnp.concatenate([k, acc[:, kw:]], axis=-1).astype(o_ref.dtype)

    @pl.when(jnp.logical_not(jnp.logical_or(is_q, is_kv)))
    def _():
        o_ref[...] = _dot(u_ref[...], wb_ref[...]).astype(o_ref.dtype)


def _odd_in_proj(x, g, wa, wsw, wb, qg, qgs, kg, kgs, cos, sin, *, plain0, n_plain, tm, tn, seq):
    t, d = x.shape
    n_q_tiles = (GQ_HEADS * GQ_HEAD_DIM) // tn
    kv_tile = n_q_tiles + n_plain
    n = (kv_tile + 1) * tn
    kvw = GQ_KV_HEADS * GQ_HEAD_DIM
    sb = seq // tm
    kern = functools.partial(_odd_in_kernel, n_q_tiles=n_q_tiles, kv_tile=kv_tile)

    def qkv_map(i, j):
        return (0, jnp.where(j < n_q_tiles, j, jnp.where(j == kv_tile, n_q_tiles, n_q_tiles - 1)))

    return pl.pallas_call(
        kern,
        grid=(t // tm, kv_tile + 1),
        in_specs=[
            pl.BlockSpec((tm, d), lambda i, j: (i, 0)),
            pl.BlockSpec((1, d), lambda i, j: (0, 0)),
            pl.BlockSpec((d, tn), qkv_map),
            pl.BlockSpec((d, tn), qkv_map),
            pl.BlockSpec((d, tn), lambda i, j: (0, plain0 + jnp.clip(j - n_q_tiles, 0, n_plain - 1))),
            pl.BlockSpec((1, tn), lambda i, j: (0, 0)),
            pl.BlockSpec((1, tn), lambda i, j: (0, 0)),
            pl.BlockSpec((1, kvw), lambda i, j: (0, 0)),
            pl.BlockSpec((1, kvw), lambda i, j: (0, 0)),
            pl.BlockSpec((tm, GQ_HEAD_DIM), lambda i, j: (i % sb, 0)),
            pl.BlockSpec((tm, GQ_HEAD_DIM), lambda i, j: (i % sb, 0)),
        ],
        out_specs=pl.BlockSpec((tm, tn), lambda i, j: (i, j)),
        out_shape=jax.ShapeDtypeStruct((t, n), BF16),
        scratch_shapes=[pltpu.VMEM((tm, d), BF16)],
        compiler_params=_params("parallel", "arbitrary"),
        name="odd_in_proj",
    )(x, g, wa, wsw, wb, qg, qgs, kg, kgs, cos, sin)


def _split3(a):
    hi = a.astype(BF16)
    r = a - hi.astype(F32)
    mid = r.astype(BF16)
    return hi, mid, (r - mid.astype(F32)).astype(BF16)


def _dot_exact_lhs(a, b):
    return sum(_dot(p, b) for p in _split3(a))


def _mlstm_kernel(qf_ref, kf_ref, vf_ref, gf_ref, qb_ref, kb_ref, vb_ref, gb_ref, bias_ref, sel_ref,
                  hf_ref, hb_ref, c_ref, n_ref, m_ref):
    L = qf_ref.shape[0]
    dh = ML_HEAD_DIM
    k_scale = dh ** -0.5

    @pl.when(pl.program_id(1) == 0)
    def _():
        c_ref[...] = jnp.zeros_like(c_ref)
        n_ref[...] = jnp.zeros_like(n_ref)
        m_ref[...] = jnp.zeros_like(m_ref)

    row = lax.broadcasted_iota(jnp.int32, (L, L), 0)
    col = lax.broadcasted_iota(jnp.int32, (L, L), 1)
    lower = row >= col
    upper = row <= col
    n_gate = 4 * ML_HEADS

    def wide(x):
        return jnp.concatenate([x] * (dh // LANES), axis=1)

    gate_row = lax.broadcasted_iota(jnp.int32, (n_gate, L), 0)
    forget_row = ((gate_row >= ML_HEADS) & (gate_row < 2 * ML_HEADS)) | (gate_row >= 3 * ML_HEADS)

    def gate_tables(g_ref, reach, first_col):
        gt = (g_ref[...] + bias_ref[...]).T[:n_gate, :]
        lf = jnp.where(forget_row, _log_sigmoid(gt), 0.0)
        cum = _dot_exact_lhs(lf, reach.astype(BF16))
        tab_t = jnp.where(forget_row, cum, gt) * LOG2_E
        sel = sel_ref[:, first_col * LANES:(first_col + 2 * ML_HEADS) * LANES]
        rep = sum(_dot_tn(p, sel) for p in _split3(tab_t))
        return rep, tab_t

    rep_f, tab_ft = gate_tables(gf_ref, upper, 0)
    rep_b, tab_bt = gate_tables(gb_ref, lower, 2 * ML_HEADS)

    def lanes(rep, j):
        return rep[:, j * LANES:(j + 1) * LANES]

    chains = []
    for hd in range(ML_HEADS):
        chains.append((hd, (qf_ref, kf_ref, vf_ref, hf_ref), rep_f, tab_ft, 0, L - 1, lower, hd))
        chains.append((ML_HEADS + hd, (qb_ref, kb_ref, vb_ref, hb_ref), rep_b, tab_bt, 2 * ML_HEADS, 0, upper, hd))

    stage1 = []
    for idx, (q_ref, k_ref, _, _), _, _, _, _, _, hd in chains:
        sl = slice(hd * dh, (hd + 1) * dh)
        q = q_ref[:, sl]
        qk = _dot_nt(q * jnp.asarray(k_scale, q.dtype), k_ref[:, sl])
        stage1.append((qk, _dot(q, c_ref[idx].astype(BF16))))

    for (idx, (q_ref, k_ref, v_ref, h_ref), rep, tab_t, go, g_row, valid, hd), (qk, qc) in zip(chains, stage1):
        sl = slice(hd * dh, (hd + 1) * dh)
        fi, ff = hd, ML_HEADS + hd
        q, k, v = q_ref[:, sl], k_ref[:, sl], v_ref[:, sl]
        b128, ig128 = lanes(rep, ff), lanes(rep, fi)
        brow, igrow = tab_t[go + ff:go + ff + 1, :], tab_t[go + fi:go + fi + 1, :]
        g128 = b128[g_row:g_row + 1, :]
        n = n_ref[idx][0:1, :]
        m_prev = m_ref[idx][0:1, :]
        d = jnp.where(valid, wide(b128) + (igrow - brow), MASK_VALUE)
        inter = b128 + m_prev
        m_t = jnp.maximum(inter, jnp.max(d, axis=-1, keepdims=True))
        w = jnp.exp2(d - wide(m_t)) * qk
        decay = jnp.exp2(inter - m_t)
        num = wide(decay) * qc + _dot(w.astype(BF16), v)
        qn = jnp.sum(q.astype(F32) * n, axis=-1, keepdims=True)
        den = decay * qn + jnp.sum(w, axis=-1, keepdims=True)
        h = num * wide(1.0 / jnp.maximum(jnp.abs(den), jnp.exp2(-m_t)))
        h_ref[:, sl] = h.astype(h_ref.dtype)
        a = g128 - b128 + ig128
        m_new = jnp.maximum(g128 + m_prev, jnp.max(a, axis=0, keepdims=True))
        carry = wide(jnp.exp2(g128 + m_prev - m_new))
        wk = wide(jnp.exp2(a - m_new) * k_scale)
        wkv = (wk * v.astype(F32)).astype(BF16)
        c_ref[idx] = carry * c_ref[idx] + _dot_tn(k, wkv)
        n_new = carry * n + jnp.sum(wk * k.astype(F32), axis=0, keepdims=True)
        n_ref[idx] = jnp.broadcast_to(n_new, n_ref.shape[1:])
        m_ref[idx] = jnp.broadcast_to(m_new, m_ref.shape[1:])


def _mlstm(p, gates, bias, *, batch, seq):
    t = p.shape[0]
    L = ML_CHUNK
    nc = seq // L
    w = ML_HEADS * ML_HEAD_DIM
    n_gate = 4 * ML_HEADS
    sel = (jnp.arange(n_gate)[:, None] == (jnp.arange(n_gate * LANES) // LANES)[None, :]).astype(BF16)

    def fwd(cb):
        return lambda b, c: (b * nc + c, cb)

    def bwd(cb):
        return lambda b, c: (b * nc + nc - 1 - c, cb)

    return pl.pallas_call(
        _mlstm_kernel,
        grid=(batch, nc),
        in_specs=[
            pl.BlockSpec((L, w), fwd(0)), pl.BlockSpec((L, w), fwd(1)), pl.BlockSpec((L, w), fwd(2)),
            pl.BlockSpec((L, LANES), fwd(0)),
            pl.BlockSpec((L, w), bwd(0)), pl.BlockSpec((L, w), bwd(1)), pl.BlockSpec((L, w), bwd(2)),
            pl.BlockSpec((L, LANES), bwd(0)),
            pl.BlockSpec((1, LANES), lambda b, c: (0, 0)),
            pl.BlockSpec(sel.shape, lambda b, c: (0, 0)),
        ],
        out_specs=[pl.BlockSpec((L, w), fwd(0)), pl.BlockSpec((L, w), bwd(0))],
        out_shape=[jax.ShapeDtypeStruct((t, w), BF16), jax.ShapeDtypeStruct((t, w), BF16)],
        scratch_shapes=[
            pltpu.VMEM((2 * ML_HEADS, ML_HEAD_DIM, ML_HEAD_DIM), F32),
            pltpu.VMEM((2 * ML_HEADS, SUBLANES, ML_HEAD_DIM), F32),
            pltpu.VMEM((2 * ML_HEADS, SUBLANES, LANES), F32),
        ],
        compiler_params=_params("parallel", "arbitrary"),
        name="mlstm",
    )(p, p, p, gates, p, p, p, gates, bias, sel)


def _na_table_kernel(rp_ref, o_ref):
    c = lax.broadcasted_iota(jnp.int32, (GRID_W, LANES), 0)
    lane = lax.broadcasted_iota(jnp.int32, (GRID_W, LANES), 1)
    c0 = jnp.clip(c - NA_KW // 2, 0, GRID_W - NA_KW)
    low_half = lane < GRID_W
    kc = jnp.where(low_half, lane, lane - GRID_W)
    in_window = (kc >= c0) & (kc < c0 + NA_KW)
    lo, hi = [], []
    for dr in range(2 * NA_KH - 1):
        row = jnp.broadcast_to(rp_ref[0, dr:dr + 1, :], (GRID_W, LANES))
        lo.append(jnp.where(in_window & low_half, pltpu.roll(row, GRID_W + 1, 1, stride=1, stride_axis=0),
                            MASK_VALUE))
        hi.append(jnp.where(in_window & jnp.logical_not(low_half),
                            pltpu.roll(row, 1, 1, stride=1, stride_axis=0), MASK_VALUE))
    for rel in range(NA_KH):
        pairs = [jnp.maximum(lo[2 * j - rel + NA_KH - 1], hi[2 * j - rel + NA_KH]) for j in range(NA_KH // 2)]
        o_ref[0, rel] = jnp.concatenate(pairs, axis=1)


def _na_bias_table(rpb):
    nh, ndr, ndc = rpb.shape
    lead = GRID_W - NA_KW
    rp = jnp.pad(rpb.astype(F32) * LOG2_E, ((0, 0), (0, 2 * NA_KH - ndr), (lead, LANES - lead - ndc)))
    return pl.pallas_call(
        _na_table_kernel,
        grid=(nh,),
        in_specs=[pl.BlockSpec((1, 2 * NA_KH, LANES), lambda h: (h, 0, 0))],
        out_specs=pl.BlockSpec((1, NA_KH, GRID_W, NA_KH * GRID_W), lambda h: (h, 0, 0, 0)),
        out_shape=jax.ShapeDtypeStruct((nh, NA_KH, GRID_W, NA_KH * GRID_W), F32),
        compiler_params=_params("parallel"),
        name="na_bias_table",
    )(rp)


def _na_kernel(q_ref, k_ref, v_ref, z_ref, tbl_ref, o_ref, *, rows):
    rb = pl.program_id(2)
    win = NA_KH * GRID_W
    lane_q = lax.broadcasted_iota(jnp.int32, (GRID_W, LANES), 1)
    head0_q = lane_q < NA_HEAD_DIM
    starts, scores = [], []
    for i in range(NA_ROWS_PER_STEP):
        r = rb * NA_ROWS_PER_STEP + i
        r0 = jnp.clip(r - NA_KH // 2, 0, rows - NA_KH)
        rel = r - r0
        start = pl.multiple_of(r0 * GRID_W, GRID_W)
        kw = k_ref[pl.ds(start, win), :]
        q = q_ref[i * GRID_W:(i + 1) * GRID_W, :]
        zero = jnp.zeros_like(q)
        qs = jnp.concatenate([jnp.where(head0_q, q, zero), jnp.where(head0_q, zero, q)], axis=0)
        bias = jnp.concatenate([tbl_ref[0, rel], tbl_ref[1, rel]], axis=0)
        scores.append(_dot_nt(qs, kw) * (NA_HEAD_DIM ** -0.5 * LOG2_E) + bias)
        starts.append(start)
    for i in range(NA_ROWS_PER_STEP):
        s = scores[i]
        e = jnp.exp2(s - jnp.max(s, axis=-1, keepdims=True))
        inv = 1.0 / jnp.sum(e, axis=-1, keepdims=True)
        o = _dot(e.astype(BF16), v_ref[pl.ds(starts[i], win), :]) * inv
        out = jnp.where(head0_q, o[:GRID_W], o[GRID_W:])
        zz = z_ref[i * GRID_W:(i + 1) * GRID_W, :].astype(F32)
        o_ref[i * GRID_W:(i + 1) * GRID_W, :] = (out * _silu(zz)).astype(o_ref.dtype)


def _natten(p, tbl, *, batch, seq, col0):
    t = p.shape[0]
    rows = seq // GRID_W
    width = NA_HEADS * NA_HEAD_DIM
    pairs = width // LANES
    tq = NA_ROWS_PER_STEP * GRID_W
    nrb = seq // tq
    cb = col0 // LANES
    kern = functools.partial(_na_kernel, rows=rows)
    return pl.pallas_call(
        kern,
        grid=(batch, pairs, nrb),
        in_specs=[
            pl.BlockSpec((tq, LANES), lambda b, h, r: (b * nrb + r, cb + h)),
            pl.BlockSpec((seq, LANES), lambda b, h, r: (b, cb + pairs + h)),
            pl.BlockSpec((seq, LANES), lambda b, h, r: (b, cb + 2 * pairs + h)),
            pl.BlockSpec((tq, LANES), lambda b, h, r: (b * nrb + r, cb + 3 * pairs + h)),
            pl.BlockSpec((2, NA_KH, GRID_W, NA_KH * GRID_W), lambda b, h, r: (h, 0, 0, 0)),
        ],
        out_specs=pl.BlockSpec((tq, LANES), lambda b, h, r: (b * nrb + r, h)),
        out_shape=jax.ShapeDtypeStruct((t, width), BF16),
        compiler_params=_params("parallel", "parallel", "arbitrary"),
        name="natten",
    )(p, p, p, p, tbl)


def _even_out_kernel(hf_ref, hb_ref, o_ref, z_ref, na_ref, g_ref, w_ref, x_ref, out_ref):
    wa = ML_HEADS * ML_HEAD_DIM
    y = _dot(na_ref[...], w_ref[wa:, :])
    for i in range(ML_HEADS):
        sl = slice(i * ML_HEAD_DIM, (i + 1) * ML_HEAD_DIM)
        h = hf_ref[:, sl].astype(F32) + hb_ref[:, sl].astype(F32)
        a = _rms(h) * g_ref[:, sl] * _sigmoid(o_ref[:, sl].astype(F32)) * _silu(z_ref[:, sl].astype(F32))
        y = y + _dot(a.astype(BF16), w_ref[sl, :])
    out_ref[...] = x_ref[...] + y


def _even_out(hf, hb, p, na, g, w, x, *, tm):
    t, d = x.shape
    wa = hf.shape[1]
    return pl.pallas_call(
        _even_out_kernel,
        grid=(t // tm,),
        in_specs=[
            pl.BlockSpec((tm, wa), lambda i: (i, 0)),
            pl.BlockSpec((tm, wa), lambda i: (i, 0)),
            pl.BlockSpec((tm, wa), lambda i: (i, 3)),
            pl.BlockSpec((tm, wa), lambda i: (i, 4)),
            pl.BlockSpec((tm, na.shape[1]), lambda i: (i, 0)),
            pl.BlockSpec((1, wa), lambda i: (0, 0)),
            pl.BlockSpec(w.shape, lambda i: (0, 0)),
            pl.BlockSpec((tm, d), lambda i: (i, 0)),
        ],
        out_specs=pl.BlockSpec((tm, d), lambda i: (i, 0)),
        out_shape=jax.ShapeDtypeStruct((t, d), F32),
        compiler_params=_params("parallel"),
        name="even_out_proj",
    )(hf, hb, p, p, na, g, w, x)


def _gqa_kernel(q_ref, k_ref, v_ref, o_ref, vaug_ref, m_ref, acc_ref, *, tk):
    tq = q_ref.shape[0]
    seq = k_ref.shape[0]
    dh = GQ_HEAD_DIM

    @pl.when(pl.program_id(2) == 0)
    def _():
        vaug_ref[:, :dh] = v_ref[...]
        vaug_ref[:, dh:] = jnp.ones((seq, dh), BF16)

    q = q_ref[...]
    qs = jnp.concatenate([q[:, g * dh:(g + 1) * dh] for g in range(GQ_GROUP)], axis=0)
    m_ref[...] = jnp.full_like(m_ref, -jnp.inf)
    acc_ref[...] = jnp.zeros_like(acc_ref)

    def scores(c):
        return _dot_nt(qs, k_ref[c * tk:(c + 1) * tk, :])

    def update(s, c):
        m_prev = m_ref[...]
        m_new = jnp.maximum(m_prev, jnp.max(s, axis=-1, keepdims=True))
        alpha = jnp.exp2(m_prev - m_new)
        p = jnp.exp2(s - jnp.concatenate([m_new] * (tk // LANES), axis=1))
        pv = _dot(p.astype(BF16), vaug_ref[c * tk:(c + 1) * tk, :])
        acc_ref[...] = jnp.concatenate([alpha, alpha], axis=1) * acc_ref[...] + pv
        m_ref[...] = m_new

    nk = seq // tk
    s_next = scores(0)
    for c in range(nk):
        s_cur = s_next
        if c + 1 < nk:
            s_next = scores(c + 1)
        update(s_cur, c)
    acc = acc_ref[...]
    out = acc[:, :dh] / acc[:, dh:]
    o_ref[...] = jnp.concatenate(
        [out[g * tq:(g + 1) * tq] for g in range(GQ_GROUP)], axis=-1).astype(o_ref.dtype)


def _gqa(p, *, batch, seq, k_col, v_col):
    t = p.shape[0]
    tq, tk = GQ_BLOCK_Q, GQ_BLOCK_K
    nq = seq // tq
    gw = GQ_GROUP * GQ_HEAD_DIM
    kb, vb = k_col // GQ_HEAD_DIM, v_col // GQ_HEAD_DIM
    kern = functools.partial(_gqa_kernel, tk=tk)
    return pl.pallas_call(
        kern,
        grid=(batch, GQ_KV_HEADS, nq),
        in_specs=[
            pl.BlockSpec((tq, gw), lambda b, h, i: (b * nq + i, h)),
            pl.BlockSpec((seq, GQ_HEAD_DIM), lambda b, h, i: (b, kb + h)),
            pl.BlockSpec((seq, GQ_HEAD_DIM), lambda b, h, i: (b, vb + h)),
        ],
        out_specs=pl.BlockSpec((tq, gw), lambda b, h, i: (b * nq + i, h)),
        out_shape=jax.ShapeDtypeStruct((t, GQ_HEADS * GQ_HEAD_DIM), BF16),
        scratch_shapes=[
            pltpu.VMEM((seq, 2 * GQ_HEAD_DIM), BF16),
            pltpu.VMEM((GQ_GROUP * tq, LANES), F32),
            pltpu.VMEM((GQ_GROUP * tq, 2 * GQ_HEAD_DIM), F32),
        ],
        compiler_params=_params("parallel", "parallel", "arbitrary"),
        name="gqa_attention",
    )(p, p, p)


def _odd_out_kernel(att_ref, gz_ref, cb_ref, cc_ref, cx_ref, cz_ref, ccp_ref, cxp_ref, ccn_ref, cxn_ref,
                    cw_ref, cbias_ref, w_ref, h_ref, fg_ref, out_ref, *, blocks_per_seq):
    tm, wa = att_ref.shape
    slab = 2 * LANES
    i = pl.program_id(0) % blocks_per_seq
    ridx = lax.broadcasted_iota(jnp.int32, (tm, slab), 0)
    hh = h_ref[...]
    for s in range(wa // slab):
        sl = slice(s * slab, (s + 1) * slab)
        att = att_ref[:, sl].astype(F32) * _silu(gz_ref[:, sl].astype(F32))
        hh = hh + _dot(att.astype(BF16), w_ref[sl, :])
    for s in range(wa // slab):
        sl = slice(s * slab, (s + 1) * slab)
        xc = cc_ref[:, sl].astype(F32) * cx_ref[:, sl].astype(F32)
        prev_row = ccp_ref[SUBLANES - 1:SUBLANES, sl].astype(F32) * cxp_ref[SUBLANES - 1:SUBLANES, sl].astype(F32)
        next_row = ccn_ref[0:1, sl].astype(F32) * cxn_ref[0:1, sl].astype(F32)
        prev_row = jnp.where(i == 0, 0.0, prev_row)
        next_row = jnp.where(i == blocks_per_seq - 1, 0.0, next_row)
        x_prev = jnp.where(ridx == 0, prev_row, pltpu.roll(xc, 1, 0))
        x_next = jnp.where(ridx == tm - 1, next_row, pltpu.roll(xc, tm - 1, 0))
        y = x_prev * cw_ref[0:1, sl] + xc * cw_ref[1:2, sl] + x_next * cw_ref[2:3, sl] + cbias_ref[:, sl]
        conv = cb_ref[:, sl].astype(F32) * y * _silu(cz_ref[:, sl].astype(F32))
        hh = hh + _dot(conv.astype(BF16), w_ref[wa + s * slab:wa + (s + 1) * slab, :])
    out_ref[...] = _rms(hh) * fg_ref[...]


def _odd_out(att, p, cw, cbias, w, h, fg, *, tm, seq, cols):
    t, d = h.shape
    bps = seq // tm
    sub = tm // SUBLANES
    nsub = t // SUBLANES
    z_c, b_c, c_c, x_c, cz_c = cols

    def blk(cb):
        return pl.BlockSpec((tm, d), lambda i: (i, cb))

    def halo_prev(cb):
        return pl.BlockSpec((SUBLANES, d), lambda i: (jnp.maximum(i * sub - 1, 0), cb))

    def halo_next(cb):
        return pl.BlockSpec((SUBLANES, d), lambda i: (jnp.minimum((i + 1) * sub, nsub - 1), cb))

    kern = functools.partial(_odd_out_kernel, blocks_per_seq=bps)
    return pl.pallas_call(
        kern,
        grid=(t // tm,),
        in_specs=[
            pl.BlockSpec((tm, d), lambda i: (i, 0)),
            blk(z_c), blk(b_c), blk(c_c), blk(x_c), blk(cz_c),
            halo_prev(c_c), halo_prev(x_c), halo_next(c_c), halo_next(x_c),
            pl.BlockSpec(cw.shape, lambda i: (0, 0)),
            pl.BlockSpec((1, d), lambda i: (0, 0)),
            pl.BlockSpec(w.shape, lambda i: (0, 0)),
            pl.BlockSpec((tm, d), lambda i: (i, 0)),
            pl.BlockSpec((1, d), lambda i: (0, 0)),
        ],
        out_specs=pl.BlockSpec((tm, d), lambda i: (i, 0)),
        out_shape=jax.ShapeDtypeStruct((t, d), F32),
        compiler_params=_params("parallel"),
        name="odd_out_proj",
    )(att, p, p, p, p, p, p, p, p, p, cw, cbias, w, h, fg)


def _rope_tables(seq):
    t = jnp.arange(seq)
    pos = jnp.stack([t // GRID_W, t % GRID_W], axis=-1).astype(F32)
    inv = ROPE_THETA ** (-jnp.arange(ROPE_HALF, dtype=F32) / ROPE_HALF)
    ang = pos[:, :, None] * inv
    cos, sin = jnp.cos(ang), jnp.sin(ang)
    cos_l = jnp.stack([cos, cos], axis=1).reshape(seq, GQ_HEAD_DIM)
    sin_l = jnp.stack([-sin, sin], axis=1).reshape(seq, GQ_HEAD_DIM)
    return cos_l, sin_l


def _half_swap(a):
    lead = a.shape[:-1]
    a = a.reshape(*lead, -1, 2, GQ_HEAD_DIM // 2)
    return a[..., ::-1, :].reshape(*lead, -1)


def _rope_dim_order(a):
    lead = a.shape[:-1]
    a = a.reshape(*lead, -1, 2, 2, ROPE_HALF)
    return jnp.swapaxes(a, -3, -2).reshape(*lead, -1)


def kernel(x, norm_g, final_g, ev_w_in, ev_gate_b, ev_w_out, ev_ml_norm_g, ev_na_rpb,
           od_w_in, od_w_out, od_q_norm_g, od_k_norm_g, od_conv_w, od_conv_b):
    batch, seq, d = x.shape
    assert d == D_MODEL and norm_g.shape[0] == 2
    assert seq % max(ML_CHUNK, GQ_BLOCK_K, GQ_BLOCK_Q, NA_ROWS_PER_STEP * GRID_W, 512) == 0
    t = batch * seq
    xf = x.reshape(t, d)
    tm_in = min(2048, seq)
    tm_out = 512

    n_gate = 4 * ML_HEADS
    g0 = 5 * D_MODEL
    w_in = ev_w_in[0].astype(BF16)
    w_gate = jnp.pad(w_in[:, g0:g0 + n_gate], ((0, 0), (0, LANES - n_gate)))
    tn_even = 1024
    p_ev, gates = _even_in_proj(xf, norm_g[0][None, :], w_in, w_in[:, g0 + n_gate:], w_gate,
                                na=g0 // tn_even, tm=tm_in, tn=tn_even)
    gate_bias = jnp.pad(ev_gate_b[0], (0, LANES - n_gate))[None, :]
    hf, hb = _mlstm(p_ev, gates, gate_bias, batch=batch, seq=seq)
    na = _natten(p_ev, _na_bias_table(ev_na_rpb[0]), batch=batch, seq=seq, col0=g0)
    h1 = _even_out(hf, hb, p_ev, na, ev_ml_norm_g[0][None, :], ev_w_out[0].astype(BF16), xf, tm=tm_out)

    w_in = od_w_in[0]
    qw, kvw = GQ_HEADS * GQ_HEAD_DIM, GQ_KV_HEADS * GQ_HEAD_DIM
    tn_odd = 512
    wq, wk = _rope_dim_order(w_in[:, :qw]), _rope_dim_order(w_in[:, qw:qw + kvw])
    w_qkv = jnp.concatenate([wq, wk, w_in[:, qw + kvw:qw + 2 * kvw]], axis=1).astype(BF16)
    w_swap = jnp.concatenate([_half_swap(wq), _half_swap(wk), jnp.zeros((d, tn_odd - kvw), w_in.dtype)],
                             axis=1).astype(BF16)
    cos, sin = _rope_tables(seq)
    qg = jnp.tile(_rope_dim_order(od_q_norm_g[0]), tn_odd // GQ_HEAD_DIM)[None, :]
    kg = jnp.tile(_rope_dim_order(od_k_norm_g[0]), GQ_KV_HEADS)[None, :]
    plain0 = (qw + 2 * kvw) // tn_odd
    p_od = _odd_in_proj(h1, norm_g[1][None, :], w_qkv, w_swap, w_in.astype(BF16), qg, _half_swap(qg),
                        kg, _half_swap(kg), cos, sin, plain0=plain0, n_plain=w_in.shape[1] // tn_odd - plain0,
                        tm=tm_in, tn=tn_odd, seq=seq)
    k_col = p_od.shape[1] - 2 * kvw
    att = _gqa(p_od, batch=batch, seq=seq, k_col=k_col, v_col=k_col + kvw)
    out = _odd_out(att, p_od, od_conv_w[0], od_conv_b[0][None, :], od_w_out[0].astype(BF16), h1,
                   final_g[None, :], tm=tm_out, seq=seq, cols=(1, 2, 3, 4, 5))
    return out.reshape(batch, seq, d)
```

```python
import functools

import jax
import jax.numpy as jnp
from jax import lax
from jax.experimental import pallas as pl
from jax.experimental.pallas import tpu as pltpu

F32 = jnp.float32
BF16 = jnp.bfloat16

D_MODEL = 1024
GRID_W = 64
RMS_EPS = 1e-6

ML_HEADS = 4
ML_HEAD_DIM = 256
ML_CHUNK = 256

NA_HEADS = 16
NA_HEAD_DIM = 64
NA_KH = 8
NA_KW = 16
NA_ROWS_PER_STEP = 32

GQ_HEADS = 8
GQ_HEAD_DIM = 128
GQ_KV_HEADS = 2
GQ_GROUP = GQ_HEADS // GQ_KV_HEADS
ROPE_THETA = 10000.0
ROPE_HALF = GQ_HEAD_DIM // 4
GQ_BLOCK_Q = 256
GQ_BLOCK_K = 512

CV_K = 3

LANES = 128
SUBLANES = 8
MASK_VALUE = -1e30
LOG2_E = 1.4426950408889634

VMEM_LIMIT = 56 * 1024 * 1024


def _params(*sem):
    return pltpu.CompilerParams(dimension_semantics=sem, vmem_limit_bytes=VMEM_LIMIT)


def _sigmoid(x):
    return 0.5 * jnp.tanh(0.5 * x) + 0.5


def _silu(x):
    return x * _sigmoid(x)


def _log_sigmoid(x):
    return jnp.minimum(x, 0.0) - jnp.log1p(jnp.exp(-jnp.abs(x)))


def _rms(x, eps=RMS_EPS):
    return x * lax.rsqrt(jnp.mean(x * x, axis=-1, keepdims=True) + eps)


def _dot(a, b):
    return jnp.dot(a, b, preferred_element_type=F32)


def _dot_nt(a, b):
    return lax.dot_general(a, b, (((1,), (1,)), ((), ())), preferred_element_type=F32)


def _dot_tn(a, b):
    return lax.dot_general(a, b, (((0,), (0,)), ((), ())), preferred_element_type=F32)


def _even_in_kernel(x_ref, g_ref, wa_ref, wb_ref, wg_ref, o_ref, og_ref, u_ref, *, n_a_tiles):
    j = pl.program_id(1)

    @pl.when(j == 0)
    def _():
        u = (_rms(x_ref[...]) * g_ref[...]).astype(BF16)
        u_ref[...] = u
        og_ref[...] = _dot(u, wg_ref[...])

    @pl.when(j < n_a_tiles)
    def _():
        o_ref[...] = _dot(u_ref[...], wa_ref[...]).astype(o_ref.dtype)

    @pl.when(j >= n_a_tiles)
    def _():
        o_ref[...] = _dot(u_ref[...], wb_ref[...]).astype(o_ref.dtype)


def _even_in_proj(x, g, wa, wb, wg, *, na, tm, tn):
    t, d = x.shape
    nb = wb.shape[1] // tn
    n = (na + nb) * tn
    kern = functools.partial(_even_in_kernel, n_a_tiles=na)
    return pl.pallas_call(
        kern,
        grid=(t // tm, na + nb),
        in_specs=[
            pl.BlockSpec((tm, d), lambda i, j: (i, 0)),
            pl.BlockSpec((1, d), lambda i, j: (0, 0)),
            pl.BlockSpec((d, tn), lambda i, j: (0, jnp.minimum(j, na - 1))),
            pl.BlockSpec((d, tn), lambda i, j: (0, jnp.maximum(j - na, 0))),
            pl.BlockSpec((d, LANES), lambda i, j: (0, 0)),
        ],
        out_specs=[
            pl.BlockSpec((tm, tn), lambda i, j: (i, j)),
            pl.BlockSpec((tm, LANES), lambda i, j: (i, 0)),
        ],
        out_shape=[
            jax.ShapeDtypeStruct((t, n), BF16),
            jax.ShapeDtypeStruct((t, LANES), F32),
        ],
        scratch_shapes=[pltpu.VMEM((tm, d), BF16)],
        compiler_params=_params("parallel", "arbitrary"),
        name="even_in_proj",
    )(x, g, wa, wb, wg)


def _even_w_prep_kernel(w_ref, wa_ref, wb_ref, wg_ref, *, g0, n_gate):
    w = w_ref[...]
    wa_ref[...] = w[:, :g0].astype(BF16)
    wb_ref[...] = w[:, g0 + n_gate:].astype(BF16)
    gates = jnp.concatenate([w[:, g0:g0 + n_gate], jnp.zeros((w.shape[0], LANES - n_gate), w.dtype)], axis=1)
    wg_ref[...] = gates.astype(BF16)


def _even_w_prep(w, *, g0, n_gate, rows):
    d, n = w.shape
    nb = n - g0 - n_gate
    kern = functools.partial(_even_w_prep_kernel, g0=g0, n_gate=n_gate)
    return pl.pallas_call(
        kern,
        grid=(d // rows,),
        in_specs=[pl.BlockSpec((rows, n), lambda i: (i, 0))],
        out_specs=[pl.BlockSpec((rows, g0), lambda i: (i, 0)), pl.BlockSpec((rows, nb), lambda i: (i, 0)),
                   pl.BlockSpec((rows, LANES), lambda i: (i, 0))],
        out_shape=[jax.ShapeDtypeStruct((d, g0), BF16), jax.ShapeDtypeStruct((d, nb), BF16),
                   jax.ShapeDtypeStruct((d, LANES), BF16)],
        compiler_params=_params("parallel"),
        name="even_w_prep",
    )(w)


def _rope_norm(acc, acc_sw, gain, gain_sw, cos, sin, scale):
    nh = acc.shape[1] // GQ_HEAD_DIM
    heads = []
    for h in range(nh):
        sl = slice(h * GQ_HEAD_DIM, (h + 1) * GQ_HEAD_DIM)
        a = acc[:, sl]
        r = lax.rsqrt(jnp.mean(a * a, axis=-1, keepdims=True) + RMS_EPS)
        y = a * r * gain[:, sl]
        partner = acc_sw[:, sl] * r * gain_sw[:, sl]
        heads.append((y * cos + partner * sin) * scale)
    return jnp.concatenate(heads, axis=-1)


def _odd_in_kernel(x_ref, g_ref, wa_ref, wsw_ref, wb_ref, qg_ref, qgs_ref, kg_ref, kgs_ref, cos_ref, sin_ref,
                   o_ref, u_ref, *, n_q_tiles, kv_tile):
    j = pl.program_id(1)

    @pl.when(j == 0)
    def _():
        u_ref[...] = (_rms(x_ref[...]) * g_ref[...]).astype(BF16)

    is_q = j < n_q_tiles
    is_kv = j == kv_tile

    @pl.when(is_q)
    def _():
        u = u_ref[...]
        o_ref[...] = _rope_norm(_dot(u, wa_ref[...]), _dot(u, wsw_ref[...]), qg_ref[...], qgs_ref[...],
                                cos_ref[...], sin_ref[...], GQ_HEAD_DIM ** -0.5 * LOG2_E).astype(o_ref.dtype)

    @pl.when(is_kv)
    def _():
        kw = GQ_KV_HEADS * GQ_HEAD_DIM
        u = u_ref[...]
        acc = _dot(u, wa_ref[...])
        k = _rope_norm(acc[:, :kw], _dot(u, wsw_ref[:, :kw]), kg_ref[...], kgs_ref[...],
                       cos_ref[...], sin_ref[...], 1.0)
        o_ref[...] = jnp.concatenate([k, acc[:, kw:]], axis=-1).astype(o_ref.dtype)

    @pl.when(jnp.logical_not(jnp.logical_or(is_q, is_kv)))
    def _():
        o_ref[...] = _dot(u_ref[...], wb_ref[...]).astype(o_ref.dtype)


def _odd_in_proj(x, g, wa, wsw, wb, qg, qgs, kg, kgs, cos, sin, *, plain0, n_plain, tm, tn, seq):
    t, d = x.shape
    n_q_tiles = (GQ_HEADS * GQ_HEAD_DIM) // tn
    kv_tile = n_q_tiles + n_plain
    n = (kv_tile + 1) * tn
    kvw = GQ_KV_HEADS * GQ_HEAD_DIM
    sb = seq // tm
    kern = functools.partial(_odd_in_kernel, n_q_tiles=n_q_tiles, kv_tile=kv_tile)

    def qkv_map(i, j):
        return (0, jnp.where(j < n_q_tiles, j, jnp.where(j == kv_tile, n_q_tiles, n_q_tiles - 1)))

    return pl.pallas_call(
        kern,
        grid=(t // tm, kv_tile + 1),
        in_specs=[
            pl.BlockSpec((tm, d), lambda i, j: (i, 0)),
            pl.BlockSpec((1, d), lambda i, j: (0, 0)),
            pl.BlockSpec((d, tn), qkv_map),
            pl.BlockSpec((d, tn), qkv_map),
            pl.BlockSpec((d, tn), lambda i, j: (0, plain0 + jnp.clip(j - n_q_tiles, 0, n_plain - 1))),
            pl.BlockSpec((1, tn), lambda i, j: (0, 0)),
            pl.BlockSpec((1, tn), lambda i, j: (0, 0)),
            pl.BlockSpec((1, kvw), lambda i, j: (0, 0)),
            pl.BlockSpec((1, kvw), lambda i, j: (0, 0)),
            pl.BlockSpec((tm, GQ_HEAD_DIM), lambda i, j: (i % sb, 0)),
            pl.BlockSpec((tm, GQ_HEAD_DIM), lambda i, j: (i % sb, 0)),
        ],
        out_specs=pl.BlockSpec((tm, tn), lambda i, j: (i, j)),
        out_shape=jax.ShapeDtypeStruct((t, n), BF16),
        scratch_shapes=[pltpu.VMEM((tm, d), BF16)],
        compiler_params=_params("parallel", "arbitrary"),
        name="odd_in_proj",
    )(x, g, wa, wsw, wb, qg, qgs, kg, kgs, cos, sin)


def _split3(a):
    hi = a.astype(BF16)
    r = a - hi.astype(F32)
    mid = r.astype(BF16)
    return hi, mid, (r - mid.astype(F32)).astype(BF16)


def _dot_exact_lhs(a, b):
    return sum(_dot(p, b) for p in _split3(a))


def _mlstm_kernel(qf_ref, kf_ref, vf_ref, gf_ref, qb_ref, kb_ref, vb_ref, gb_ref, bias_ref, sel_ref,
                  hf_ref, hb_ref, c_ref, n_ref, m_ref):
    L = qf_ref.shape[0]
    dh = ML_HEAD_DIM
    k_scale = dh ** -0.5

    @pl.when(pl.program_id(1) == 0)
    def _():
        c_ref[...] = jnp.zeros_like(c_ref)
        n_ref[...] = jnp.zeros_like(n_ref)
        m_ref[...] = jnp.zeros_like(m_ref)

    row = lax.broadcasted_iota(jnp.int32, (L, L), 0)
    col = lax.broadcasted_iota(jnp.int32, (L, L), 1)
    lower = row >= col
    upper = row <= col
    n_gate = 4 * ML_HEADS

    def wide(x):
        return jnp.concatenate([x] * (dh // LANES), axis=1)

    gate_row = lax.broadcasted_iota(jnp.int32, (n_gate, L), 0)
    forget_row = ((gate_row >= ML_HEADS) & (gate_row < 2 * ML_HEADS)) | (gate_row >= 3 * ML_HEADS)

    def gate_tables(g_ref, reach, first_col):
        gt = (g_ref[...] + bias_ref[...]).T[:n_gate, :]
        lf = jnp.where(forget_row, _log_sigmoid(gt), 0.0)
        cum = _dot_exact_lhs(lf, reach.astype(BF16))
        tab_t = jnp.where(forget_row, cum, gt) * LOG2_E
        sel = sel_ref[:, first_col * LANES:(first_col + 2 * ML_HEADS) * LANES]
        rep = sum(_dot_tn(p, sel) for p in _split3(tab_t))
        return rep, tab_t

    rep_f, tab_ft = gate_tables(gf_ref, upper, 0)
    rep_b, tab_bt = gate_tables(gb_ref, lower, 2 * ML_HEADS)

    def lanes(rep, j):
        return rep[:, j * LANES:(j + 1) * LANES]

    chains = []
    for hd in range(ML_HEADS):
        chains.append((hd, (qf_ref, kf_ref, vf_ref, hf_ref), rep_f, tab_ft, 0, L - 1, lower, hd))
        chains.append((ML_HEADS + hd, (qb_ref, kb_ref, vb_ref, hb_ref), rep_b, tab_bt, 2 * ML_HEADS, 0, upper, hd))

    stage1 = []
    for idx, (q_ref, k_ref, _, _), _, _, _, _, _, hd in chains:
        sl = slice(hd * dh, (hd + 1) * dh)
        q = q_ref[:, sl]
        qk = _dot_nt(q * jnp.asarray(k_scale, q.dtype), k_ref[:, sl])
        stage1.append((qk, _dot(q, c_ref[idx].astype(BF16))))

    for (idx, (q_ref, k_ref, v_ref, h_ref), rep, tab_t, go, g_row, valid, hd), (qk, qc) in zip(chains, stage1):
        sl = slice(hd * dh, (hd + 1) * dh)
        fi, ff = hd, ML_HEADS + hd
        q, k, v = q_ref[:, sl], k_ref[:, sl], v_ref[:, sl]
        b128, ig128 = lanes(rep, ff), lanes(rep, fi)
        brow, igrow = tab_t[go + ff:go + ff + 1, :], tab_t[go + fi:go + fi + 1, :]
        g128 = b128[g_row:g_row + 1, :]
        n = n_ref[idx][0:1, :]
        m_prev = m_ref[idx][0:1, :]
        d = jnp.where(valid, wide(b128) + (igrow - brow), MASK_VALUE)
        inter = b128 + m_prev
        m_t = jnp.maximum(inter, jnp.max(d, axis=-1, keepdims=True))
        w = jnp.exp2(d - wide(m_t)) * qk
        decay = jnp.exp2(inter - m_t)
        num = wide(decay) * qc + _dot(w.astype(BF16), v)
        qn = jnp.sum(q.astype(F32) * n, axis=-1, keepdims=True)
        den = decay * qn + jnp.sum(w, axis=-1, keepdims=True)
        h = num * wide(1.0 / jnp.maximum(jnp.abs(den), jnp.exp2(-m_t)))
        h_ref[:, sl] = h.astype(h_ref.dtype)
        a = g128 - b128 + ig128
        m_new = jnp.maximum(g128 + m_prev, jnp.max(a, axis=0, keepdims=True))
        carry = wide(jnp.exp2(g128 + m_prev - m_new))
        wk = wide(jnp.exp2(a - m_new) * k_scale)
        wkv = (wk * v.astype(F32)).astype(BF16)
        c_ref[idx] = carry * c_ref[idx] + _dot_tn(k, wkv)
        n_new = carry * n + jnp.sum(wk * k.astype(F32), axis=0, keepdims=True)
        n_ref[idx] = jnp.broadcast_to(n_new, n_ref.shape[1:])
        m_ref[idx] = jnp.broadcast_to(m_new, m_ref.shape[1:])


def _mlstm(p, gates, bias, *, batch, seq):
    t = p.shape[0]
    L = ML_CHUNK
    nc = seq // L
    w = ML_HEADS * ML_HEAD_DIM
    n_gate = 4 * ML_HEADS
    sel = (jnp.arange(n_gate)[:, None] == (jnp.arange(n_gate * LANES) // LANES)[None, :]).astype(BF16)

    def fwd(cb):
        return lambda b, c: (b * nc + c, cb)

    def bwd(cb):
        return lambda b, c: (b * nc + nc - 1 - c, cb)

    return pl.pallas_call(
        _mlstm_kernel,
        grid=(batch, nc),
        in_specs=[
            pl.BlockSpec((L, w), fwd(0)), pl.BlockSpec((L, w), fwd(1)), pl.BlockSpec((L, w), fwd(2)),
            pl.BlockSpec((L, LANES), fwd(0)),
            pl.BlockSpec((L, w), bwd(0)), pl.BlockSpec((L, w), bwd(1)), pl.BlockSpec((L, w), bwd(2)),
            pl.BlockSpec((L, LANES), bwd(0)),
            pl.BlockSpec((1, LANES), lambda b, c: (0, 0)),
            pl.BlockSpec(sel.shape, lambda b, c: (0, 0)),
        ],
        out_specs=[pl.BlockSpec((L, w), fwd(0)), pl.BlockSpec((L, w), bwd(0))],
        out_shape=[jax.ShapeDtypeStruct((t, w), BF16), jax.ShapeDtypeStruct((t, w), BF16)],
        scratch_shapes=[
            pltpu.VMEM((2 * ML_HEADS, ML_HEAD_DIM, ML_HEAD_DIM), F32),
            pltpu.VMEM((2 * ML_HEADS, SUBLANES, ML_HEAD_DIM), F32),
            pltpu.VMEM((2 * ML_HEADS, SUBLANES, LANES), F32),
        ],
        compiler_params=_params("parallel", "arbitrary"),
        name="mlstm",
    )(p, p, p, gates, p, p, p, gates, bias, sel)


def _na_table_kernel(rp_ref, o_ref):
    c = lax.broadcasted_iota(jnp.int32, (GRID_W, LANES), 0)
    lane = lax.broadcasted_iota(jnp.int32, (GRID_W, LANES), 1)
    c0 = jnp.clip(c - NA_KW // 2, 0, GRID_W - NA_KW)
    low_half = lane < GRID_W
    kc = jnp.where(low_half, lane, lane - GRID_W)
    in_window = (kc >= c0) & (kc < c0 + NA_KW)
    lo, hi = [], []
    for dr in range(2 * NA_KH - 1):
        row = jnp.broadcast_to(rp_ref[0, dr:dr + 1, :], (GRID_W, LANES))
        lo.append(jnp.where(in_window & low_half, pltpu.roll(row, GRID_W + 1, 1, stride=1, stride_axis=0),
                            MASK_VALUE))
        hi.append(jnp.where(in_window & jnp.logical_not(low_half),
                            pltpu.roll(row, 1, 1, stride=1, stride_axis=0), MASK_VALUE))
    for rel in range(NA_KH):
        pairs = [jnp.maximum(lo[2 * j - rel + NA_KH - 1], hi[2 * j - rel + NA_KH]) for j in range(NA_KH // 2)]
        o_ref[0, rel] = jnp.concatenate(pairs, axis=1)


def _na_bias_table(rpb):
    nh, ndr, ndc = rpb.shape
    lead = GRID_W - NA_KW
    rp = jnp.pad(rpb.astype(F32) * LOG2_E, ((0, 0), (0, 2 * NA_KH - ndr), (lead, LANES - lead - ndc)))
    return pl.pallas_call(
        _na_table_kernel,
        grid=(nh,),
        in_specs=[pl.BlockSpec((1, 2 * NA_KH, LANES), lambda h: (h, 0, 0))],
        out_specs=pl.BlockSpec((1, NA_KH, GRID_W, NA_KH * GRID_W), lambda h: (h, 0, 0, 0)),
        out_shape=jax.ShapeDtypeStruct((nh, NA_KH, GRID_W, NA_KH * GRID_W), F32),
        compiler_params=_params("parallel"),
        name="na_bias_table",
    )(rp)


def _na_kernel(q_ref, k_ref, v_ref, z_ref, tbl_ref, o_ref, *, rows):
    rb = pl.program_id(2)
    win = NA_KH * GRID_W
    lane_q = lax.broadcasted_iota(jnp.int32, (GRID_W, LANES), 1)
    head0_q = lane_q < NA_HEAD_DIM
    starts, scores = [], []
    for i in range(NA_ROWS_PER_STEP):
        r = rb * NA_ROWS_PER_STEP + i
        r0 = jnp.clip(r - NA_KH // 2, 0, rows - NA_KH)
        rel = r - r0
        start = pl.multiple_of(r0 * GRID_W, GRID_W)
        kw = k_ref[pl.ds(start, win), :]
        q = q_ref[i * GRID_W:(i + 1) * GRID_W, :]
        zero = jnp.zeros_like(q)
        qs = jnp.concatenate([jnp.where(head0_q, q, zero), jnp.where(head0_q, zero, q)], axis=0)
        bias = jnp.concatenate([tbl_ref[0, rel], tbl_ref[1, rel]], axis=0)
        scores.append(_dot_nt(qs, kw) * (NA_HEAD_DIM ** -0.5 * LOG2_E) + bias)
        starts.append(start)
    for i in range(NA_ROWS_PER_STEP):
        s = scores[i]
        e = jnp.exp2(s - jnp.max(s, axis=-1, keepdims=True))
        inv = 1.0 / jnp.sum(e, axis=-1, keepdims=True)
        o = _dot(e.astype(BF16), v_ref[pl.ds(starts[i], win), :]) * inv
        out = jnp.where(head0_q, o[:GRID_W], o[GRID_W:])
        zz = z_ref[i * GRID_W:(i + 1) * GRID_W, :].astype(F32)
        o_ref[i * GRID_W:(i + 1) * GRID_W, :] = (out * _silu(zz)).astype(o_ref.dtype)


def _natten(p, tbl, *, batch, seq, col0):
    t = p.shape[0]
    rows = seq // GRID_W
    width = NA_HEADS * NA_HEAD_DIM
    pairs = width // LANES
    tq = NA_ROWS_PER_STEP * GRID_W
    nrb = seq // tq
    cb = col0 // LANES
    kern = functools.partial(_na_kernel, rows=rows)
    return pl.pallas_call(
        kern,
        grid=(batch, pairs, nrb),
        in_specs=[
            pl.BlockSpec((tq, LANES), lambda b, h, r: (b * nrb + r, cb + h)),
            pl.BlockSpec((seq, LANES), lambda b, h, r: (b, cb + pairs + h)),
            pl.BlockSpec((seq, LANES), lambda b, h, r: (b, cb + 2 * pairs + h)),
            pl.BlockSpec((tq, LANES), lambda b, h, r: (b * nrb + r, cb + 3 * pairs + h)),
            pl.BlockSpec((2, NA_KH, GRID_W, NA_KH * GRID_W), lambda b, h, r: (h, 0, 0, 0)),
        ],
        out_specs=pl.BlockSpec((tq, LANES), lambda b, h, r: (b * nrb + r, h)),
        out_shape=jax.ShapeDtypeStruct((t, width), BF16),
        compiler_params=_params("parallel", "parallel", "arbitrary"),
        name="natten",
    )(p, p, p, p, tbl)


def _even_out_kernel(hf_ref, hb_ref, o_ref, z_ref, na_ref, g_ref, w_ref, x_ref, out_ref):
    wa = ML_HEADS * ML_HEAD_DIM
    y = _dot(na_ref[...], w_ref[wa:, :])
    for i in range(ML_HEADS):
        sl = slice(i * ML_HEAD_DIM, (i + 1) * ML_HEAD_DIM)
        h = hf_ref[:, sl].astype(F32) + hb_ref[:, sl].astype(F32)
        a = _rms(h) * g_ref[:, sl] * _sigmoid(o_ref[:, sl].astype(F32)) * _silu(z_ref[:, sl].astype(F32))
        y = y + _dot(a.astype(BF16), w_ref[sl, :])
    out_ref[...] = x_ref[...] + y


def _even_out(hf, hb, p, na, g, w, x, *, tm):
    t, d = x.shape
    wa = hf.shape[1]
    return pl.pallas_call(
        _even_out_kernel,
        grid=(t // tm,),
        in_specs=[
            pl.BlockSpec((tm, wa), lambda i: (i, 0)),
            pl.BlockSpec((tm, wa), lambda i: (i, 0)),
            pl.BlockSpec((tm, wa), lambda i: (i, 3)),
            pl.BlockSpec((tm, wa), lambda i: (i, 4)),
            pl.BlockSpec((tm, na.shape[1]), lambda i: (i, 0)),
            pl.BlockSpec((1, wa), lambda i: (0, 0)),
            pl.BlockSpec(w.shape, lambda i: (0, 0)),
            pl.BlockSpec((tm, d), lambda i: (i, 0)),
        ],
        out_specs=pl.BlockSpec((tm, d), lambda i: (i, 0)),
        out_shape=jax.ShapeDtypeStruct((t, d), F32),
        compiler_params=_params("parallel"),
        name="even_out_proj",
    )(hf, hb, p, p, na, g, w, x)


def _gqa_kernel(q_ref, k_ref, v_ref, o_ref, vaug_ref, m_ref, acc_ref, *, tk):
    tq = q_ref.shape[0]
    seq = k_ref.shape[0]
    dh = GQ_HEAD_DIM

    @pl.when(pl.program_id(2) == 0)
    def _():
        vaug_ref[:, :dh] = v_ref[...]
        vaug_ref[:, dh:] = jnp.ones((seq, dh), BF16)

    q = q_ref[...]
    qs = jnp.concatenate([q[:, g * dh:(g + 1) * dh] for g in range(GQ_GROUP)], axis=0)
    m_ref[...] = jnp.full_like(m_ref, -jnp.inf)
    acc_ref[...] = jnp.zeros_like(acc_ref)

    def scores(c):
        return _dot_nt(qs, k_ref[c * tk:(c + 1) * tk, :])

    def update(s, c):
        m_prev = m_ref[...]
        m_new = jnp.maximum(m_prev, jnp.max(s, axis=-1, keepdims=True))
        alpha = jnp.exp2(m_prev - m_new)
        p = jnp.exp2(s - jnp.concatenate([m_new] * (tk // LANES), axis=1))
        pv = _dot(p.astype(BF16), vaug_ref[c * tk:(c + 1) * tk, :])
        acc_ref[...] = jnp.concatenate([alpha, alpha], axis=1) * acc_ref[...] + pv
        m_ref[...] = m_new

    nk = seq // tk
    s_next = scores(0)
    for c in range(nk):
        s_cur = s_next
        if c + 1 < nk:
            s_next = scores(c + 1)
        update(s_cur, c)
    acc = acc_ref[...]
    out = acc[:, :dh] / acc[:, dh:]
    o_ref[...] = jnp.concatenate(
        [out[g * tq:(g + 1) * tq] for g in range(GQ_GROUP)], axis=-1).astype(o_ref.dtype)


def _gqa(p, *, batch, seq, k_col, v_col):
    t = p.shape[0]
    tq, tk = GQ_BLOCK_Q, GQ_BLOCK_K
    nq = seq // tq
    gw = GQ_GROUP * GQ_HEAD_DIM
    kb, vb = k_col // GQ_HEAD_DIM, v_col // GQ_HEAD_DIM
    kern = functools.partial(_gqa_kernel, tk=tk)
    return pl.pallas_call(
        kern,
        grid=(batch, GQ_KV_HEADS, nq),
        in_specs=[
            pl.BlockSpec((tq, gw), lambda b, h, i: (b * nq + i, h)),
            pl.BlockSpec((seq, GQ_HEAD_DIM), lambda b, h, i: (b, kb + h)),
            pl.BlockSpec((seq, GQ_HEAD_DIM), lambda b, h, i: (b, vb + h)),
        ],
        out_specs=pl.BlockSpec((tq, gw), lambda b, h, i: (b * nq + i, h)),
        out_shape=jax.ShapeDtypeStruct((t, GQ_HEADS * GQ_HEAD_DIM), BF16),
        scratch_shapes=[
            pltpu.VMEM((seq, 2 * GQ_HEAD_DIM), BF16),
            pltpu.VMEM((GQ_GROUP * tq, LANES), F32),
            pltpu.VMEM((GQ_GROUP * tq, 2 * GQ_HEAD_DIM), F32),
        ],
        compiler_params=_params("parallel", "parallel", "arbitrary"),
        name="gqa_attention",
    )(p, p, p)


def _odd_out_kernel(att_ref, gz_ref, cb_ref, cc_ref, cx_ref, cz_ref, ccp_ref, cxp_ref, ccn_ref, cxn_ref,
                    cw_ref, cbias_ref, w_ref, h_ref, fg_ref, out_ref, *, blocks_per_seq):
    tm, wa = att_ref.shape
    slab = 2 * LANES
    i = pl.program_id(0) % blocks_per_seq
    ridx = lax.broadcasted_iota(jnp.int32, (tm, slab), 0)
    hh = h_ref[...]
    for s in range(wa // slab):
        sl = slice(s * slab, (s + 1) * slab)
        att = att_ref[:, sl].astype(F32) * _silu(gz_ref[:, sl].astype(F32))
        hh = hh + _dot(att.astype(BF16), w_ref[sl, :])
    for s in range(wa // slab):
        sl = slice(s * slab, (s + 1) * slab)
        xc = cc_ref[:, sl].astype(F32) * cx_ref[:, sl].astype(F32)
        prev_row = ccp_ref[SUBLANES - 1:SUBLANES, sl].astype(F32) * cxp_ref[SUBLANES - 1:SUBLANES, sl].astype(F32)
        next_row = ccn_ref[0:1, sl].astype(F32) * cxn_ref[0:1, sl].astype(F32)
        prev_row = jnp.where(i == 0, 0.0, prev_row)
        next_row = jnp.where(i == blocks_per_seq - 1, 0.0, next_row)
        x_prev = jnp.where(ridx == 0, prev_row, pltpu.roll(xc, 1, 0))
        x_next = jnp.where(ridx == tm - 1, next_row, pltpu.roll(xc, tm - 1, 0))
        y = x_prev * cw_ref[0:1, sl] + xc * cw_ref[1:2, sl] + x_next * cw_ref[2:3, sl] + cbias_ref[:, sl]
        conv = cb_ref[:, sl].astype(F32) * y * _silu(cz_ref[:, sl].astype(F32))
        hh = hh + _dot(conv.astype(BF16), w_ref[wa + s * slab:wa + (s + 1) * slab, :])
    out_ref[...] = _rms(hh) * fg_ref[...]


def _odd_out(att, p, cw, cbias, w, h, fg, *, tm, seq, cols):
    t, d = h.shape
    bps = seq // tm
    sub = tm // SUBLANES
    nsub = t // SUBLANES
    z_c, b_c, c_c, x_c, cz_c = cols

    def blk(cb):
        return pl.BlockSpec((tm, d), lambda i: (i, cb))

    def halo_prev(cb):
        return pl.BlockSpec((SUBLANES, d), lambda i: (jnp.maximum(i * sub - 1, 0), cb))

    def halo_next(cb):
        return pl.BlockSpec((SUBLANES, d), lambda i: (jnp.minimum((i + 1) * sub, nsub - 1), cb))

    kern = functools.partial(_odd_out_kernel, blocks_per_seq=bps)
    return pl.pallas_call(
        kern,
        grid=(t // tm,),
        in_specs=[
            pl.BlockSpec((tm, d), lambda i: (i, 0)),
            blk(z_c), blk(b_c), blk(c_c), blk(x_c), blk(cz_c),
            halo_prev(c_c), halo_prev(x_c), halo_next(c_c), halo_next(x_c),
            pl.BlockSpec(cw.shape, lambda i: (0, 0)),
            pl.BlockSpec((1, d), lambda i: (0, 0)),
            pl.BlockSpec(w.shape, lambda i: (0, 0)),
            pl.BlockSpec((tm, d), lambda i: (i, 0)),
            pl.BlockSpec((1, d), lambda i: (0, 0)),
        ],
        out_specs=pl.BlockSpec((tm, d), lambda i: (i, 0)),
        out_shape=jax.ShapeDtypeStruct((t, d), F32),
        compiler_params=_params("parallel"),
        name="odd_out_proj",
    )(att, p, p, p, p, p, p, p, p, p, cw, cbias, w, h, fg)


def _rope_tables(seq):
    rows = seq // GRID_W
    inv = ROPE_THETA ** (-jnp.arange(ROPE_HALF, dtype=F32) / ROPE_HALF)
    ang_r = jnp.arange(rows).astype(F32)[:, None] * inv
    ang_c = jnp.arange(GRID_W).astype(F32)[:, None] * inv

    def table(by_row, by_col, sign_x1):
        r = jnp.broadcast_to(by_row[:, None, :], (rows, GRID_W, ROPE_HALF))
        c = jnp.broadcast_to(by_col[None, :, :], (rows, GRID_W, ROPE_HALF))
        return jnp.concatenate([sign_x1 * r, sign_x1 * c, r, c], axis=-1).reshape(seq, GQ_HEAD_DIM)

    return table(jnp.cos(ang_r), jnp.cos(ang_c), 1.0), table(jnp.sin(ang_r), jnp.sin(ang_c), -1.0)


def _half_swap(a):
    lead = a.shape[:-1]
    a = a.reshape(*lead, -1, 2, GQ_HEAD_DIM // 2)
    return a[..., ::-1, :].reshape(*lead, -1)


def _rope_dim_order(a):
    lead = a.shape[:-1]
    a = a.reshape(*lead, -1, 2, 2, ROPE_HALF)
    return jnp.swapaxes(a, -3, -2).reshape(*lead, -1)


def kernel(x, norm_g, final_g, ev_w_in, ev_gate_b, ev_w_out, ev_ml_norm_g, ev_na_rpb,
           od_w_in, od_w_out, od_q_norm_g, od_k_norm_g, od_conv_w, od_conv_b):
    batch, seq, d = x.shape
    assert d == D_MODEL and norm_g.shape[0] == 2
    assert seq % max(ML_CHUNK, GQ_BLOCK_K, GQ_BLOCK_Q, NA_ROWS_PER_STEP * GRID_W, 512) == 0
    t = batch * seq
    xf = x.reshape(t, d)
    tm_in = min(2048, seq)
    tm_out = 512

    n_gate = 4 * ML_HEADS
    g0 = 5 * D_MODEL
    w_a, w_b, w_gate = _even_w_prep(ev_w_in[0], g0=g0, n_gate=n_gate, rows=128)
    tn_even = 1024
    p_ev, gates = _even_in_proj(xf, norm_g[0][None, :], w_a, w_b, w_gate,
                                na=g0 // tn_even, tm=tm_in, tn=tn_even)
    gate_bias = jnp.pad(ev_gate_b[0], (0, LANES - n_gate))[None, :]
    hf, hb = _mlstm(p_ev, gates, gate_bias, batch=batch, seq=seq)
    na = _natten(p_ev, _na_bias_table(ev_na_rpb[0]), batch=batch, seq=seq, col0=g0)
    h1 = _even_out(hf, hb, p_ev, na, ev_ml_norm_g[0][None, :], ev_w_out[0].astype(BF16), xf, tm=tm_out)

    w_in = od_w_in[0]
    qw, kvw = GQ_HEADS * GQ_HEAD_DIM, GQ_KV_HEADS * GQ_HEAD_DIM
    tn_odd = 512
    wq, wk = _rope_dim_order(w_in[:, :qw]), _rope_dim_order(w_in[:, qw:qw + kvw])
    w_qkv = jnp.concatenate([wq, wk, w_in[:, qw + kvw:qw + 2 * kvw]], axis=1).astype(BF16)
    w_swap = jnp.concatenate([_half_swap(wq), _half_swap(wk), jnp.zeros((d, tn_odd - kvw), w_in.dtype)],
                             axis=1).astype(BF16)
    cos, sin = _rope_tables(seq)
    qg = jnp.tile(_rope_dim_order(od_q_norm_g[0]), tn_odd // GQ_HEAD_DIM)[None, :]
    kg = jnp.tile(_rope_dim_order(od_k_norm_g[0]), GQ_KV_HEADS)[None, :]
    plain0 = (qw + 2 * kvw) // tn_odd
    p_od = _odd_in_proj(h1, norm_g[1][None, :], w_qkv, w_swap, w_in.astype(BF16), qg, _half_swap(qg),
                        kg, _half_swap(kg), cos, sin, plain0=plain0, n_plain=w_in.shape[1] // tn_odd - plain0,
                        tm=tm_in, tn=tn_odd, seq=seq)
    k_col = p_od.shape[1] - 2 * kvw
    att = _gqa(p_od, batch=batch, seq=seq, k_col=k_col, v_col=k_col + kvw)
    out = _odd_out(att, p_od, od_conv_w[0], od_conv_b[0][None, :], od_w_out[0].astype(BF16), h1,
                   final_g[None, :], tm=tm_out, seq=seq, cols=(1, 2, 3, 4, 5))
    return out.reshape(batch, seq, d)
```

```python
import functools

import jax
import jax.numpy as jnp
from jax import lax
from jax.experimental import pallas as pl
from jax.experimental.pallas import tpu as pltpu

F32 = jnp.float32
BF16 = jnp.bfloat16

D_MODEL = 1024
GRID_W = 64
RMS_EPS = 1e-6

ML_HEADS = 4
ML_HEAD_DIM = 256
ML_CHUNK = 256

NA_HEADS = 16
NA_HEAD_DIM = 64
NA_KH = 8
NA_KW = 16
NA_ROWS_PER_STEP = 64

GQ_HEADS = 8
GQ_HEAD_DIM = 128
GQ_KV_HEADS = 2
GQ_GROUP = GQ_HEADS // GQ_KV_HEADS
ROPE_THETA = 10000.0
ROPE_HALF = GQ_HEAD_DIM // 4
GQ_BLOCK_Q = 256
GQ_BLOCK_K = 512

CV_K = 3

LANES = 128
SUBLANES = 8
MASK_VALUE = -1e30
LOG2_E = 1.4426950408889634

VMEM_LIMIT = 56 * 1024 * 1024


def _params(*sem):
    return pltpu.CompilerParams(dimension_semantics=sem, vmem_limit_bytes=VMEM_LIMIT)


def _sigmoid(x):
    return 0.5 * jnp.tanh(0.5 * x) + 0.5


def _silu(x):
    return x * _sigmoid(x)


def _log_sigmoid(x):
    return jnp.minimum(x, 0.0) - jnp.log1p(jnp.exp(-jnp.abs(x)))


def _rms(x, eps=RMS_EPS):
    return x * lax.rsqrt(jnp.mean(x * x, axis=-1, keepdims=True) + eps)


def _dot(a, b):
    return jnp.dot(a, b, preferred_element_type=F32)


def _dot_nt(a, b):
    return lax.dot_general(a, b, (((1,), (1,)), ((), ())), preferred_element_type=F32)


def _dot_tn(a, b):
    return lax.dot_general(a, b, (((0,), (0,)), ((), ())), preferred_element_type=F32)


def _even_in_kernel(x_ref, g_ref, wa_ref, wb_ref, wg_ref, o_ref, og_ref, u_ref, *, n_a_tiles):
    j = pl.program_id(1)

    @pl.when(j == 0)
    def _():
        u = (_rms(x_ref[...]) * g_ref[...]).astype(BF16)
        u_ref[...] = u
        og_ref[...] = _dot(u, wg_ref[...])

    @pl.when(j < n_a_tiles)
    def _():
        o_ref[...] = _dot(u_ref[...], wa_ref[...]).astype(o_ref.dtype)

    @pl.when(j >= n_a_tiles)
    def _():
        o_ref[...] = _dot(u_ref[...], wb_ref[...]).astype(o_ref.dtype)


def _even_in_proj(x, g, wa, wb, wg, *, na, tm, tn):
    t, d = x.shape
    nb = wb.shape[1] // tn
    n = (na + nb) * tn
    kern = functools.partial(_even_in_kernel, n_a_tiles=na)
    return pl.pallas_call(
        kern,
        grid=(t // tm, na + nb),
        in_specs=[
            pl.BlockSpec((tm, d), lambda i, j: (i, 0)),
            pl.BlockSpec((1, d), lambda i, j: (0, 0)),
            pl.BlockSpec((d, tn), lambda i, j: (0, jnp.minimum(j, na - 1))),
            pl.BlockSpec((d, tn), lambda i, j: (0, jnp.maximum(j - na, 0))),
            pl.BlockSpec((d, LANES), lambda i, j: (0, 0)),
        ],
        out_specs=[
            pl.BlockSpec((tm, tn), lambda i, j: (i, j)),
            pl.BlockSpec((tm, LANES), lambda i, j: (i, 0)),
        ],
        out_shape=[
            jax.ShapeDtypeStruct((t, n), BF16),
            jax.ShapeDtypeStruct((t, LANES), F32),
        ],
        scratch_shapes=[pltpu.VMEM((tm, d), BF16)],
        compiler_params=_params("parallel", "arbitrary"),
        name="even_in_proj",
    )(x, g, wa, wb, wg)


def _rope_norm(acc, acc_sw, gain, gain_sw, cos, sin, scale):
    nh = acc.shape[1] // GQ_HEAD_DIM
    heads = []
    for h in range(nh):
        sl = slice(h * GQ_HEAD_DIM, (h + 1) * GQ_HEAD_DIM)
        a = acc[:, sl]
        r = lax.rsqrt(jnp.mean(a * a, axis=-1, keepdims=True) + RMS_EPS)
        y = a * r * gain[:, sl]
        partner = acc_sw[:, sl] * r * gain_sw[:, sl]
        heads.append((y * cos + partner * sin) * scale)
    return jnp.concatenate(heads, axis=-1)


def _odd_in_kernel(x_ref, g_ref, wa_ref, wsw_ref, wb_ref, qg_ref, qgs_ref, kg_ref, kgs_ref, cos_ref, sin_ref,
                   o_ref, u_ref, *, n_q_tiles, kv_tile):
    j = pl.program_id(1)

    @pl.when(j == 0)
    def _():
        u_ref[...] = (_rms(x_ref[...]) * g_ref[...]).astype(BF16)

    is_q = j < n_q_tiles
    is_kv = j == kv_tile

    @pl.when(is_q)
    def _():
        u = u_ref[...]
        o_ref[...] = _rope_norm(_dot(u, wa_ref[...]), _dot(u, wsw_ref[...]), qg_ref[...], qgs_ref[...],
                                cos_ref[...], sin_ref[...], GQ_HEAD_DIM ** -0.5 * LOG2_E).astype(o_ref.dtype)

    @pl.when(is_kv)
    def _():
        kw = GQ_KV_HEADS * GQ_HEAD_DIM
        u = u_ref[...]
        acc = _dot(u, wa_ref[...])
        k = _rope_norm(acc[:, :kw], _dot(u, wsw_ref[:, :kw]), kg_ref[...], kgs_ref[...],
                       cos_ref[...], sin_ref[...], 1.0)
        o_ref[...] = jnp.concatenate([k, acc[:, kw:]], axis=-1).astype(o_ref.dtype)

    @pl.when(jnp.logical_not(jnp.logical_or(is_q, is_kv)))
    def _():
        o_ref[...] = _dot(u_ref[...], wb_ref[...]).astype(o_ref.dtype)


def _odd_in_proj(x, g, wa, wsw, wb, qg, qgs, kg, kgs, cos, sin, *, plain0, n_plain, tm, tn, seq):
    t, d = x.shape
    n_q_tiles = (GQ_HEADS * GQ_HEAD_DIM) // tn
    kv_tile = n_q_tiles + n_plain
    n = (kv_tile + 1) * tn
    kvw = GQ_KV_HEADS * GQ_HEAD_DIM
    sb = seq // tm
    kern = functools.partial(_odd_in_kernel, n_q_tiles=n_q_tiles, kv_tile=kv_tile)

    def qkv_map(i, j):
        return (0, jnp.where(j < n_q_tiles, j, jnp.where(j == kv_tile, n_q_tiles, n_q_tiles - 1)))

    return pl.pallas_call(
        kern,
        grid=(t // tm, kv_tile + 1),
        in_specs=[
            pl.BlockSpec((tm, d), lambda i, j: (i, 0)),
            pl.BlockSpec((1, d), lambda i, j: (0, 0)),
            pl.BlockSpec((d, tn), qkv_map),
            pl.BlockSpec((d, tn), qkv_map),
            pl.BlockSpec((d, tn), lambda i, j: (0, plain0 + jnp.clip(j - n_q_tiles, 0, n_plain - 1))),
            pl.BlockSpec((1, tn), lambda i, j: (0, 0)),
            pl.BlockSpec((1, tn), lambda i, j: (0, 0)),
            pl.BlockSpec((1, kvw), lambda i, j: (0, 0)),
            pl.BlockSpec((1, kvw), lambda i, j: (0, 0)),
            pl.BlockSpec((tm, GQ_HEAD_DIM), lambda i, j: (i % sb, 0)),
            pl.BlockSpec((tm, GQ_HEAD_DIM), lambda i, j: (i % sb, 0)),
        ],
        out_specs=pl.BlockSpec((tm, tn), lambda i, j: (i, j)),
        out_shape=jax.ShapeDtypeStruct((t, n), BF16),
        scratch_shapes=[pltpu.VMEM((tm, d), BF16)],
        compiler_params=_params("parallel", "arbitrary"),
        name="odd_in_proj",
    )(x, g, wa, wsw, wb, qg, qgs, kg, kgs, cos, sin)


def _split3(a):
    hi = a.astype(BF16)
    r = a - hi.astype(F32)
    mid = r.astype(BF16)
    return hi, mid, (r - mid.astype(F32)).astype(BF16)


def _dot_exact_lhs(a, b):
    return sum(_dot(p, b) for p in _split3(a))


def _mlstm_kernel(qf_ref, kf_ref, vf_ref, gf_ref, qb_ref, kb_ref, vb_ref, gb_ref, bias_ref, sel_ref,
                  hf_ref, hb_ref, c_ref, n_ref, m_ref):
    nb, L = qf_ref.shape[:2]
    dh = ML_HEAD_DIM
    k_scale = dh ** -0.5

    @pl.when(pl.program_id(0) == 0)
    def _():
        c_ref[...] = jnp.zeros_like(c_ref)
        n_ref[...] = jnp.zeros_like(n_ref)
        m_ref[...] = jnp.zeros_like(m_ref)

    row = lax.broadcasted_iota(jnp.int32, (L, L), 0)
    col = lax.broadcasted_iota(jnp.int32, (L, L), 1)
    lower = row >= col
    upper = row <= col
    n_gate = 4 * ML_HEADS

    def wide(x):
        return jnp.concatenate([x] * (dh // LANES), axis=1)

    gate_row = lax.broadcasted_iota(jnp.int32, (n_gate, L), 0)
    forget_row = ((gate_row >= ML_HEADS) & (gate_row < 2 * ML_HEADS)) | (gate_row >= 3 * ML_HEADS)

    def gate_tables(g, reach, first_col):
        gt = (g + bias_ref[...]).T[:n_gate, :]
        lf = jnp.where(forget_row, _log_sigmoid(gt), 0.0)
        cum = _dot_exact_lhs(lf, reach.astype(BF16))
        tab_t = jnp.where(forget_row, cum, gt) * LOG2_E
        sel = sel_ref[:, first_col * LANES:(first_col + 2 * ML_HEADS) * LANES]
        rep = sum(_dot_tn(p, sel) for p in _split3(tab_t))
        return rep, tab_t

    def lanes(rep, j):
        return rep[:, j * LANES:(j + 1) * LANES]

    chains = []
    for b in range(nb):
        rep_f, tab_ft = gate_tables(gf_ref[b], upper, 0)
        rep_b, tab_bt = gate_tables(gb_ref[b], lower, 2 * ML_HEADS)
        fwd_refs = (qf_ref.at[b], kf_ref.at[b], vf_ref.at[b], hf_ref.at[b])
        bwd_refs = (qb_ref.at[b], kb_ref.at[b], vb_ref.at[b], hb_ref.at[b])
        for hd in range(ML_HEADS):
            base = 2 * ML_HEADS * b
            chains.append((base + hd, fwd_refs, rep_f, tab_ft, 0, L - 1, lower, hd))
            chains.append((base + ML_HEADS + hd, bwd_refs, rep_b, tab_bt, 2 * ML_HEADS, 0, upper, hd))

    stage1 = []
    for idx, (q_ref, k_ref, _, _), _, _, _, _, _, hd in chains:
        sl = slice(hd * dh, (hd + 1) * dh)
        q = q_ref[:, sl]
        qk = _dot_nt(q * jnp.asarray(k_scale, q.dtype), k_ref[:, sl])
        stage1.append((qk, _dot(q, c_ref[idx].astype(BF16))))

    for (idx, (q_ref, k_ref, v_ref, h_ref), rep, tab_t, go, g_row, valid, hd), (qk, qc) in zip(chains, stage1):
        sl = slice(hd * dh, (hd + 1) * dh)
        fi, ff = hd, ML_HEADS + hd
        q, k, v = q_ref[:, sl], k_ref[:, sl], v_ref[:, sl]
        b128, ig128 = lanes(rep, ff), lanes(rep, fi)
        brow, igrow = tab_t[go + ff:go + ff + 1, :], tab_t[go + fi:go + fi + 1, :]
        g128 = b128[g_row:g_row + 1, :]
        n = n_ref[idx][0:1, :]
        m_prev = m_ref[idx][0:1, :]
        d = jnp.where(valid, wide(b128) + (igrow - brow), MASK_VALUE)
        inter = b128 + m_prev
        m_t = jnp.maximum(inter, jnp.max(d, axis=-1, keepdims=True))
        w = jnp.exp2(d - wide(m_t)) * qk
        decay = jnp.exp2(inter - m_t)
        num = wide(decay) * qc + _dot(w.astype(BF16), v)
        qn = jnp.sum(q.astype(F32) * n, axis=-1, keepdims=True)
        den = decay * qn + jnp.sum(w, axis=-1, keepdims=True)
        h = num * wide(1.0 / jnp.maximum(jnp.abs(den), jnp.exp2(-m_t)))
        h_ref[:, sl] = h.astype(h_ref.dtype)
        a = g128 - b128 + ig128
        m_new = jnp.maximum(g128 + m_prev, jnp.max(a, axis=0, keepdims=True))
        carry = wide(jnp.exp2(g128 + m_prev - m_new))
        wk = wide(jnp.exp2(a - m_new) * k_scale)
        wkv = (wk * v.astype(F32)).astype(BF16)
        c_ref[idx] = carry * c_ref[idx] + _dot_tn(k, wkv)
        n_new = carry * n + jnp.sum(wk * k.astype(F32), axis=0, keepdims=True)
        n_ref[idx] = jnp.broadcast_to(n_new, n_ref.shape[1:])
        m_ref[idx] = jnp.broadcast_to(m_new, m_ref.shape[1:])


def _mlstm(p, gates, bias, *, batch, seq):
    t = p.shape[0]
    L = ML_CHUNK
    nc = seq // L
    w = ML_HEADS * ML_HEAD_DIM
    n_gate = 4 * ML_HEADS
    sel = (jnp.arange(n_gate)[:, None] == (jnp.arange(n_gate * LANES) // LANES)[None, :]).astype(BF16)
    p3 = p.reshape(batch, seq, p.shape[1])
    g3 = gates.reshape(batch, seq, LANES)

    def fwd(cb):
        return lambda c: (0, c, cb)

    def bwd(cb):
        return lambda c: (0, nc - 1 - c, cb)

    def blk(width, index_map):
        return pl.BlockSpec((batch, L, width), index_map)

    hf, hb = pl.pallas_call(
        _mlstm_kernel,
        grid=(nc,),
        in_specs=[
            blk(w, fwd(0)), blk(w, fwd(1)), blk(w, fwd(2)), blk(LANES, fwd(0)),
            blk(w, bwd(0)), blk(w, bwd(1)), blk(w, bwd(2)), blk(LANES, bwd(0)),
            pl.BlockSpec((1, LANES), lambda c: (0, 0)),
            pl.BlockSpec(sel.shape, lambda c: (0, 0)),
        ],
        out_specs=[blk(w, fwd(0)), blk(w, bwd(0))],
        out_shape=[jax.ShapeDtypeStruct((batch, seq, w), BF16), jax.ShapeDtypeStruct((batch, seq, w), BF16)],
        scratch_shapes=[
            pltpu.VMEM((batch * 2 * ML_HEADS, ML_HEAD_DIM, ML_HEAD_DIM), F32),
            pltpu.VMEM((batch * 2 * ML_HEADS, SUBLANES, ML_HEAD_DIM), F32),
            pltpu.VMEM((batch * 2 * ML_HEADS, SUBLANES, LANES), F32),
        ],
        compiler_params=_params("arbitrary"),
        name="mlstm",
    )(p3, p3, p3, g3, p3, p3, p3, g3, bias, sel)
    return hf.reshape(t, w), hb.reshape(t, w)


def _na_table_kernel(rp_ref, o_ref):
    c = lax.broadcasted_iota(jnp.int32, (GRID_W, LANES), 0)
    lane = lax.broadcasted_iota(jnp.int32, (GRID_W, LANES), 1)
    c0 = jnp.clip(c - NA_KW // 2, 0, GRID_W - NA_KW)
    low_half = lane < GRID_W
    kc = jnp.where(low_half, lane, lane - GRID_W)
    in_window = (kc >= c0) & (kc < c0 + NA_KW)
    lo, hi = [], []
    for dr in range(2 * NA_KH - 1):
        row = jnp.broadcast_to(rp_ref[0, dr:dr + 1, :], (GRID_W, LANES))
        lo.append(jnp.where(in_window & low_half, pltpu.roll(row, GRID_W + 1, 1, stride=1, stride_axis=0),
                            MASK_VALUE))
        hi.append(jnp.where(in_window & jnp.logical_not(low_half),
                            pltpu.roll(row, 1, 1, stride=1, stride_axis=0), MASK_VALUE))
    for rel in range(NA_KH):
        pairs = [jnp.maximum(lo[2 * j - rel + NA_KH - 1], hi[2 * j - rel + NA_KH]) for j in range(NA_KH // 2)]
        o_ref[0, rel] = jnp.concatenate(pairs, axis=1)


def _na_bias_table(rpb):
    nh, ndr, ndc = rpb.shape
    lead = GRID_W - NA_KW
    rp = jnp.pad(rpb.astype(F32) * LOG2_E, ((0, 0), (0, 2 * NA_KH - ndr), (lead, LANES - lead - ndc)))
    return pl.pallas_call(
        _na_table_kernel,
        grid=(nh,),
        in_specs=[pl.BlockSpec((1, 2 * NA_KH, LANES), lambda h: (h, 0, 0))],
        out_specs=pl.BlockSpec((1, NA_KH, GRID_W, NA_KH * GRID_W), lambda h: (h, 0, 0, 0)),
        out_shape=jax.ShapeDtypeStruct((nh, NA_KH, GRID_W, NA_KH * GRID_W), F32),
        compiler_params=_params("parallel"),
        name="na_bias_table",
    )(rp)


def _na_kernel(q_ref, k_ref, v_ref, z_ref, tbl_ref, o_ref, *, rows):
    rb = pl.program_id(2)
    win = NA_KH * GRID_W
    lane_q = lax.broadcasted_iota(jnp.int32, (GRID_W, LANES), 1)
    head0_q = lane_q < NA_HEAD_DIM
    starts, scores = [], []
    for i in range(NA_ROWS_PER_STEP):
        r = rb * NA_ROWS_PER_STEP + i
        r0 = jnp.clip(r - NA_KH // 2, 0, rows - NA_KH)
        rel = r - r0
        start = pl.multiple_of(r0 * GRID_W, GRID_W)
        kw = k_ref[pl.ds(start, win), :]
        q = q_ref[i * GRID_W:(i + 1) * GRID_W, :]
        zero = jnp.zeros_like(q)
        qs = jnp.concatenate([jnp.where(head0_q, q, zero), jnp.where(head0_q, zero, q)], axis=0)
        bias = jnp.concatenate([tbl_ref[0, rel], tbl_ref[1, rel]], axis=0)
        scores.append(_dot_nt(qs, kw) * (NA_HEAD_DIM ** -0.5 * LOG2_E) + bias)
        starts.append(start)
    for i in range(NA_ROWS_PER_STEP):
        s = scores[i]
        e = jnp.exp2(s - jnp.max(s, axis=-1, keepdims=True))
        inv = 1.0 / jnp.sum(e, axis=-1, keepdims=True)
        o = _dot(e.astype(BF16), v_ref[pl.ds(starts[i], win), :]) * inv
        out = jnp.where(head0_q, o[:GRID_W], o[GRID_W:])
        zz = z_ref[i * GRID_W:(i + 1) * GRID_W, :].astype(F32)
        o_ref[i * GRID_W:(i + 1) * GRID_W, :] = (out * _silu(zz)).astype(o_ref.dtype)


def _natten(p, tbl, *, batch, seq, col0):
    t = p.shape[0]
    rows = seq // GRID_W
    width = NA_HEADS * NA_HEAD_DIM
    pairs = width // LANES
    tq = NA_ROWS_PER_STEP * GRID_W
    nrb = seq // tq
    cb = col0 // LANES
    kern = functools.partial(_na_kernel, rows=rows)
    return pl.pallas_call(
        kern,
        grid=(batch, pairs, nrb),
        in_specs=[
            pl.BlockSpec((tq, LANES), lambda b, h, r: (b * nrb + r, cb + h)),
            pl.BlockSpec((seq, LANES), lambda b, h, r: (b, cb + pairs + h)),
            pl.BlockSpec((seq, LANES), lambda b, h, r: (b, cb + 2 * pairs + h)),
            pl.BlockSpec((tq, LANES), lambda b, h, r: (b * nrb + r, cb + 3 * pairs + h)),
            pl.BlockSpec((2, NA_KH, GRID_W, NA_KH * GRID_W), lambda b, h, r: (h, 0, 0, 0)),
        ],
        out_specs=pl.BlockSpec((tq, LANES), lambda b, h, r: (b * nrb + r, h)),
        out_shape=jax.ShapeDtypeStruct((t, width), BF16),
        compiler_params=_params("parallel", "parallel", "arbitrary"),
        name="natten",
    )(p, p, p, p, tbl)


def _even_out_kernel(hf_ref, hb_ref, o_ref, z_ref, na_ref, g_ref, w_ref, x_ref, out_ref):
    wa = ML_HEADS * ML_HEAD_DIM
    y = _dot(na_ref[...], w_ref[wa:, :])
    for i in range(ML_HEADS):
        sl = slice(i * ML_HEAD_DIM, (i + 1) * ML_HEAD_DIM)
        h = hf_ref[:, sl].astype(F32) + hb_ref[:, sl].astype(F32)
        a = _rms(h) * g_ref[:, sl] * _sigmoid(o_ref[:, sl].astype(F32)) * _silu(z_ref[:, sl].astype(F32))
        y = y + _dot(a.astype(BF16), w_ref[sl, :])
    out_ref[...] = x_ref[...] + y


def _even_out(hf, hb, p, na, g, w, x, *, tm):
    t, d = x.shape
    wa = hf.shape[1]
    return pl.pallas_call(
        _even_out_kernel,
        grid=(t // tm,),
        in_specs=[
            pl.BlockSpec((tm, wa), lambda i: (i, 0)),
            pl.BlockSpec((tm, wa), lambda i: (i, 0)),
            pl.BlockSpec((tm, wa), lambda i: (i, 3)),
            pl.BlockSpec((tm, wa), lambda i: (i, 4)),
            pl.BlockSpec((tm, na.shape[1]), lambda i: (i, 0)),
            pl.BlockSpec((1, wa), lambda i: (0, 0)),
            pl.BlockSpec(w.shape, lambda i: (0, 0)),
            pl.BlockSpec((tm, d), lambda i: (i, 0)),
        ],
        out_specs=pl.BlockSpec((tm, d), lambda i: (i, 0)),
        out_shape=jax.ShapeDtypeStruct((t, d), F32),
        compiler_params=_params("parallel"),
        name="even_out_proj",
    )(hf, hb, p, p, na, g, w, x)


def _gqa_kernel(q_ref, k_ref, v_ref, o_ref, vaug_ref, m_ref, acc_ref, *, tk):
    tq = q_ref.shape[0]
    seq = k_ref.shape[0]
    dh = GQ_HEAD_DIM

    @pl.when(pl.program_id(2) == 0)
    def _():
        vaug_ref[:, :dh] = v_ref[...]
        vaug_ref[:, dh:] = jnp.ones((seq, dh), BF16)

    q = q_ref[...]
    qs = jnp.concatenate([q[:, g * dh:(g + 1) * dh] for g in range(GQ_GROUP)], axis=0)
    m_ref[...] = jnp.full_like(m_ref, -jnp.inf)
    acc_ref[...] = jnp.zeros_like(acc_ref)

    def scores(c):
        return _dot_nt(qs, k_ref[c * tk:(c + 1) * tk, :])

    def update(s, c):
        m_prev = m_ref[...]
        m_new = jnp.maximum(m_prev, jnp.max(s, axis=-1, keepdims=True))
        alpha = jnp.exp2(m_prev - m_new)
        p = jnp.exp2(s - jnp.concatenate([m_new] * (tk // LANES), axis=1))
        pv = _dot(p.astype(BF16), vaug_ref[c * tk:(c + 1) * tk, :])
        acc_ref[...] = jnp.concatenate([alpha, alpha], axis=1) * acc_ref[...] + pv
        m_ref[...] = m_new

    nk = seq // tk
    s_next = scores(0)
    for c in range(nk):
        s_cur = s_next
        if c + 1 < nk:
            s_next = scores(c + 1)
        update(s_cur, c)
    acc = acc_ref[...]
    out = acc[:, :dh] / acc[:, dh:]
    o_ref[...] = jnp.concatenate(
        [out[g * tq:(g + 1) * tq] for g in range(GQ_GROUP)], axis=-1).astype(o_ref.dtype)


def _gqa(p, *, batch, seq, k_col, v_col):
    t = p.shape[0]
    tq, tk = GQ_BLOCK_Q, GQ_BLOCK_K
    nq = seq // tq
    gw = GQ_GROUP * GQ_HEAD_DIM
    kb, vb = k_col // GQ_HEAD_DIM, v_col // GQ_HEAD_DIM
    kern = functools.partial(_gqa_kernel, tk=tk)
    return pl.pallas_call(
        kern,
        grid=(batch, GQ_KV_HEADS, nq),
        in_specs=[
            pl.BlockSpec((tq, gw), lambda b, h, i: (b * nq + i, h)),
            pl.BlockSpec((seq, GQ_HEAD_DIM), lambda b, h, i: (b, kb + h)),
            pl.BlockSpec((seq, GQ_HEAD_DIM), lambda b, h, i: (b, vb + h)),
        ],
        out_specs=pl.BlockSpec((tq, gw), lambda b, h, i: (b * nq + i, h)),
        out_shape=jax.ShapeDtypeStruct((t, GQ_HEADS * GQ_HEAD_DIM), BF16),
        scratch_shapes=[
            pltpu.VMEM((seq, 2 * GQ_HEAD_DIM), BF16),
            pltpu.VMEM((GQ_GROUP * tq, LANES), F32),
            pltpu.VMEM((GQ_GROUP * tq, 2 * GQ_HEAD_DIM), F32),
        ],
        compiler_params=_params("parallel", "parallel", "arbitrary"),
        name="gqa_attention",
    )(p, p, p)


def _odd_out_kernel(att_ref, gz_ref, cb_ref, cc_ref, cx_ref, cz_ref, ccp_ref, cxp_ref, ccn_ref, cxn_ref,
                    cw_ref, cbias_ref, w_ref, h_ref, fg_ref, out_ref, *, blocks_per_seq):
    tm, wa = att_ref.shape
    slab = 2 * LANES
    i = pl.program_id(0) % blocks_per_seq
    ridx = lax.broadcasted_iota(jnp.int32, (tm, slab), 0)
    hh = h_ref[...]
    for s in range(wa // slab):
        sl = slice(s * slab, (s + 1) * slab)
        att = att_ref[:, sl].astype(F32) * _silu(gz_ref[:, sl].astype(F32))
        hh = hh + _dot(att.astype(BF16), w_ref[sl, :])
    for s in range(wa // slab):
        sl = slice(s * slab, (s + 1) * slab)
        xc = cc_ref[:, sl].astype(F32) * cx_ref[:, sl].astype(F32)
        prev_row = ccp_ref[SUBLANES - 1:SUBLANES, sl].astype(F32) * cxp_ref[SUBLANES - 1:SUBLANES, sl].astype(F32)
        next_row = ccn_ref[0:1, sl].astype(F32) * cxn_ref[0:1, sl].astype(F32)
        prev_row = jnp.where(i == 0, 0.0, prev_row)
        next_row = jnp.where(i == blocks_per_seq - 1, 0.0, next_row)
        x_prev = jnp.where(ridx == 0, prev_row, pltpu.roll(xc, 1, 0))
        x_next = jnp.where(ridx == tm - 1, next_row, pltpu.roll(xc, tm - 1, 0))
        y = x_prev * cw_ref[0:1, sl] + xc * cw_ref[1:2, sl] + x_next * cw_ref[2:3, sl] + cbias_ref[:, sl]
        conv = cb_ref[:, sl].astype(F32) * y * _silu(cz_ref[:, sl].astype(F32))
        hh = hh + _dot(conv.astype(BF16), w_ref[wa + s * slab:wa + (s + 1) * slab, :])
    out_ref[...] = _rms(hh) * fg_ref[...]


def _odd_out(att, p, cw, cbias, w, h, fg, *, tm, seq, cols):
    t, d = h.shape
    bps = seq // tm
    sub = tm // SUBLANES
    nsub = t // SUBLANES
    z_c, b_c, c_c, x_c, cz_c = cols

    def blk(cb):
        return pl.BlockSpec((tm, d), lambda i: (i, cb))

    def halo_prev(cb):
        return pl.BlockSpec((SUBLANES, d), lambda i: (jnp.maximum(i * sub - 1, 0), cb))

    def halo_next(cb):
        return pl.BlockSpec((SUBLANES, d), lambda i: (jnp.minimum((i + 1) * sub, nsub - 1), cb))

    kern = functools.partial(_odd_out_kernel, blocks_per_seq=bps)
    return pl.pallas_call(
        kern,
        grid=(t // tm,),
        in_specs=[
            pl.BlockSpec((tm, d), lambda i: (i, 0)),
            blk(z_c), blk(b_c), blk(c_c), blk(x_c), blk(cz_c),
            halo_prev(c_c), halo_prev(x_c), halo_next(c_c), halo_next(x_c),
            pl.BlockSpec(cw.shape, lambda i: (0, 0)),
            pl.BlockSpec((1, d), lambda i: (0, 0)),
            pl.BlockSpec(w.shape, lambda i: (0, 0)),
            pl.BlockSpec((tm, d), lambda i: (i, 0)),
            pl.BlockSpec((1, d), lambda i: (0, 0)),
        ],
        out_specs=pl.BlockSpec((tm, d), lambda i: (i, 0)),
        out_shape=jax.ShapeDtypeStruct((t, d), F32),
        compiler_params=_params("parallel"),
        name="odd_out_proj",
    )(att, p, p, p, p, p, p, p, p, p, cw, cbias, w, h, fg)


def _rope_tables(seq):
    t = jnp.arange(seq)
    pos = jnp.stack([t // GRID_W, t % GRID_W], axis=-1).astype(F32)
    inv = ROPE_THETA ** (-jnp.arange(ROPE_HALF, dtype=F32) / ROPE_HALF)
    ang = pos[:, :, None] * inv
    cos, sin = jnp.cos(ang), jnp.sin(ang)
    cos_l = jnp.stack([cos, cos], axis=1).reshape(seq, GQ_HEAD_DIM)
    sin_l = jnp.stack([-sin, sin], axis=1).reshape(seq, GQ_HEAD_DIM)
    return cos_l, sin_l


def _half_swap(a):
    lead = a.shape[:-1]
    a = a.reshape(*lead, -1, 2, GQ_HEAD_DIM // 2)
    return a[..., ::-1, :].reshape(*lead, -1)


def _rope_dim_order(a):
    lead = a.shape[:-1]
    a = a.reshape(*lead, -1, 2, 2, ROPE_HALF)
    return jnp.swapaxes(a, -3, -2).reshape(*lead, -1)


def kernel(x, norm_g, final_g, ev_w_in, ev_gate_b, ev_w_out, ev_ml_norm_g, ev_na_rpb,
           od_w_in, od_w_out, od_q_norm_g, od_k_norm_g, od_conv_w, od_conv_b):
    batch, seq, d = x.shape
    assert d == D_MODEL and norm_g.shape[0] == 2
    assert seq % max(ML_CHUNK, GQ_BLOCK_K, GQ_BLOCK_Q, NA_ROWS_PER_STEP * GRID_W, 512) == 0
    t = batch * seq
    xf = x.reshape(t, d)
    tm_in = min(2048, seq)
    tm_out = 512

    n_gate = 4 * ML_HEADS
    g0 = 5 * D_MODEL
    w_in = ev_w_in[0].astype(BF16)
    w_gate = jnp.pad(w_in[:, g0:g0 + n_gate], ((0, 0), (0, LANES - n_gate)))
    tn_even = 1024
    p_ev, gates = _even_in_proj(xf, norm_g[0][None, :], w_in, w_in[:, g0 + n_gate:], w_gate,
                                na=g0 // tn_even, tm=tm_in, tn=tn_even)
    gate_bias = jnp.pad(ev_gate_b[0], (0, LANES - n_gate))[None, :]
    hf, hb = _mlstm(p_ev, gates, gate_bias, batch=batch, seq=seq)
    na = _natten(p_ev, _na_bias_table(ev_na_rpb[0]), batch=batch, seq=seq, col0=g0)
    h1 = _even_out(hf, hb, p_ev, na, ev_ml_norm_g[0][None, :], ev_w_out[0].astype(BF16), xf, tm=tm_out)

    w_in = od_w_in[0]
    qw, kvw = GQ_HEADS * GQ_HEAD_DIM, GQ_KV_HEADS * GQ_HEAD_DIM
    tn_odd = 512
    wq, wk = _rope_dim_order(w_in[:, :qw]), _rope_dim_order(w_in[:, qw:qw + kvw])
    w_qkv = jnp.concatenate([wq, wk, w_in[:, qw + kvw:qw + 2 * kvw]], axis=1).astype(BF16)
    w_swap = jnp.concatenate([_half_swap(wq), _half_swap(wk), jnp.zeros((d, tn_odd - kvw), w_in.dtype)],
                             axis=1).astype(BF16)
    cos, sin = _rope_tables(seq)
    qg = jnp.tile(_rope_dim_order(od_q_norm_g[0]), tn_odd // GQ_HEAD_DIM)[None, :]
    kg = jnp.tile(_rope_dim_order(od_k_norm_g[0]), GQ_KV_HEADS)[None, :]
    plain0 = (qw + 2 * kvw) // tn_odd
    p_od = _odd_in_proj(h1, norm_g[1][None, :], w_qkv, w_swap, w_in.astype(BF16), qg, _half_swap(qg),
                        kg, _half_swap(kg), cos, sin, plain0=plain0, n_plain=w_in.shape[1] // tn_odd - plain0,
                        tm=tm_in, tn=tn_odd, seq=seq)
    k_col = p_od.shape[1] - 2 * kvw
    att = _gqa(p_od, batch=batch, seq=seq, k_col=k_col, v_col=k_col + kvw)
    out = _odd_out(att, p_od, od_conv_w[0], od_conv_b[0][None, :], od_w_out[0].astype(BF16), h1,
                   final_g[None, :], tm=tm_out, seq=seq, cols=(1, 2, 3, 4, 5))
    return out.reshape(batch, seq, d)
```

```python
import functools

import jax
import jax.numpy as jnp
from jax import lax
from jax.experimental import pallas as pl
from jax.experimental.pallas import tpu as pltpu

F32 = jnp.float32
BF16 = jnp.bfloat16

D_MODEL = 1024
GRID_W = 64
RMS_EPS = 1e-6

ML_HEADS = 4
ML_HEAD_DIM = 256
ML_CHUNK = 256

NA_HEADS = 16
NA_HEAD_DIM = 64
NA_KH = 8
NA_KW = 16
NA_ROWS_PER_STEP = 64

GQ_HEADS = 8
GQ_HEAD_DIM = 128
GQ_KV_HEADS = 2
GQ_GROUP = GQ_HEADS // GQ_KV_HEADS
ROPE_THETA = 10000.0
ROPE_HALF = GQ_HEAD_DIM // 4
GQ_BLOCK_Q = 256
GQ_BLOCK_K = 512

CV_K = 3

LANES = 128
SUBLANES = 8
MASK_VALUE = -1e30
LOG2_E = 1.4426950408889634

VMEM_LIMIT = 56 * 1024 * 1024


def _params(*sem):
    return pltpu.CompilerParams(dimension_semantics=sem, vmem_limit_bytes=VMEM_LIMIT)


def _sigmoid(x):
    return 0.5 * jnp.tanh(0.5 * x) + 0.5


def _silu(x):
    return x * _sigmoid(x)


def _log_sigmoid(x):
    return jnp.minimum(x, 0.0) - jnp.log1p(jnp.exp(-jnp.abs(x)))


def _rms(x, eps=RMS_EPS):
    return x * lax.rsqrt(jnp.mean(x * x, axis=-1, keepdims=True) + eps)


def _dot(a, b):
    return jnp.dot(a, b, preferred_element_type=F32)


def _dot_nt(a, b):
    return lax.dot_general(a, b, (((1,), (1,)), ((), ())), preferred_element_type=F32)


def _dot_tn(a, b):
    return lax.dot_general(a, b, (((0,), (0,)), ((), ())), preferred_element_type=F32)


def _even_in_kernel(x_ref, g_ref, wa_ref, wb_ref, wg_ref, o_ref, og_ref, u_ref, *, n_a_tiles):
    j = pl.program_id(1)

    @pl.when(j == 0)
    def _():
        u = (_rms(x_ref[...]) * g_ref[...]).astype(BF16)
        u_ref[...] = u
        og_ref[...] = _dot(u, wg_ref[...])

    @pl.when(j < n_a_tiles)
    def _():
        o_ref[...] = _dot(u_ref[...], wa_ref[...]).astype(o_ref.dtype)

    @pl.when(j >= n_a_tiles)
    def _():
        o_ref[...] = _dot(u_ref[...], wb_ref[...]).astype(o_ref.dtype)


def _even_in_proj(x, g, wa, wb, wg, *, na, tm, tn):
    t, d = x.shape
    nb = wb.shape[1] // tn
    n = (na + nb) * tn
    kern = functools.partial(_even_in_kernel, n_a_tiles=na)
    return pl.pallas_call(
        kern,
        grid=(t // tm, na + nb),
        in_specs=[
            pl.BlockSpec((tm, d), lambda i, j: (i, 0)),
            pl.BlockSpec((1, d), lambda i, j: (0, 0)),
            pl.BlockSpec((d, tn), lambda i, j: (0, jnp.minimum(j, na - 1))),
            pl.BlockSpec((d, tn), lambda i, j: (0, jnp.maximum(j - na, 0))),
            pl.BlockSpec((d, LANES), lambda i, j: (0, 0)),
        ],
        out_specs=[
            pl.BlockSpec((tm, tn), lambda i, j: (i, j)),
            pl.BlockSpec((tm, LANES), lambda i, j: (i, 0)),
        ],
        out_shape=[
            jax.ShapeDtypeStruct((t, n), BF16),
            jax.ShapeDtypeStruct((t, LANES), F32),
        ],
        scratch_shapes=[pltpu.VMEM((tm, d), BF16)],
        compiler_params=_params("parallel", "arbitrary"),
        name="even_in_proj",
    )(x, g, wa, wb, wg)


def _rope_norm(acc, acc_sw, gain, gain_sw, cos, sin, scale):
    nh = acc.shape[1] // GQ_HEAD_DIM
    heads = []
    for h in range(nh):
        sl = slice(h * GQ_HEAD_DIM, (h + 1) * GQ_HEAD_DIM)
        a = acc[:, sl]
        r = lax.rsqrt(jnp.mean(a * a, axis=-1, keepdims=True) + RMS_EPS)
        y = a * r * gain[:, sl]
        partner = acc_sw[:, sl] * r * gain_sw[:, sl]
        heads.append((y * cos + partner * sin) * scale)
    return jnp.concatenate(heads, axis=-1)


def _odd_in_kernel(x_ref, g_ref, wa_ref, wsw_ref, wb_ref, qg_ref, qgs_ref, kg_ref, kgs_ref, cos_ref, sin_ref,
                   o_ref, u_ref, *, n_q_tiles, kv_tile):
    j = pl.program_id(1)

    @pl.when(j == 0)
    def _():
        u_ref[...] = (_rms(x_ref[...]) * g_ref[...]).astype(BF16)

    is_q = j < n_q_tiles
    is_kv = j == kv_tile

    @pl.when(is_q)
    def _():
        u = u_ref[...]
        o_ref[...] = _rope_norm(_dot(u, wa_ref[...]), _dot(u, wsw_ref[...]), qg_ref[...], qgs_ref[...],
                                cos_ref[...], sin_ref[...], GQ_HEAD_DIM ** -0.5 * LOG2_E).astype(o_ref.dtype)

    @pl.when(is_kv)
    def _():
        kw = GQ_KV_HEADS * GQ_HEAD_DIM
        u = u_ref[...]
        acc = _dot(u, wa_ref[...])
        k = _rope_norm(acc[:, :kw], _dot(u, wsw_ref[:, :kw]), kg_ref[...], kgs_ref[...],
                       cos_ref[...], sin_ref[...], 1.0)
        o_ref[...] = jnp.concatenate([k, acc[:, kw:]], axis=-1).astype(o_ref.dtype)

    @pl.when(jnp.logical_not(jnp.logical_or(is_q, is_kv)))
    def _():
        o_ref[...] = _dot(u_ref[...], wb_ref[...]).astype(o_ref.dtype)


def _odd_in_proj(x, g, wa, wsw, wb, qg, qgs, kg, kgs, cos, sin, *, plain0, n_plain, tm, tn, seq):
    t, d = x.shape
    n_q_tiles = (GQ_HEADS * GQ_HEAD_DIM) // tn
    kv_tile = n_q_tiles + n_plain
    n = (kv_tile + 1) * tn
    kvw = GQ_KV_HEADS * GQ_HEAD_DIM
    sb = seq // tm
    kern = functools.partial(_odd_in_kernel, n_q_tiles=n_q_tiles, kv_tile=kv_tile)

    def qkv_map(i, j):
        return (0, jnp.where(j < n_q_tiles, j, jnp.where(j == kv_tile, n_q_tiles, n_q_tiles - 1)))

    return pl.pallas_call(
        kern,
        grid=(t // tm, kv_tile + 1),
        in_specs=[
            pl.BlockSpec((tm, d), lambda i, j: (i, 0)),
            pl.BlockSpec((1, d), lambda i, j: (0, 0)),
            pl.BlockSpec((d, tn), qkv_map),
            pl.BlockSpec((d, tn), qkv_map),
            pl.BlockSpec((d, tn), lambda i, j: (0, plain0 + jnp.clip(j - n_q_tiles, 0, n_plain - 1))),
            pl.BlockSpec((1, tn), lambda i, j: (0, 0)),
            pl.BlockSpec((1, tn), lambda i, j: (0, 0)),
            pl.BlockSpec((1, kvw), lambda i, j: (0, 0)),
            pl.BlockSpec((1, kvw), lambda i, j: (0, 0)),
            pl.BlockSpec((tm, GQ_HEAD_DIM), lambda i, j: (i % sb, 0)),
            pl.BlockSpec((tm, GQ_HEAD_DIM), lambda i, j: (i % sb, 0)),
        ],
        out_specs=pl.BlockSpec((tm, tn), lambda i, j: (i, j)),
        out_shape=jax.ShapeDtypeStruct((t, n), BF16),
        scratch_shapes=[pltpu.VMEM((tm, d), BF16)],
        compiler_params=_params("parallel", "arbitrary"),
        name="odd_in_proj",
    )(x, g, wa, wsw, wb, qg, qgs, kg, kgs, cos, sin)


def _split3(a):
    hi = a.astype(BF16)
    r = a - hi.astype(F32)
    mid = r.astype(BF16)
    return hi, mid, (r - mid.astype(F32)).astype(BF16)


def _dot_exact_lhs(a, b):
    return sum(_dot(p, b) for p in _split3(a))


def _mlstm_kernel(qf_ref, kf_ref, vf_ref, gf_ref, qb_ref, kb_ref, vb_ref, gb_ref, bias_ref, sel_ref,
                  hf_ref, hb_ref, c_ref, n_ref, m_ref):
    nb, L = qf_ref.shape[:2]
    dh = ML_HEAD_DIM
    k_scale = dh ** -0.5

    @pl.when(pl.program_id(0) == 0)
    def _():
        c_ref[...] = jnp.zeros_like(c_ref)
        n_ref[...] = jnp.zeros_like(n_ref)
        m_ref[...] = jnp.zeros_like(m_ref)

    row = lax.broadcasted_iota(jnp.int32, (L, L), 0)
    col = lax.broadcasted_iota(jnp.int32, (L, L), 1)
    lower = row >= col
    upper = row <= col
    n_gate = 4 * ML_HEADS

    def wide(x):
        return jnp.concatenate([x] * (dh // LANES), axis=1)

    gate_row = lax.broadcasted_iota(jnp.int32, (n_gate, L), 0)
    forget_row = ((gate_row >= ML_HEADS) & (gate_row < 2 * ML_HEADS)) | (gate_row >= 3 * ML_HEADS)

    def gate_tables(g, reach, first_col):
        gt = (g + bias_ref[...]).T[:n_gate, :]
        lf = jnp.where(forget_row, _log_sigmoid(gt), 0.0)
        cum = _dot_exact_lhs(lf, reach.astype(BF16))
        tab_t = jnp.where(forget_row, cum, gt) * LOG2_E
        sel = sel_ref[:, first_col * LANES:(first_col + 2 * ML_HEADS) * LANES]
        rep = sum(_dot_tn(p, sel) for p in _split3(tab_t))
        return rep, tab_t

    def lanes(rep, j):
        return rep[:, j * LANES:(j + 1) * LANES]

    chains = []
    for b in range(nb):
        rep_f, tab_ft = gate_tables(gf_ref[b], upper, 0)
        rep_b, tab_bt = gate_tables(gb_ref[b], lower, 2 * ML_HEADS)
        fwd_refs = (qf_ref.at[b], kf_ref.at[b], vf_ref.at[b], hf_ref.at[b])
        bwd_refs = (qb_ref.at[b], kb_ref.at[b], vb_ref.at[b], hb_ref.at[b])
        for hd in range(ML_HEADS):
            base = 2 * ML_HEADS * b
            chains.append((base + hd, fwd_refs, rep_f, tab_ft, 0, L - 1, lower, hd))
            chains.append((base + ML_HEADS + hd, bwd_refs, rep_b, tab_bt, 2 * ML_HEADS, 0, upper, hd))

    stage1 = []
    for idx, (q_ref, k_ref, _, _), _, _, _, _, _, hd in chains:
        sl = slice(hd * dh, (hd + 1) * dh)
        q = q_ref[:, sl]
        qk = _dot_nt(q * jnp.asarray(k_scale, q.dtype), k_ref[:, sl])
        stage1.append((qk, _dot(q, c_ref[idx].astype(BF16))))

    for (idx, (q_ref, k_ref, v_ref, h_ref), rep, tab_t, go, g_row, valid, hd), (qk, qc) in zip(chains, stage1):
        sl = slice(hd * dh, (hd + 1) * dh)
        fi, ff = hd, ML_HEADS + hd
        q, k, v = q_ref[:, sl], k_ref[:, sl], v_ref[:, sl]
        b128, ig128 = lanes(rep, ff), lanes(rep, fi)
        brow, igrow = tab_t[go + ff:go + ff + 1, :], tab_t[go + fi:go + fi + 1, :]
        g128 = b128[g_row:g_row + 1, :]
        n = n_ref[idx][0:1, :]
        m_prev = m_ref[idx][0:1, :]
        d = jnp.where(valid, wide(b128) + (igrow - brow), MASK_VALUE)
        inter = b128 + m_prev
        m_t = jnp.maximum(inter, jnp.max(d, axis=-1, keepdims=True))
        w = jnp.exp2(d - wide(m_t)) * qk
        decay = jnp.exp2(inter - m_t)
        num = wide(decay) * qc + _dot(w.astype(BF16), v)
        qn = jnp.sum(q.astype(F32) * n, axis=-1, keepdims=True)
        den = decay * qn + jnp.sum(w, axis=-1, keepdims=True)
        h = num * wide(1.0 / jnp.maximum(jnp.abs(den), jnp.exp2(-m_t)))
        h_ref[:, sl] = h.astype(h_ref.dtype)
        a = g128 - b128 + ig128
        m_new = jnp.maximum(g128 + m_prev, jnp.max(a, axis=0, keepdims=True))
        carry = wide(jnp.exp2(g128 + m_prev - m_new))
        wk = wide(jnp.exp2(a - m_new) * k_scale)
        wkv = (wk * v.astype(F32)).astype(BF16)
        c_ref[idx] = carry * c_ref[idx] + _dot_tn(k, wkv)
        n_new = carry * n + jnp.sum(wk * k.astype(F32), axis=0, keepdims=True)
        n_ref[idx] = jnp.broadcast_to(n_new, n_ref.shape[1:])
        m_ref[idx] = jnp.broadcast_to(m_new, m_ref.shape[1:])


def _mlstm(p, gates, bias, *, batch, seq):
    t = p.shape[0]
    L = ML_CHUNK
    nc = seq // L
    w = ML_HEADS * ML_HEAD_DIM
    n_gate = 4 * ML_HEADS
    sel = (jnp.arange(n_gate)[:, None] == (jnp.arange(n_gate * LANES) // LANES)[None, :]).astype(BF16)
    p3 = p.reshape(batch, seq, p.shape[1])
    g3 = gates.reshape(batch, seq, LANES)

    def fwd(cb):
        return lambda c: (0, c, cb)

    def bwd(cb):
        return lambda c: (0, nc - 1 - c, cb)

    def blk(width, index_map):
        return pl.BlockSpec((batch, L, width), index_map)

    hf, hb = pl.pallas_call(
        _mlstm_kernel,
        grid=(nc,),
        in_specs=[
            blk(w, fwd(0)), blk(w, fwd(1)), blk(w, fwd(2)), blk(LANES, fwd(0)),
            blk(w, bwd(0)), blk(w, bwd(1)), blk(w, bwd(2)), blk(LANES, bwd(0)),
            pl.BlockSpec((1, LANES), lambda c: (0, 0)),
            pl.BlockSpec(sel.shape, lambda c: (0, 0)),
        ],
        out_specs=[blk(w, fwd(0)), blk(w, bwd(0))],
        out_shape=[jax.ShapeDtypeStruct((batch, seq, w), BF16), jax.ShapeDtypeStruct((batch, seq, w), BF16)],
        scratch_shapes=[
            pltpu.VMEM((batch * 2 * ML_HEADS, ML_HEAD_DIM, ML_HEAD_DIM), F32),
            pltpu.VMEM((batch * 2 * ML_HEADS, SUBLANES, ML_HEAD_DIM), F32),
            pltpu.VMEM((batch * 2 * ML_HEADS, SUBLANES, LANES), F32),
        ],
        compiler_params=_params("arbitrary"),
        name="mlstm",
    )(p3, p3, p3, g3, p3, p3, p3, g3, bias, sel)
    return hf.reshape(t, w), hb.reshape(t, w)


def _na_table_kernel(rp_ref, o_ref):
    c = lax.broadcasted_iota(jnp.int32, (GRID_W, LANES), 0)
    lane = lax.broadcasted_iota(jnp.int32, (GRID_W, LANES), 1)
    c0 = jnp.clip(c - NA_KW // 2, 0, GRID_W - NA_KW)
    low_half = lane < GRID_W
    kc = jnp.where(low_half, lane, lane - GRID_W)
    in_window = (kc >= c0) & (kc < c0 + NA_KW)
    lo, hi = [], []
    for dr in range(2 * NA_KH - 1):
        row = jnp.broadcast_to(rp_ref[0, dr:dr + 1, :], (GRID_W, LANES))
        lo.append(jnp.where(in_window & low_half, pltpu.roll(row, GRID_W + 1, 1, stride=1, stride_axis=0),
                            MASK_VALUE))
        hi.append(jnp.where(in_window & jnp.logical_not(low_half),
                            pltpu.roll(row, 1, 1, stride=1, stride_axis=0), MASK_VALUE))
    for rel in range(NA_KH):
        pairs = [jnp.maximum(lo[2 * j - rel + NA_KH - 1], hi[2 * j - rel + NA_KH]) for j in range(NA_KH // 2)]
        o_ref[0, rel] = jnp.concatenate(pairs, axis=1)


def _na_bias_table(rpb):
    nh, ndr, ndc = rpb.shape
    lead = GRID_W - NA_KW
    rp = jnp.pad(rpb.astype(F32) * LOG2_E, ((0, 0), (0, 2 * NA_KH - ndr), (lead, LANES - lead - ndc)))
    return pl.pallas_call(
        _na_table_kernel,
        grid=(nh,),
        in_specs=[pl.BlockSpec((1, 2 * NA_KH, LANES), lambda h: (h, 0, 0))],
        out_specs=pl.BlockSpec((1, NA_KH, GRID_W, NA_KH * GRID_W), lambda h: (h, 0, 0, 0)),
        out_shape=jax.ShapeDtypeStruct((nh, NA_KH, GRID_W, NA_KH * GRID_W), F32),
        compiler_params=_params("parallel"),
        name="na_bias_table",
    )(rp)


def _na_kernel(q_ref, k_ref, v_ref, z_ref, tbl_ref, o_ref, *, rows):
    rb = pl.program_id(2)
    win = NA_KH * GRID_W
    lane_q = lax.broadcasted_iota(jnp.int32, (GRID_W, LANES), 1)
    head0_q = lane_q < NA_HEAD_DIM
    starts, scores = [], []
    for i in range(NA_ROWS_PER_STEP):
        r = rb * NA_ROWS_PER_STEP + i
        r0 = jnp.clip(r - NA_KH // 2, 0, rows - NA_KH)
        rel = r - r0
        start = pl.multiple_of(r0 * GRID_W, GRID_W)
        kw = k_ref[pl.ds(start, win), :]
        q = q_ref[i * GRID_W:(i + 1) * GRID_W, :]
        zero = jnp.zeros_like(q)
        qs = jnp.concatenate([jnp.where(head0_q, q, zero), jnp.where(head0_q, zero, q)], axis=0)
        bias = jnp.concatenate([tbl_ref[0, rel], tbl_ref[1, rel]], axis=0)
        scores.append(_dot_nt(qs, kw) * (NA_HEAD_DIM ** -0.5 * LOG2_E) + bias)
        starts.append(start)
    for i in range(NA_ROWS_PER_STEP):
        s = scores[i]
        e = jnp.exp2(s - jnp.max(s, axis=-1, keepdims=True))
        inv = 1.0 / jnp.sum(e, axis=-1, keepdims=True)
        o = _dot(e.astype(BF16), v_ref[pl.ds(starts[i], win), :]) * inv
        out = jnp.where(head0_q, o[:GRID_W], o[GRID_W:])
        zz = z_ref[i * GRID_W:(i + 1) * GRID_W, :].astype(F32)
        o_ref[i * GRID_W:(i + 1) * GRID_W, :] = (out * _silu(zz)).astype(o_ref.dtype)


def _natten(p, tbl, *, batch, seq, col0):
    t = p.shape[0]
    rows = seq // GRID_W
    width = NA_HEADS * NA_HEAD_DIM
    pairs = width // LANES
    tq = NA_ROWS_PER_STEP * GRID_W
    nrb = seq // tq
    cb = col0 // LANES
    kern = functools.partial(_na_kernel, rows=rows)
    return pl.pallas_call(
        kern,
        grid=(batch, pairs, nrb),
        in_specs=[
            pl.BlockSpec((tq, LANES), lambda b, h, r: (b * nrb + r, cb + h)),
            pl.BlockSpec((seq, LANES), lambda b, h, r: (b, cb + pairs + h)),
            pl.BlockSpec((seq, LANES), lambda b, h, r: (b, cb + 2 * pairs + h)),
            pl.BlockSpec((tq, LANES), lambda b, h, r: (b * nrb + r, cb + 3 * pairs + h)),
            pl.BlockSpec((2, NA_KH, GRID_W, NA_KH * GRID_W), lambda b, h, r: (h, 0, 0, 0)),
        ],
        out_specs=pl.BlockSpec((tq, LANES), lambda b, h, r: (b * nrb + r, h)),
        out_shape=jax.ShapeDtypeStruct((t, width), BF16),
        compiler_params=_params("parallel", "parallel", "arbitrary"),
        name="natten",
    )(p, p, p, p, tbl)


def _even_out_kernel(hf_ref, hb_ref, o_ref, z_ref, na_ref, g_ref, w_ref, x_ref, out_ref):
    wa = ML_HEADS * ML_HEAD_DIM
    y = _dot(na_ref[...], w_ref[wa:, :])
    for i in range(ML_HEADS):
        sl = slice(i * ML_HEAD_DIM, (i + 1) * ML_HEAD_DIM)
        h = hf_ref[:, sl].astype(F32) + hb_ref[:, sl].astype(F32)
        a = _rms(h) * g_ref[:, sl] * _sigmoid(o_ref[:, sl].astype(F32)) * _silu(z_ref[:, sl].astype(F32))
        y = y + _dot(a.astype(BF16), w_ref[sl, :])
    out_ref[...] = x_ref[...] + y


def _even_out(hf, hb, p, na, g, w, x, *, tm):
    t, d = x.shape
    wa = hf.shape[1]
    return pl.pallas_call(
        _even_out_kernel,
        grid=(t // tm,),
        in_specs=[
            pl.BlockSpec((tm, wa), lambda i: (i, 0)),
            pl.BlockSpec((tm, wa), lambda i: (i, 0)),
            pl.BlockSpec((tm, wa), lambda i: (i, 3)),
            pl.BlockSpec((tm, wa), lambda i: (i, 4)),
            pl.BlockSpec((tm, na.shape[1]), lambda i: (i, 0)),
            pl.BlockSpec((1, wa), lambda i: (0, 0)),
            pl.BlockSpec(w.shape, lambda i: (0, 0)),
            pl.BlockSpec((tm, d), lambda i: (i, 0)),
        ],
        out_specs=pl.BlockSpec((tm, d), lambda i: (i, 0)),
        out_shape=jax.ShapeDtypeStruct((t, d), F32),
        compiler_params=_params("parallel"),
        name="even_out_proj",
    )(hf, hb, p, p, na, g, w, x)


def _gqa_kernel(q_ref, k_ref, v_ref, z_ref, o_ref, vaug_ref, m_ref, acc_ref, *, tk):
    tq = q_ref.shape[0]
    seq = k_ref.shape[0]
    dh = GQ_HEAD_DIM

    @pl.when(pl.program_id(2) == 0)
    def _():
        vaug_ref[:, :dh] = v_ref[...]
        vaug_ref[:, dh:] = jnp.ones((seq, dh), BF16)

    q = q_ref[...]
    qs = jnp.concatenate([q[:, g * dh:(g + 1) * dh] for g in range(GQ_GROUP)], axis=0)
    m_ref[...] = jnp.full_like(m_ref, -jnp.inf)
    acc_ref[...] = jnp.zeros_like(acc_ref)

    def scores(c):
        return _dot_nt(qs, k_ref[c * tk:(c + 1) * tk, :])

    def update(s, c):
        m_prev = m_ref[...]
        m_new = jnp.maximum(m_prev, jnp.max(s, axis=-1, keepdims=True))
        alpha = jnp.exp2(m_prev - m_new)
        p = jnp.exp2(s - jnp.concatenate([m_new] * (tk // LANES), axis=1))
        pv = _dot(p.astype(BF16), vaug_ref[c * tk:(c + 1) * tk, :])
        acc_ref[...] = jnp.concatenate([alpha, alpha], axis=1) * acc_ref[...] + pv
        m_ref[...] = m_new

    nk = seq // tk
    s_next = scores(0)
    for c in range(nk):
        s_cur = s_next
        if c + 1 < nk:
            s_next = scores(c + 1)
        update(s_cur, c)
    acc = acc_ref[...]
    out = acc[:, :dh] / acc[:, dh:]
    heads = jnp.concatenate([out[g * tq:(g + 1) * tq] for g in range(GQ_GROUP)], axis=-1)
    o_ref[...] = (heads * _silu(z_ref[...].astype(F32))).astype(o_ref.dtype)


def _gqa(p, *, batch, seq, k_col, v_col, z_col):
    t = p.shape[0]
    tq, tk = GQ_BLOCK_Q, GQ_BLOCK_K
    nq = seq // tq
    gw = GQ_GROUP * GQ_HEAD_DIM
    kb, vb, zb = k_col // GQ_HEAD_DIM, v_col // GQ_HEAD_DIM, z_col // gw
    kern = functools.partial(_gqa_kernel, tk=tk)
    return pl.pallas_call(
        kern,
        grid=(batch, GQ_KV_HEADS, nq),
        in_specs=[
            pl.BlockSpec((tq, gw), lambda b, h, i: (b * nq + i, h)),
            pl.BlockSpec((seq, GQ_HEAD_DIM), lambda b, h, i: (b, kb + h)),
            pl.BlockSpec((seq, GQ_HEAD_DIM), lambda b, h, i: (b, vb + h)),
            pl.BlockSpec((tq, gw), lambda b, h, i: (b * nq + i, zb + h)),
        ],
        out_specs=pl.BlockSpec((tq, gw), lambda b, h, i: (b * nq + i, h)),
        out_shape=jax.ShapeDtypeStruct((t, GQ_HEADS * GQ_HEAD_DIM), BF16),
        scratch_shapes=[
            pltpu.VMEM((seq, 2 * GQ_HEAD_DIM), BF16),
            pltpu.VMEM((GQ_GROUP * tq, LANES), F32),
            pltpu.VMEM((GQ_GROUP * tq, 2 * GQ_HEAD_DIM), F32),
        ],
        compiler_params=_params("parallel", "parallel", "arbitrary"),
        name="gqa_attention",
    )(p, p, p, p)


def _odd_out_kernel(att_ref, cb_ref, cc_ref, cx_ref, cz_ref, ccp_ref, cxp_ref, ccn_ref, cxn_ref,
                    cw_ref, cbias_ref, w_ref, h_ref, fg_ref, out_ref, *, blocks_per_seq):
    tm, wa = att_ref.shape
    slab = 2 * LANES
    i = pl.program_id(0) % blocks_per_seq
    ridx = lax.broadcasted_iota(jnp.int32, (tm, slab), 0)
    hh = h_ref[...]
    hh = hh + _dot(att_ref[...], w_ref[:wa, :])
    for s in range(wa // slab):
        sl = slice(s * slab, (s + 1) * slab)
        xc = cc_ref[:, sl].astype(F32) * cx_ref[:, sl].astype(F32)
        prev_row = ccp_ref[SUBLANES - 1:SUBLANES, sl].astype(F32) * cxp_ref[SUBLANES - 1:SUBLANES, sl].astype(F32)
        next_row = ccn_ref[0:1, sl].astype(F32) * cxn_ref[0:1, sl].astype(F32)
        prev_row = jnp.where(i == 0, 0.0, prev_row)
        next_row = jnp.where(i == blocks_per_seq - 1, 0.0, next_row)
        x_prev = jnp.where(ridx == 0, prev_row, pltpu.roll(xc, 1, 0))
        x_next = jnp.where(ridx == tm - 1, next_row, pltpu.roll(xc, tm - 1, 0))
        y = x_prev * cw_ref[0:1, sl] + xc * cw_ref[1:2, sl] + x_next * cw_ref[2:3, sl] + cbias_ref[:, sl]
        conv = cb_ref[:, sl].astype(F32) * y * _silu(cz_ref[:, sl].astype(F32))
        hh = hh + _dot(conv.astype(BF16), w_ref[wa + s * slab:wa + (s + 1) * slab, :])
    out_ref[...] = _rms(hh) * fg_ref[...]


def _odd_out(att, p, cw, cbias, w, h, fg, *, tm, seq, cols):
    t, d = h.shape
    bps = seq // tm
    sub = tm // SUBLANES
    nsub = t // SUBLANES
    b_c, c_c, x_c, cz_c = cols

    def blk(cb):
        return pl.BlockSpec((tm, d), lambda i: (i, cb))

    def halo_prev(cb):
        return pl.BlockSpec((SUBLANES, d), lambda i: (jnp.maximum(i * sub - 1, 0), cb))

    def halo_next(cb):
        return pl.BlockSpec((SUBLANES, d), lambda i: (jnp.minimum((i + 1) * sub, nsub - 1), cb))

    kern = functools.partial(_odd_out_kernel, blocks_per_seq=bps)
    return pl.pallas_call(
        kern,
        grid=(t // tm,),
        in_specs=[
            pl.BlockSpec((tm, d), lambda i: (i, 0)),
            blk(b_c), blk(c_c), blk(x_c), blk(cz_c),
            halo_prev(c_c), halo_prev(x_c), halo_next(c_c), halo_next(x_c),
            pl.BlockSpec(cw.shape, lambda i: (0, 0)),
            pl.BlockSpec((1, d), lambda i: (0, 0)),
            pl.BlockSpec(w.shape, lambda i: (0, 0)),
            pl.BlockSpec((tm, d), lambda i: (i, 0)),
            pl.BlockSpec((1, d), lambda i: (0, 0)),
        ],
        out_specs=pl.BlockSpec((tm, d), lambda i: (i, 0)),
        out_shape=jax.ShapeDtypeStruct((t, d), F32),
        compiler_params=_params("parallel"),
        name="odd_out_proj",
    )(att, p, p, p, p, p, p, p, p, cw, cbias, w, h, fg)


def _rope_tables(seq):
    t = jnp.arange(seq)
    pos = jnp.stack([t // GRID_W, t % GRID_W], axis=-1).astype(F32)
    inv = ROPE_THETA ** (-jnp.arange(ROPE_HALF, dtype=F32) / ROPE_HALF)
    ang = pos[:, :, None] * inv
    cos, sin = jnp.cos(ang), jnp.sin(ang)
    cos_l = jnp.stack([cos, cos], axis=1).reshape(seq, GQ_HEAD_DIM)
    sin_l = jnp.stack([-sin, sin], axis=1).reshape(seq, GQ_HEAD_DIM)
    return cos_l, sin_l


def _half_swap(a):
    lead = a.shape[:-1]
    a = a.reshape(*lead, -1, 2, GQ_HEAD_DIM // 2)
    return a[..., ::-1, :].reshape(*lead, -1)


def _rope_dim_order(a):
    lead = a.shape[:-1]
    a = a.reshape(*lead, -1, 2, 2, ROPE_HALF)
    return jnp.swapaxes(a, -3, -2).reshape(*lead, -1)


def kernel(x, norm_g, final_g, ev_w_in, ev_gate_b, ev_w_out, ev_ml_norm_g, ev_na_rpb,
           od_w_in, od_w_out, od_q_norm_g, od_k_norm_g, od_conv_w, od_conv_b):
    batch, seq, d = x.shape
    assert d == D_MODEL and norm_g.shape[0] == 2
    assert seq % max(ML_CHUNK, GQ_BLOCK_K, GQ_BLOCK_Q, NA_ROWS_PER_STEP * GRID_W, 512) == 0
    t = batch * seq
    xf = x.reshape(t, d)
    tm_in = min(2048, seq)
    tm_out = 512

    n_gate = 4 * ML_HEADS
    g0 = 5 * D_MODEL
    w_in = ev_w_in[0].astype(BF16)
    w_gate = jnp.pad(w_in[:, g0:g0 + n_gate], ((0, 0), (0, LANES - n_gate)))
    tn_even = 1024
    p_ev, gates = _even_in_proj(xf, norm_g[0][None, :], w_in, w_in[:, g0 + n_gate:], w_gate,
                                na=g0 // tn_even, tm=tm_in, tn=tn_even)
    gate_bias = jnp.pad(ev_gate_b[0], (0, LANES - n_gate))[None, :]
    hf, hb = _mlstm(p_ev, gates, gate_bias, batch=batch, seq=seq)
    na = _natten(p_ev, _na_bias_table(ev_na_rpb[0]), batch=batch, seq=seq, col0=g0)
    h1 = _even_out(hf, hb, p_ev, na, ev_ml_norm_g[0][None, :], ev_w_out[0].astype(BF16), xf, tm=tm_out)

    w_in = od_w_in[0]
    qw, kvw = GQ_HEADS * GQ_HEAD_DIM, GQ_KV_HEADS * GQ_HEAD_DIM
    tn_odd = 512
    wq, wk = _rope_dim_order(w_in[:, :qw]), _rope_dim_order(w_in[:, qw:qw + kvw])
    w_qkv = jnp.concatenate([wq, wk, w_in[:, qw + kvw:qw + 2 * kvw]], axis=1).astype(BF16)
    w_swap = jnp.concatenate([_half_swap(wq), _half_swap(wk), jnp.zeros((d, tn_odd - kvw), w_in.dtype)],
                             axis=1).astype(BF16)
    cos, sin = _rope_tables(seq)
    qg = jnp.tile(_rope_dim_order(od_q_norm_g[0]), tn_odd // GQ_HEAD_DIM)[None, :]
    kg = jnp.tile(_rope_dim_order(od_k_norm_g[0]), GQ_KV_HEADS)[None, :]
    plain0 = (qw + 2 * kvw) // tn_odd
    p_od = _odd_in_proj(h1, norm_g[1][None, :], w_qkv, w_swap, w_in.astype(BF16), qg, _half_swap(qg),
                        kg, _half_swap(kg), cos, sin, plain0=plain0, n_plain=w_in.shape[1] // tn_odd - plain0,
                        tm=tm_in, tn=tn_odd, seq=seq)
    k_col = p_od.shape[1] - 2 * kvw
    att = _gqa(p_od, batch=batch, seq=seq, k_col=k_col, v_col=k_col + kvw, z_col=qw)
    out = _odd_out(att, p_od, od_conv_w[0], od_conv_b[0][None, :], od_w_out[0].astype(BF16), h1,
                   final_g[None, :], tm=tm_out, seq=seq, cols=(2, 3, 4, 5))
    return out.reshape(batch, seq, d)
```

```python
import functools

import jax
import jax.numpy as jnp
from jax import lax
from jax.experimental import pallas as pl
from jax.experimental.pallas import tpu as pltpu

F32 = jnp.float32
BF16 = jnp.bfloat16

D_MODEL = 1024
GRID_W = 64
RMS_EPS = 1e-6

ML_HEADS = 4
ML_HEAD_DIM = 256
ML_CHUNK = 256

NA_HEADS = 16
NA_HEAD_DIM = 64
NA_KH = 8
NA_KW = 16
NA_ROWS_PER_STEP = 64

GQ_HEADS = 8
GQ_HEAD_DIM = 128
GQ_KV_HEADS = 2
GQ_GROUP = GQ_HEADS // GQ_KV_HEADS
ROPE_THETA = 10000.0
ROPE_HALF = GQ_HEAD_DIM // 4
GQ_BLOCK_Q = 256
GQ_BLOCK_K = 512

LANES = 128
SUBLANES = 8
MASK_VALUE = -1e30
LOG2_E = 1.4426950408889634

VMEM_LIMIT = 56 * 1024 * 1024


def _params(*sem):
    return pltpu.CompilerParams(dimension_semantics=sem, vmem_limit_bytes=VMEM_LIMIT)


def _sigmoid(x):
    return 0.5 * jnp.tanh(0.5 * x) + 0.5


def _silu(x):
    return x * _sigmoid(x)


def _log_sigmoid(x):
    return jnp.minimum(x, 0.0) - jnp.log1p(jnp.exp(-jnp.abs(x)))


def _rms(x, eps=RMS_EPS):
    return x * lax.rsqrt(jnp.mean(x * x, axis=-1, keepdims=True) + eps)


def _dot(a, b):
    return jnp.dot(a, b, preferred_element_type=F32)


def _dot_nt(a, b):
    return lax.dot_general(a, b, (((1,), (1,)), ((), ())), preferred_element_type=F32)


def _dot_tn(a, b):
    return lax.dot_general(a, b, (((0,), (0,)), ((), ())), preferred_element_type=F32)


def _even_in_kernel(x_ref, g_ref, wa_ref, wb_ref, wg_ref, o_ref, og_ref, u_ref, *, n_a_tiles):
    j = pl.program_id(1)

    @pl.when(j == 0)
    def _():
        u = (_rms(x_ref[...]) * g_ref[...]).astype(BF16)
        u_ref[...] = u
        og_ref[...] = _dot(u, wg_ref[...])

    @pl.when(j < n_a_tiles)
    def _():
        o_ref[...] = _dot(u_ref[...], wa_ref[...]).astype(o_ref.dtype)

    @pl.when(j >= n_a_tiles)
    def _():
        o_ref[...] = _dot(u_ref[...], wb_ref[...]).astype(o_ref.dtype)


def _even_in_proj(x, g, wa, wb, wg, *, na, tm, tn):
    t, d = x.shape
    nb = wb.shape[1] // tn
    n = (na + nb) * tn
    kern = functools.partial(_even_in_kernel, n_a_tiles=na)
    return pl.pallas_call(
        kern,
        grid=(t // tm, na + nb),
        in_specs=[
            pl.BlockSpec((tm, d), lambda i, j: (i, 0)),
            pl.BlockSpec((1, d), lambda i, j: (0, 0)),
            pl.BlockSpec((d, tn), lambda i, j: (0, jnp.minimum(j, na - 1))),
            pl.BlockSpec((d, tn), lambda i, j: (0, jnp.maximum(j - na, 0))),
            pl.BlockSpec((d, LANES), lambda i, j: (0, 0)),
        ],
        out_specs=[
            pl.BlockSpec((tm, tn), lambda i, j: (i, j)),
            pl.BlockSpec((tm, LANES), lambda i, j: (i, 0)),
        ],
        out_shape=[
            jax.ShapeDtypeStruct((t, n), BF16),
            jax.ShapeDtypeStruct((t, LANES), F32),
        ],
        scratch_shapes=[pltpu.VMEM((tm, d), BF16)],
        compiler_params=_params("parallel", "arbitrary"),
        name="even_in_proj",
    )(x, g, wa, wb, wg)


def _rope_norm(acc, acc_sw, gain, gain_sw, cos, sin, scale):
    nh = acc.shape[1] // GQ_HEAD_DIM
    heads = []
    for h in range(nh):
        sl = slice(h * GQ_HEAD_DIM, (h + 1) * GQ_HEAD_DIM)
        a = acc[:, sl]
        r = lax.rsqrt(jnp.mean(a * a, axis=-1, keepdims=True) + RMS_EPS)
        y = a * r * gain[:, sl]
        partner = acc_sw[:, sl] * r * gain_sw[:, sl]
        heads.append((y * cos + partner * sin) * scale)
    return jnp.concatenate(heads, axis=-1)


def _odd_in_kernel(x_ref, g_ref, wa_ref, wsw_ref, wb_ref, qg_ref, qgs_ref, kg_ref, kgs_ref, cos_ref, sin_ref,
                   o_ref, u_ref, *, n_q_tiles, kv_tile):
    j = pl.program_id(1)

    @pl.when(j == 0)
    def _():
        u_ref[...] = (_rms(x_ref[...]) * g_ref[...]).astype(BF16)

    is_q = j < n_q_tiles
    is_kv = j == kv_tile

    @pl.when(is_q)
    def _():
        u = u_ref[...]
        o_ref[...] = _rope_norm(_dot(u, wa_ref[...]), _dot(u, wsw_ref[...]), qg_ref[...], qgs_ref[...],
                                cos_ref[...], sin_ref[...], GQ_HEAD_DIM ** -0.5 * LOG2_E).astype(o_ref.dtype)

    @pl.when(is_kv)
    def _():
        kw = GQ_KV_HEADS * GQ_HEAD_DIM
        u = u_ref[...]
        acc = _dot(u, wa_ref[...])
        k = _rope_norm(acc[:, :kw], _dot(u, wsw_ref[:, :kw]), kg_ref[...], kgs_ref[...],
                       cos_ref[...], sin_ref[...], 1.0)
        o_ref[...] = jnp.concatenate([k, acc[:, kw:]], axis=-1).astype(o_ref.dtype)

    @pl.when(jnp.logical_not(jnp.logical_or(is_q, is_kv)))
    def _():
        o_ref[...] = _dot(u_ref[...], wb_ref[...]).astype(o_ref.dtype)


def _odd_in_proj(x, g, wa, wsw, wb, qg, qgs, kg, kgs, cos, sin, *, plain0, n_plain, tm, tn, seq):
    t, d = x.shape
    n_q_tiles = (GQ_HEADS * GQ_HEAD_DIM) // tn
    kv_tile = n_q_tiles + n_plain
    n = (kv_tile + 1) * tn
    kvw = GQ_KV_HEADS * GQ_HEAD_DIM
    sb = seq // tm
    kern = functools.partial(_odd_in_kernel, n_q_tiles=n_q_tiles, kv_tile=kv_tile)

    def qkv_map(i, j):
        return (0, jnp.where(j < n_q_tiles, j, jnp.where(j == kv_tile, n_q_tiles, n_q_tiles - 1)))

    return pl.pallas_call(
        kern,
        grid=(t // tm, kv_tile + 1),
        in_specs=[
            pl.BlockSpec((tm, d), lambda i, j: (i, 0)),
            pl.BlockSpec((1, d), lambda i, j: (0, 0)),
            pl.BlockSpec((d, tn), qkv_map),
            pl.BlockSpec((d, tn), qkv_map),
            pl.BlockSpec((d, tn), lambda i, j: (0, plain0 + jnp.clip(j - n_q_tiles, 0, n_plain - 1))),
            pl.BlockSpec((1, tn), lambda i, j: (0, 0)),
            pl.BlockSpec((1, tn), lambda i, j: (0, 0)),
            pl.BlockSpec((1, kvw), lambda i, j: (0, 0)),
            pl.BlockSpec((1, kvw), lambda i, j: (0, 0)),
            pl.BlockSpec((tm, GQ_HEAD_DIM), lambda i, j: (i % sb, 0)),
            pl.BlockSpec((tm, GQ_HEAD_DIM), lambda i, j: (i % sb, 0)),
        ],
        out_specs=pl.BlockSpec((tm, tn), lambda i, j: (i, j)),
        out_shape=jax.ShapeDtypeStruct((t, n), BF16),
        scratch_shapes=[pltpu.VMEM((tm, d), BF16)],
        compiler_params=_params("parallel", "arbitrary"),
        name="odd_in_proj",
    )(x, g, wa, wsw, wb, qg, qgs, kg, kgs, cos, sin)


def _split3(a):
    hi = a.astype(BF16)
    r = a - hi.astype(F32)
    mid = r.astype(BF16)
    return hi, mid, (r - mid.astype(F32)).astype(BF16)


def _dot_exact_lhs(a, b):
    return sum(_dot(p, b) for p in _split3(a))


def _mlstm_kernel(qf_ref, kf_ref, vf_ref, gf_ref, qb_ref, kb_ref, vb_ref, gb_ref, bias_ref, sel_ref,
                  hf_ref, hb_ref, c_ref, n_ref, m_ref):
    nb, L = qf_ref.shape[:2]
    dh = ML_HEAD_DIM
    k_scale = dh ** -0.5

    @pl.when(pl.program_id(0) == 0)
    def _():
        c_ref[...] = jnp.zeros_like(c_ref)
        n_ref[...] = jnp.zeros_like(n_ref)
        m_ref[...] = jnp.zeros_like(m_ref)

    row = lax.broadcasted_iota(jnp.int32, (L, L), 0)
    col = lax.broadcasted_iota(jnp.int32, (L, L), 1)
    lower = row >= col
    upper = row <= col
    n_gate = 4 * ML_HEADS

    def wide(x):
        return jnp.concatenate([x] * (dh // LANES), axis=1)

    gate_row = lax.broadcasted_iota(jnp.int32, (n_gate, L), 0)
    forget_row = ((gate_row >= ML_HEADS) & (gate_row < 2 * ML_HEADS)) | (gate_row >= 3 * ML_HEADS)

    def gate_tables(g, reach, first_col):
        gt = (g + bias_ref[...]).T[:n_gate, :]
        lf = jnp.where(forget_row, _log_sigmoid(gt), 0.0)
        cum = _dot_exact_lhs(lf, reach.astype(BF16))
        tab_t = jnp.where(forget_row, cum, gt) * LOG2_E
        sel = sel_ref[:, first_col * LANES:(first_col + 2 * ML_HEADS) * LANES]
        rep = sum(_dot_tn(p, sel) for p in _split3(tab_t))
        return rep, tab_t

    def lanes(rep, j):
        return rep[:, j * LANES:(j + 1) * LANES]

    chains = []
    for b in range(nb):
        rep_f, tab_ft = gate_tables(gf_ref[b], upper, 0)
        rep_b, tab_bt = gate_tables(gb_ref[b], lower, 2 * ML_HEADS)
        fwd_refs = (qf_ref.at[b], kf_ref.at[b], vf_ref.at[b], hf_ref.at[b])
        bwd_refs = (qb_ref.at[b], kb_ref.at[b], vb_ref.at[b], hb_ref.at[b])
        for hd in range(ML_HEADS):
            base = 2 * ML_HEADS * b
            chains.append((base + hd, fwd_refs, rep_f, tab_ft, 0, L - 1, lower, hd))
            chains.append((base + ML_HEADS + hd, bwd_refs, rep_b, tab_bt, 2 * ML_HEADS, 0, upper, hd))

    stage1 = []
    for idx, (q_ref, k_ref, _, _), _, _, _, _, _, hd in chains:
        sl = slice(hd * dh, (hd + 1) * dh)
        q = q_ref[:, sl]
        qk = _dot_nt(q * jnp.asarray(k_scale, q.dtype), k_ref[:, sl])
        stage1.append((qk, _dot(q, c_ref[idx].astype(BF16))))

    for (idx, (q_ref, k_ref, v_ref, h_ref), rep, tab_t, go, g_row, valid, hd), (qk, qc) in zip(chains, stage1):
        sl = slice(hd * dh, (hd + 1) * dh)
        fi, ff = hd, ML_HEADS + hd
        q, k, v = q_ref[:, sl], k_ref[:, sl], v_ref[:, sl]
        b128, ig128 = lanes(rep, ff), lanes(rep, fi)
        brow, igrow = tab_t[go + ff:go + ff + 1, :], tab_t[go + fi:go + fi + 1, :]
        g128 = b128[g_row:g_row + 1, :]
        n = n_ref[idx][0:1, :]
        m_prev = m_ref[idx][0:1, :]
        d = jnp.where(valid, wide(b128) + (igrow - brow), MASK_VALUE)
        inter = b128 + m_prev
        m_t = jnp.maximum(inter, jnp.max(d, axis=-1, keepdims=True))
        w = jnp.exp2(d - wide(m_t)) * qk
        decay = jnp.exp2(inter - m_t)
        num = wide(decay) * qc + _dot(w.astype(BF16), v)
        qn = jnp.sum(q.astype(F32) * n, axis=-1, keepdims=True)
        den = decay * qn + jnp.sum(w, axis=-1, keepdims=True)
        h = num * wide(1.0 / jnp.maximum(jnp.abs(den), jnp.exp2(-m_t)))
        h_ref[:, sl] = h.astype(h_ref.dtype)
        a = g128 - b128 + ig128
        m_new = jnp.maximum(g128 + m_prev, jnp.max(a, axis=0, keepdims=True))
        carry = wide(jnp.exp2(g128 + m_prev - m_new))
        wk = wide(jnp.exp2(a - m_new) * k_scale)
        wkv = (wk * v.astype(F32)).astype(BF16)
        c_ref[idx] = carry * c_ref[idx] + _dot_tn(k, wkv)
        n_new = carry * n + jnp.sum(wk * k.astype(F32), axis=0, keepdims=True)
        n_ref[idx] = jnp.broadcast_to(n_new, n_ref.shape[1:])
        m_ref[idx] = jnp.broadcast_to(m_new, m_ref.shape[1:])


def _mlstm(p, gates, bias, *, batch, seq):
    t = p.shape[0]
    L = ML_CHUNK
    nc = seq // L
    w = ML_HEADS * ML_HEAD_DIM
    n_gate = 4 * ML_HEADS
    sel = (jnp.arange(n_gate)[:, None] == (jnp.arange(n_gate * LANES) // LANES)[None, :]).astype(BF16)
    p3 = p.reshape(batch, seq, p.shape[1])
    g3 = gates.reshape(batch, seq, LANES)

    def fwd(cb):
        return lambda c: (0, c, cb)

    def bwd(cb):
        return lambda c: (0, nc - 1 - c, cb)

    def blk(width, index_map):
        return pl.BlockSpec((batch, L, width), index_map)

    hf, hb = pl.pallas_call(
        _mlstm_kernel,
        grid=(nc,),
        in_specs=[
            blk(w, fwd(0)), blk(w, fwd(1)), blk(w, fwd(2)), blk(LANES, fwd(0)),
            blk(w, bwd(0)), blk(w, bwd(1)), blk(w, bwd(2)), blk(LANES, bwd(0)),
            pl.BlockSpec((1, LANES), lambda c: (0, 0)),
            pl.BlockSpec(sel.shape, lambda c: (0, 0)),
        ],
        out_specs=[blk(w, fwd(0)), blk(w, bwd(0))],
        out_shape=[jax.ShapeDtypeStruct((batch, seq, w), BF16), jax.ShapeDtypeStruct((batch, seq, w), BF16)],
        scratch_shapes=[
            pltpu.VMEM((batch * 2 * ML_HEADS, ML_HEAD_DIM, ML_HEAD_DIM), F32),
            pltpu.VMEM((batch * 2 * ML_HEADS, SUBLANES, ML_HEAD_DIM), F32),
            pltpu.VMEM((batch * 2 * ML_HEADS, SUBLANES, LANES), F32),
        ],
        compiler_params=_params("arbitrary"),
        name="mlstm",
    )(p3, p3, p3, g3, p3, p3, p3, g3, bias, sel)
    return hf.reshape(t, w), hb.reshape(t, w)


def _na_table_kernel(rp_ref, o_ref):
    c = lax.broadcasted_iota(jnp.int32, (GRID_W, LANES), 0)
    lane = lax.broadcasted_iota(jnp.int32, (GRID_W, LANES), 1)
    c0 = jnp.clip(c - NA_KW // 2, 0, GRID_W - NA_KW)
    low_half = lane < GRID_W
    kc = jnp.where(low_half, lane, lane - GRID_W)
    in_window = (kc >= c0) & (kc < c0 + NA_KW)
    lo, hi = [], []
    for dr in range(2 * NA_KH - 1):
        row = jnp.broadcast_to(rp_ref[0, dr:dr + 1, :], (GRID_W, LANES))
        lo.append(jnp.where(in_window & low_half, pltpu.roll(row, GRID_W + 1, 1, stride=1, stride_axis=0),
                            MASK_VALUE))
        hi.append(jnp.where(in_window & jnp.logical_not(low_half),
                            pltpu.roll(row, 1, 1, stride=1, stride_axis=0), MASK_VALUE))
    for rel in range(NA_KH):
        pairs = [jnp.maximum(lo[2 * j - rel + NA_KH - 1], hi[2 * j - rel + NA_KH]) for j in range(NA_KH // 2)]
        o_ref[0, rel] = jnp.concatenate(pairs, axis=1)


def _na_bias_table(rpb):
    nh, ndr, ndc = rpb.shape
    lead = GRID_W - NA_KW
    rp = jnp.pad(rpb.astype(F32) * LOG2_E, ((0, 0), (0, 2 * NA_KH - ndr), (lead, LANES - lead - ndc)))
    return pl.pallas_call(
        _na_table_kernel,
        grid=(nh,),
        in_specs=[pl.BlockSpec((1, 2 * NA_KH, LANES), lambda h: (h, 0, 0))],
        out_specs=pl.BlockSpec((1, NA_KH, GRID_W, NA_KH * GRID_W), lambda h: (h, 0, 0, 0)),
        out_shape=jax.ShapeDtypeStruct((nh, NA_KH, GRID_W, NA_KH * GRID_W), F32),
        compiler_params=_params("parallel"),
        name="na_bias_table",
    )(rp)


def _na_kernel(q_ref, k_ref, v_ref, z_ref, tbl_ref, o_ref, *, rows):
    rb = pl.program_id(2)
    win = NA_KH * GRID_W
    lane_q = lax.broadcasted_iota(jnp.int32, (GRID_W, LANES), 1)
    head0_q = lane_q < NA_HEAD_DIM
    starts, scores = [], []
    for i in range(NA_ROWS_PER_STEP):
        r = rb * NA_ROWS_PER_STEP + i
        r0 = jnp.clip(r - NA_KH // 2, 0, rows - NA_KH)
        rel = r - r0
        start = pl.multiple_of(r0 * GRID_W, GRID_W)
        kw = k_ref[pl.ds(start, win), :]
        q = q_ref[i * GRID_W:(i + 1) * GRID_W, :]
        zero = jnp.zeros_like(q)
        qs = jnp.concatenate([jnp.where(head0_q, q, zero), jnp.where(head0_q, zero, q)], axis=0)
        bias = jnp.concatenate([tbl_ref[0, rel], tbl_ref[1, rel]], axis=0)
        scores.append(_dot_nt(qs, kw) * (NA_HEAD_DIM ** -0.5 * LOG2_E) + bias)
        starts.append(start)
    for i in range(NA_ROWS_PER_STEP):
        s = scores[i]
        e = jnp.exp2(s - jnp.max(s, axis=-1, keepdims=True))
        inv = 1.0 / jnp.sum(e, axis=-1, keepdims=True)
        o = _dot(e.astype(BF16), v_ref[pl.ds(starts[i], win), :]) * inv
        out = jnp.where(head0_q, o[:GRID_W], o[GRID_W:])
        zz = z_ref[i * GRID_W:(i + 1) * GRID_W, :].astype(F32)
        o_ref[i * GRID_W:(i + 1) * GRID_W, :] = (out * _silu(zz)).astype(o_ref.dtype)


def _natten(p, tbl, *, batch, seq, col0):
    t = p.shape[0]
    rows = seq // GRID_W
    width = NA_HEADS * NA_HEAD_DIM
    pairs = width // LANES
    tq = NA_ROWS_PER_STEP * GRID_W
    nrb = seq // tq
    cb = col0 // LANES
    kern = functools.partial(_na_kernel, rows=rows)
    return pl.pallas_call(
        kern,
        grid=(batch, pairs, nrb),
        in_specs=[
            pl.BlockSpec((tq, LANES), lambda b, h, r: (b * nrb + r, cb + h)),
            pl.BlockSpec((seq, LANES), lambda b, h, r: (b, cb + pairs + h)),
            pl.BlockSpec((seq, LANES), lambda b, h, r: (b, cb + 2 * pairs + h)),
            pl.BlockSpec((tq, LANES), lambda b, h, r: (b * nrb + r, cb + 3 * pairs + h)),
            pl.BlockSpec((2, NA_KH, GRID_W, NA_KH * GRID_W), lambda b, h, r: (h, 0, 0, 0)),
        ],
        out_specs=pl.BlockSpec((tq, LANES), lambda b, h, r: (b * nrb + r, h)),
        out_shape=jax.ShapeDtypeStruct((t, width), BF16),
        compiler_params=_params("parallel", "parallel", "arbitrary"),
        name="natten",
    )(p, p, p, p, tbl)


def _even_out_kernel(hf_ref, hb_ref, o_ref, z_ref, na_ref, g_ref, w_ref, x_ref, out_ref):
    wa = ML_HEADS * ML_HEAD_DIM
    y = _dot(na_ref[...], w_ref[wa:, :])
    for i in range(ML_HEADS):
        sl = slice(i * ML_HEAD_DIM, (i + 1) * ML_HEAD_DIM)
        h = hf_ref[:, sl].astype(F32) + hb_ref[:, sl].astype(F32)
        a = _rms(h) * g_ref[:, sl] * _sigmoid(o_ref[:, sl].astype(F32)) * _silu(z_ref[:, sl].astype(F32))
        y = y + _dot(a.astype(BF16), w_ref[sl, :])
    out_ref[...] = x_ref[...] + y


def _even_out(hf, hb, p, na, g, w, x, *, tm):
    t, d = x.shape
    wa = hf.shape[1]
    return pl.pallas_call(
        _even_out_kernel,
        grid=(t // tm,),
        in_specs=[
            pl.BlockSpec((tm, wa), lambda i: (i, 0)),
            pl.BlockSpec((tm, wa), lambda i: (i, 0)),
            pl.BlockSpec((tm, wa), lambda i: (i, 3)),
            pl.BlockSpec((tm, wa), lambda i: (i, 4)),
            pl.BlockSpec((tm, na.shape[1]), lambda i: (i, 0)),
            pl.BlockSpec((1, wa), lambda i: (0, 0)),
            pl.BlockSpec(w.shape, lambda i: (0, 0)),
            pl.BlockSpec((tm, d), lambda i: (i, 0)),
        ],
        out_specs=pl.BlockSpec((tm, d), lambda i: (i, 0)),
        out_shape=jax.ShapeDtypeStruct((t, d), F32),
        compiler_params=_params("parallel"),
        name="even_out_proj",
    )(hf, hb, p, p, na, g, w, x)


def _gqa_kernel(q_ref, k_ref, v_ref, o_ref, vaug_ref, m_ref, acc_ref, *, tk):
    tq = q_ref.shape[0]
    seq = k_ref.shape[0]
    dh = GQ_HEAD_DIM

    @pl.when(pl.program_id(2) == 0)
    def _():
        vaug_ref[:, :dh] = v_ref[...]
        vaug_ref[:, dh:] = jnp.ones((seq, dh), BF16)

    q = q_ref[...]
    qs = jnp.concatenate([q[:, g * dh:(g + 1) * dh] for g in range(GQ_GROUP)], axis=0)
    m_ref[...] = jnp.full_like(m_ref, -jnp.inf)
    acc_ref[...] = jnp.zeros_like(acc_ref)

    def scores(c):
        return _dot_nt(qs, k_ref[c * tk:(c + 1) * tk, :])

    def update(s, c):
        m_prev = m_ref[...]
        m_new = jnp.maximum(m_prev, jnp.max(s, axis=-1, keepdims=True))
        alpha = jnp.exp2(m_prev - m_new)
        p = jnp.exp2(s - jnp.concatenate([m_new] * (tk // LANES), axis=1))
        pv = _dot(p.astype(BF16), vaug_ref[c * tk:(c + 1) * tk, :])
        acc_ref[...] = jnp.concatenate([alpha, alpha], axis=1) * acc_ref[...] + pv
        m_ref[...] = m_new

    nk = seq // tk
    s_next = scores(0)
    for c in range(nk):
        s_cur = s_next
        if c + 1 < nk:
            s_next = scores(c + 1)
        update(s_cur, c)
    acc = acc_ref[...]
    out = acc[:, :dh] / acc[:, dh:]
    o_ref[...] = jnp.concatenate(
        [out[g * tq:(g + 1) * tq] for g in range(GQ_GROUP)], axis=-1).astype(o_ref.dtype)


def _gqa(p, *, batch, seq, k_col, v_col):
    t = p.shape[0]
    tq, tk = GQ_BLOCK_Q, GQ_BLOCK_K
    nq = seq // tq
    gw = GQ_GROUP * GQ_HEAD_DIM
    kb, vb = k_col // GQ_HEAD_DIM, v_col // GQ_HEAD_DIM
    kern = functools.partial(_gqa_kernel, tk=tk)
    return pl.pallas_call(
        kern,
        grid=(batch, GQ_KV_HEADS, nq),
        in_specs=[
            pl.BlockSpec((tq, gw), lambda b, h, i: (b * nq + i, h)),
            pl.BlockSpec((seq, GQ_HEAD_DIM), lambda b, h, i: (b, kb + h)),
            pl.BlockSpec((seq, GQ_HEAD_DIM), lambda b, h, i: (b, vb + h)),
        ],
        out_specs=pl.BlockSpec((tq, gw), lambda b, h, i: (b * nq + i, h)),
        out_shape=jax.ShapeDtypeStruct((t, GQ_HEADS * GQ_HEAD_DIM), BF16),
        scratch_shapes=[
            pltpu.VMEM((seq, 2 * GQ_HEAD_DIM), BF16),
            pltpu.VMEM((GQ_GROUP * tq, LANES), F32),
            pltpu.VMEM((GQ_GROUP * tq, 2 * GQ_HEAD_DIM), F32),
        ],
        compiler_params=_params("parallel", "parallel", "arbitrary"),
        name="gqa_attention",
    )(p, p, p)


def _odd_out_kernel(att_ref, gz_ref, cb_ref, cc_ref, cx_ref, cz_ref, ccp_ref, cxp_ref, ccn_ref, cxn_ref,
                    cw_ref, cbias_ref, w_ref, h_ref, fg_ref, out_ref, *, blocks_per_seq):
    tm, wa = att_ref.shape
    slab = 2 * LANES
    i = pl.program_id(0) % blocks_per_seq
    ridx = lax.broadcasted_iota(jnp.int32, (tm, slab), 0)
    hh = h_ref[...]
    for s in range(wa // slab):
        sl = slice(s * slab, (s + 1) * slab)
        att = att_ref[:, sl].astype(F32) * _silu(gz_ref[:, sl].astype(F32))
        hh = hh + _dot(att.astype(BF16), w_ref[sl, :])
    for s in range(wa // slab):
        sl = slice(s * slab, (s + 1) * slab)
        xc = cc_ref[:, sl].astype(F32) * cx_ref[:, sl].astype(F32)
        prev_row = ccp_ref[SUBLANES - 1:SUBLANES, sl].astype(F32) * cxp_ref[SUBLANES - 1:SUBLANES, sl].astype(F32)
        next_row = ccn_ref[0:1, sl].astype(F32) * cxn_ref[0:1, sl].astype(F32)
        prev_row = jnp.where(i == 0, 0.0, prev_row)
        next_row = jnp.where(i == blocks_per_seq - 1, 0.0, next_row)
        x_prev = jnp.where(ridx == 0, prev_row, pltpu.roll(xc, 1, 0))
        x_next = jnp.where(ridx == tm - 1, next_row, pltpu.roll(xc, tm - 1, 0))
        y = x_prev * cw_ref[0:1, sl] + xc * cw_ref[1:2, sl] + x_next * cw_ref[2:3, sl] + cbias_ref[:, sl]
        conv = cb_ref[:, sl].astype(F32) * y * _silu(cz_ref[:, sl].astype(F32))
        hh = hh + _dot(conv.astype(BF16), w_ref[wa + s * slab:wa + (s + 1) * slab, :])
    out_ref[...] = _rms(hh) * fg_ref[...]


def _odd_out(att, p, cw, cbias, w, h, fg, *, tm, seq, cols):
    t, d = h.shape
    bps = seq // tm
    sub = tm // SUBLANES
    nsub = t // SUBLANES
    z_c, b_c, c_c, x_c, cz_c = cols

    def blk(cb):
        return pl.BlockSpec((tm, d), lambda i: (i, cb))

    def halo_prev(cb):
        return pl.BlockSpec((SUBLANES, d), lambda i: (jnp.maximum(i * sub - 1, 0), cb))

    def halo_next(cb):
        return pl.BlockSpec((SUBLANES, d), lambda i: (jnp.minimum((i + 1) * sub, nsub - 1), cb))

    kern = functools.partial(_odd_out_kernel, blocks_per_seq=bps)
    return pl.pallas_call(
        kern,
        grid=(t // tm,),
        in_specs=[
            pl.BlockSpec((tm, d), lambda i: (i, 0)),
            blk(z_c), blk(b_c), blk(c_c), blk(x_c), blk(cz_c),
            halo_prev(c_c), halo_prev(x_c), halo_next(c_c), halo_next(x_c),
            pl.BlockSpec(cw.shape, lambda i: (0, 0)),
            pl.BlockSpec((1, d), lambda i: (0, 0)),
            pl.BlockSpec(w.shape, lambda i: (0, 0)),
            pl.BlockSpec((tm, d), lambda i: (i, 0)),
            pl.BlockSpec((1, d), lambda i: (0, 0)),
        ],
        out_specs=pl.BlockSpec((tm, d), lambda i: (i, 0)),
        out_shape=jax.ShapeDtypeStruct((t, d), F32),
        compiler_params=_params("parallel"),
        name="odd_out_proj",
    )(att, p, p, p, p, p, p, p, p, p, cw, cbias, w, h, fg)


def _rope_tables(seq):
    t = jnp.arange(seq)
    pos = jnp.stack([t // GRID_W, t % GRID_W], axis=-1).astype(F32)
    inv = ROPE_THETA ** (-jnp.arange(ROPE_HALF, dtype=F32) / ROPE_HALF)
    ang = pos[:, :, None] * inv
    cos, sin = jnp.cos(ang), jnp.sin(ang)
    cos_l = jnp.stack([cos, cos], axis=1).reshape(seq, GQ_HEAD_DIM)
    sin_l = jnp.stack([-sin, sin], axis=1).reshape(seq, GQ_HEAD_DIM)
    return cos_l, sin_l


def _half_swap(a):
    lead = a.shape[:-1]
    a = a.reshape(*lead, -1, 2, GQ_HEAD_DIM // 2)
    return a[..., ::-1, :].reshape(*lead, -1)


def _rope_dim_order(a):
    lead = a.shape[:-1]
    a = a.reshape(*lead, -1, 2, 2, ROPE_HALF)
    return jnp.swapaxes(a, -3, -2).reshape(*lead, -1)


def kernel(x, norm_g, final_g, ev_w_in, ev_gate_b, ev_w_out, ev_ml_norm_g, ev_na_rpb,
           od_w_in, od_w_out, od_q_norm_g, od_k_norm_g, od_conv_w, od_conv_b):
    batch, seq, d = x.shape
    assert d == D_MODEL and norm_g.shape[0] == 2
    assert seq % max(ML_CHUNK, GQ_BLOCK_K, GQ_BLOCK_Q, NA_ROWS_PER_STEP * GRID_W, 512) == 0
    t = batch * seq
    xf = x.reshape(t, d)
    tm_in = min(2048, seq)
    tm_out = 512

    n_gate = 4 * ML_HEADS
    g0 = 5 * D_MODEL
    w_in = ev_w_in[0].astype(BF16)
    w_gate = jnp.pad(w_in[:, g0:g0 + n_gate], ((0, 0), (0, LANES - n_gate)))
    tn_even = 1024
    p_ev, gates = _even_in_proj(xf, norm_g[0][None, :], w_in, w_in[:, g0 + n_gate:], w_gate,
                                na=g0 // tn_even, tm=tm_in, tn=tn_even)
    gate_bias = jnp.pad(ev_gate_b[0], (0, LANES - n_gate))[None, :]
    hf, hb = _mlstm(p_ev, gates, gate_bias, batch=batch, seq=seq)
    na = _natten(p_ev, _na_bias_table(ev_na_rpb[0]), batch=batch, seq=seq, col0=g0)
    h1 = _even_out(hf, hb, p_ev, na, ev_ml_norm_g[0][None, :], ev_w_out[0].astype(BF16), xf, tm=tm_out)

    w_in = od_w_in[0]
    qw, kvw = GQ_HEADS * GQ_HEAD_DIM, GQ_KV_HEADS * GQ_HEAD_DIM
    tn_odd = 512
    wq, wk = _rope_dim_order(w_in[:, :qw]), _rope_dim_order(w_in[:, qw:qw + kvw])
    w_qkv = jnp.concatenate([wq, wk, w_in[:, qw + kvw:qw + 2 * kvw]], axis=1).astype(BF16)
    w_swap = jnp.concatenate([_half_swap(wq), _half_swap(wk), jnp.zeros((d, tn_odd - kvw), w_in.dtype)],
                             axis=1).astype(BF16)
    cos, sin = _rope_tables(seq)
    qg = jnp.tile(_rope_dim_order(od_q_norm_g[0]), tn_odd // GQ_HEAD_DIM)[None, :]
    kg = jnp.tile(_rope_dim_order(od_k_norm_g[0]), GQ_KV_HEADS)[None, :]
    plain0 = (qw + 2 * kvw) // tn_odd
    p_od = _odd_in_proj(h1, norm_g[1][None, :], w_qkv, w_swap, w_in.astype(BF16), qg, _half_swap(qg),
                        kg, _half_swap(kg), cos, sin, plain0=plain0, n_plain=w_in.shape[1] // tn_odd - plain0,
                        tm=tm_in, tn=tn_odd, seq=seq)
    k_col = p_od.shape[1] - 2 * kvw
    att = _gqa(p_od, batch=batch, seq=seq, k_col=k_col, v_col=k_col + kvw)
    out = _odd_out(att, p_od, od_conv_w[0], od_conv_b[0][None, :], od_w_out[0].astype(BF16), h1,
                   final_g[None, :], tm=tm_out, seq=seq, cols=(1, 2, 3, 4, 5))
    return out.reshape(batch, seq, d)
```

```python
import functools

import jax
import jax.numpy as jnp
from jax import lax
from jax.experimental import pallas as pl
from jax.experimental.pallas import tpu as pltpu

F32 = jnp.float32
BF16 = jnp.bfloat16

D_MODEL = 1024
GRID_W = 64
RMS_EPS = 1e-6

ML_HEADS = 4
ML_HEAD_DIM = 256
ML_CHUNK = 256

NA_HEADS = 16
NA_HEAD_DIM = 64
NA_KH = 8
NA_KW = 16
NA_ROWS_PER_STEP = 64

GQ_HEADS = 8
GQ_HEAD_DIM = 128
GQ_KV_HEADS = 2
GQ_GROUP = GQ_HEADS // GQ_KV_HEADS
ROPE_THETA = 10000.0
ROPE_HALF = GQ_HEAD_DIM // 4
GQ_BLOCK_Q = 256
GQ_BLOCK_K = 512

LANES = 128
SUBLANES = 8
MASK_VALUE = -1e30
LOG2_E = 1.4426950408889634

VMEM_LIMIT = 56 * 1024 * 1024


def _params(*sem):
    return pltpu.CompilerParams(dimension_semantics=sem, vmem_limit_bytes=VMEM_LIMIT)


def _sigmoid(x):
    return 0.5 * jnp.tanh(0.5 * x) + 0.5


def _silu(x):
    return x * _sigmoid(x)


def _log_sigmoid(x):
    return jnp.minimum(x, 0.0) - jnp.log1p(jnp.exp(-jnp.abs(x)))


def _rms(x, eps=RMS_EPS):
    return x * lax.rsqrt(jnp.mean(x * x, axis=-1, keepdims=True) + eps)


def _dot(a, b):
    return jnp.dot(a, b, preferred_element_type=F32)


def _dot_nt(a, b):
    return lax.dot_general(a, b, (((1,), (1,)), ((), ())), preferred_element_type=F32)


def _dot_tn(a, b):
    return lax.dot_general(a, b, (((0,), (0,)), ((), ())), preferred_element_type=F32)


def _even_in_kernel(x_ref, g_ref, wa_ref, wb_ref, wg_ref, o_ref, og_ref, u_ref, *, n_a_tiles):
    j = pl.program_id(1)

    @pl.when(j == 0)
    def _():
        u = (_rms(x_ref[...]) * g_ref[...]).astype(BF16)
        u_ref[...] = u
        og_ref[...] = _dot(u, wg_ref[...])

    @pl.when(j < n_a_tiles)
    def _():
        o_ref[...] = _dot(u_ref[...], wa_ref[...]).astype(o_ref.dtype)

    @pl.when(j >= n_a_tiles)
    def _():
        o_ref[...] = _dot(u_ref[...], wb_ref[...]).astype(o_ref.dtype)


def _even_in_proj(x, g, wa, wb, wg, *, na, tm, tn):
    t, d = x.shape
    nb = wb.shape[1] // tn
    n = (na + nb) * tn
    kern = functools.partial(_even_in_kernel, n_a_tiles=na)
    return pl.pallas_call(
        kern,
        grid=(t // tm, na + nb),
        in_specs=[
            pl.BlockSpec((tm, d), lambda i, j: (i, 0)),
            pl.BlockSpec((1, d), lambda i, j: (0, 0)),
            pl.BlockSpec((d, tn), lambda i, j: (0, jnp.minimum(j, na - 1))),
            pl.BlockSpec((d, tn), lambda i, j: (0, jnp.maximum(j - na, 0))),
            pl.BlockSpec((d, LANES), lambda i, j: (0, 0)),
        ],
        out_specs=[
            pl.BlockSpec((tm, tn), lambda i, j: (i, j)),
            pl.BlockSpec((tm, LANES), lambda i, j: (i, 0)),
        ],
        out_shape=[
            jax.ShapeDtypeStruct((t, n), BF16),
            jax.ShapeDtypeStruct((t, LANES), F32),
        ],
        scratch_shapes=[pltpu.VMEM((tm, d), BF16)],
        compiler_params=_params("parallel", "arbitrary"),
        name="even_in_proj",
    )(x, g, wa, wb, wg)


def _rope_norm(acc, acc_sw, gain, gain_sw, cos, sin, scale):
    nh = acc.shape[1] // GQ_HEAD_DIM
    heads = []
    for h in range(nh):
        sl = slice(h * GQ_HEAD_DIM, (h + 1) * GQ_HEAD_DIM)
        a = acc[:, sl]
        r = lax.rsqrt(jnp.mean(a * a, axis=-1, keepdims=True) + RMS_EPS)
        y = a * r * gain[:, sl]
        partner = acc_sw[:, sl] * r * gain_sw[:, sl]
        heads.append((y * cos + partner * sin) * scale)
    return jnp.concatenate(heads, axis=-1)


def _odd_in_kernel(x_ref, g_ref, wa_ref, wsw_ref, wb_ref, qg_ref, qgs_ref, kg_ref, kgs_ref, cos_ref, sin_ref,
                   o_ref, u_ref, *, n_q_tiles, kv_tile):
    j = pl.program_id(1)

    @pl.when(j == 0)
    def _():
        u_ref[...] = (_rms(x_ref[...]) * g_ref[...]).astype(BF16)

    is_q = j < n_q_tiles
    is_kv = j == kv_tile

    @pl.when(is_q)
    def _():
        u = u_ref[...]
        o_ref[...] = _rope_norm(_dot(u, wa_ref[...]), _dot(u, wsw_ref[...]), qg_ref[...], qgs_ref[...],
                                cos_ref[...], sin_ref[...], GQ_HEAD_DIM ** -0.5 * LOG2_E).astype(o_ref.dtype)

    @pl.when(is_kv)
    def _():
        kw = GQ_KV_HEADS * GQ_HEAD_DIM
        u = u_ref[...]
        acc = _dot(u, wa_ref[...])
        k = _rope_norm(acc[:, :kw], _dot(u, wsw_ref[:, :kw]), kg_ref[...], kgs_ref[...],
                       cos_ref[...], sin_ref[...], 1.0)
        o_ref[...] = jnp.concatenate([k, acc[:, kw:]], axis=-1).astype(o_ref.dtype)

    @pl.when(jnp.logical_not(jnp.logical_or(is_q, is_kv)))
    def _():
        o_ref[...] = _dot(u_ref[...], wb_ref[...]).astype(o_ref.dtype)


def _odd_in_proj(x, g, wa, wsw, wb, qg, qgs, kg, kgs, cos, sin, *, plain0, n_plain, tm, tn, seq):
    t, d = x.shape
    n_q_tiles = (GQ_HEADS * GQ_HEAD_DIM) // tn
    kv_tile = n_q_tiles + n_plain
    n = (kv_tile + 1) * tn
    kvw = GQ_KV_HEADS * GQ_HEAD_DIM
    sb = seq // tm
    kern = functools.partial(_odd_in_kernel, n_q_tiles=n_q_tiles, kv_tile=kv_tile)

    def qkv_map(i, j):
        return (0, jnp.where(j < n_q_tiles, j, jnp.where(j == kv_tile, n_q_tiles, n_q_tiles - 1)))

    return pl.pallas_call(
        kern,
        grid=(t // tm, kv_tile + 1),
        in_specs=[
            pl.BlockSpec((tm, d), lambda i, j: (i, 0)),
            pl.BlockSpec((1, d), lambda i, j: (0, 0)),
            pl.BlockSpec((d, tn), qkv_map),
            pl.BlockSpec((d, tn), qkv_map),
            pl.BlockSpec((d, tn), lambda i, j: (0, plain0 + jnp.clip(j - n_q_tiles, 0, n_plain - 1))),
            pl.BlockSpec((1, tn), lambda i, j: (0, 0)),
            pl.BlockSpec((1, tn), lambda i, j: (0, 0)),
            pl.BlockSpec((1, kvw), lambda i, j: (0, 0)),
            pl.BlockSpec((1, kvw), lambda i, j: (0, 0)),
            pl.BlockSpec((tm, GQ_HEAD_DIM), lambda i, j: (i % sb, 0)),
            pl.BlockSpec((tm, GQ_HEAD_DIM), lambda i, j: (i % sb, 0)),
        ],
        out_specs=pl.BlockSpec((tm, tn), lambda i, j: (i, j)),
        out_shape=jax.ShapeDtypeStruct((t, n), BF16),
        scratch_shapes=[pltpu.VMEM((tm, d), BF16)],
        compiler_params=_params("parallel", "arbitrary"),
        name="odd_in_proj",
    )(x, g, wa, wsw, wb, qg, qgs, kg, kgs, cos, sin)


def _split3(a):
    hi = a.astype(BF16)
    r = a - hi.astype(F32)
    mid = r.astype(BF16)
    return hi, mid, (r - mid.astype(F32)).astype(BF16)


def _dot_exact_lhs(a, b):
    return sum(_dot(p, b) for p in _split3(a))


def _mlstm_kernel(qf_ref, kf_ref, vf_ref, gf_ref, qb_ref, kb_ref, vb_ref, gb_ref, bias_ref, sel_ref,
                  hf_ref, hb_ref, c_ref, n_ref, m_ref):
    nb, L = qf_ref.shape[:2]
    dh = ML_HEAD_DIM
    k_scale = dh ** -0.5

    @pl.when(pl.program_id(0) == 0)
    def _():
        c_ref[...] = jnp.zeros_like(c_ref)
        n_ref[...] = jnp.zeros_like(n_ref)
        m_ref[...] = jnp.zeros_like(m_ref)

    row = lax.broadcasted_iota(jnp.int32, (L, L), 0)
    col = lax.broadcasted_iota(jnp.int32, (L, L), 1)
    lower = row >= col
    upper = row <= col
    n_gate = 4 * ML_HEADS

    def wide(x):
        return jnp.concatenate([x] * (dh // LANES), axis=1)

    gate_row = lax.broadcasted_iota(jnp.int32, (n_gate, L), 0)
    forget_row = ((gate_row >= ML_HEADS) & (gate_row < 2 * ML_HEADS)) | (gate_row >= 3 * ML_HEADS)

    def gate_tables(g, reach, first_col):
        gt = (g + bias_ref[...]).T[:n_gate, :]
        lf = jnp.where(forget_row, _log_sigmoid(gt), 0.0)
        cum = _dot_exact_lhs(lf, reach.astype(BF16))
        tab_t = jnp.where(forget_row, cum, gt) * LOG2_E
        sel = sel_ref[:, first_col * LANES:(first_col + 2 * ML_HEADS) * LANES]
        rep = sum(_dot_tn(p, sel) for p in _split3(tab_t))
        return rep, tab_t

    def lanes(rep, j):
        return rep[:, j * LANES:(j + 1) * LANES]

    chains = []
    for b in range(nb):
        rep_f, tab_ft = gate_tables(gf_ref[b], upper, 0)
        rep_b, tab_bt = gate_tables(gb_ref[b], lower, 2 * ML_HEADS)
        fwd_refs = (qf_ref.at[b], kf_ref.at[b], vf_ref.at[b], hf_ref.at[b])
        bwd_refs = (qb_ref.at[b], kb_ref.at[b], vb_ref.at[b], hb_ref.at[b])
        for hd in range(ML_HEADS):
            base = 2 * ML_HEADS * b
            chains.append((base + hd, fwd_refs, rep_f, tab_ft, 0, L - 1, lower, hd))
            chains.append((base + ML_HEADS + hd, bwd_refs, rep_b, tab_bt, 2 * ML_HEADS, 0, upper, hd))

    stage1 = []
    for idx, (q_ref, k_ref, _, _), _, _, _, _, _, hd in chains:
        sl = slice(hd * dh, (hd + 1) * dh)
        q = q_ref[:, sl]
        qk = _dot_nt(q * jnp.asarray(k_scale, q.dtype), k_ref[:, sl])
        stage1.append((qk, _dot(q, c_ref[idx].astype(BF16))))

    for (idx, (q_ref, k_ref, v_ref, h_ref), rep, tab_t, go, g_row, valid, hd), (qk, qc) in zip(chains, stage1):
        sl = slice(hd * dh, (hd + 1) * dh)
        fi, ff = hd, ML_HEADS + hd
        q, k, v = q_ref[:, sl], k_ref[:, sl], v_ref[:, sl]
        b128, ig128 = lanes(rep, ff), lanes(rep, fi)
        brow, igrow = tab_t[go + ff:go + ff + 1, :], tab_t[go + fi:go + fi + 1, :]
        g128 = b128[g_row:g_row + 1, :]
        n = n_ref[idx][0:1, :]
        m_prev = m_ref[idx][0:1, :]
        d = jnp.where(valid, wide(b128) + (igrow - brow), MASK_VALUE)
        inter = b128 + m_prev
        m_t = jnp.maximum(inter, jnp.max(d, axis=-1, keepdims=True))
        w = jnp.exp2(d - wide(m_t)) * qk
        decay = jnp.exp2(inter - m_t)
        num = wide(decay) * qc + _dot(w.astype(BF16), v)
        qn = jnp.sum(q.astype(F32) * n, axis=-1, keepdims=True)
        den = decay * qn + jnp.sum(w, axis=-1, keepdims=True)
        h = num * wide(1.0 / jnp.maximum(jnp.abs(den), jnp.exp2(-m_t)))
        h_ref[:, sl] = h.astype(h_ref.dtype)
        a = g128 - b128 + ig128
        m_new = jnp.maximum(g128 + m_prev, jnp.max(a, axis=0, keepdims=True))
        carry = wide(jnp.exp2(g128 + m_prev - m_new))
        wk = wide(jnp.exp2(a - m_new) * k_scale)
        wkv = (wk * v.astype(F32)).astype(BF16)
        c_ref[idx] = carry * c_ref[idx] + _dot_tn(k, wkv)
        n_new = carry * n + jnp.sum(wk * k.astype(F32), axis=0, keepdims=True)
        n_ref[idx] = jnp.broadcast_to(n_new, n_ref.shape[1:])
        m_ref[idx] = jnp.broadcast_to(m_new, m_ref.shape[1:])


def _mlstm(p, gates, bias, *, batch, seq):
    t = p.shape[0]
    L = ML_CHUNK
    nc = seq // L
    w = ML_HEADS * ML_HEAD_DIM
    n_gate = 4 * ML_HEADS
    sel = (jnp.arange(n_gate)[:, None] == (jnp.arange(n_gate * LANES) // LANES)[None, :]).astype(BF16)
    p3 = p.reshape(batch, seq, p.shape[1])
    g3 = gates.reshape(batch, seq, LANES)

    def fwd(cb):
        return lambda c: (0, c, cb)

    def bwd(cb):
        return lambda c: (0, nc - 1 - c, cb)

    def blk(width, index_map):
        return pl.BlockSpec((batch, L, width), index_map)

    hf, hb = pl.pallas_call(
        _mlstm_kernel,
        grid=(nc,),
        in_specs=[
            blk(w, fwd(0)), blk(w, fwd(1)), blk(w, fwd(2)), blk(LANES, fwd(0)),
            blk(w, bwd(0)), blk(w, bwd(1)), blk(w, bwd(2)), blk(LANES, bwd(0)),
            pl.BlockSpec((1, LANES), lambda c: (0, 0)),
            pl.BlockSpec(sel.shape, lambda c: (0, 0)),
        ],
        out_specs=[blk(w, fwd(0)), blk(w, bwd(0))],
        out_shape=[jax.ShapeDtypeStruct((batch, seq, w), BF16), jax.ShapeDtypeStruct((batch, seq, w), BF16)],
        scratch_shapes=[
            pltpu.VMEM((batch * 2 * ML_HEADS, ML_HEAD_DIM, ML_HEAD_DIM), F32),
            pltpu.VMEM((batch * 2 * ML_HEADS, SUBLANES, ML_HEAD_DIM), F32),
            pltpu.VMEM((batch * 2 * ML_HEADS, SUBLANES, LANES), F32),
        ],
        compiler_params=_params("arbitrary"),
        name="mlstm",
    )(p3, p3, p3, g3, p3, p3, p3, g3, bias, sel)
    return hf.reshape(t, w), hb.reshape(t, w)


def _na_table_kernel(rp_ref, o_ref):
    c = lax.broadcasted_iota(jnp.int32, (GRID_W, LANES), 0)
    lane = lax.broadcasted_iota(jnp.int32, (GRID_W, LANES), 1)
    c0 = jnp.clip(c - NA_KW // 2, 0, GRID_W - NA_KW)
    low_half = lane < GRID_W
    kc = jnp.where(low_half, lane, lane - GRID_W)
    in_window = (kc >= c0) & (kc < c0 + NA_KW)
    lo, hi = [], []
    for dr in range(2 * NA_KH - 1):
        row = jnp.broadcast_to(rp_ref[0, dr:dr + 1, :], (GRID_W, LANES))
        lo.append(jnp.where(in_window & low_half, pltpu.roll(row, GRID_W + 1, 1, stride=1, stride_axis=0),
                            MASK_VALUE))
        hi.append(jnp.where(in_window & jnp.logical_not(low_half),
                            pltpu.roll(row, 1, 1, stride=1, stride_axis=0), MASK_VALUE))
    for rel in range(NA_KH):
        pairs = [jnp.maximum(lo[2 * j - rel + NA_KH - 1], hi[2 * j - rel + NA_KH]) for j in range(NA_KH // 2)]
        o_ref[0, rel] = jnp.concatenate(pairs, axis=1)


def _na_bias_table(rpb):
    nh, ndr, ndc = rpb.shape
    lead = GRID_W - NA_KW
    rp = jnp.pad(rpb.astype(F32) * LOG2_E, ((0, 0), (0, 2 * NA_KH - ndr), (lead, LANES - lead - ndc)))
    return pl.pallas_call(
        _na_table_kernel,
        grid=(nh,),
        in_specs=[pl.BlockSpec((1, 2 * NA_KH, LANES), lambda h: (h, 0, 0))],
        out_specs=pl.BlockSpec((1, NA_KH, GRID_W, NA_KH * GRID_W), lambda h: (h, 0, 0, 0)),
        out_shape=jax.ShapeDtypeStruct((nh, NA_KH, GRID_W, NA_KH * GRID_W), F32),
        compiler_params=_params("parallel"),
        name="na_bias_table",
    )(rp)


def _na_kernel(q_ref, k_ref, v_ref, z_ref, tbl_ref, o_ref, *, rows):
    rb = pl.program_id(2)
    win = NA_KH * GRID_W
    lane_q = lax.broadcasted_iota(jnp.int32, (GRID_W, LANES), 1)
    head0_q = lane_q < NA_HEAD_DIM
    starts, scores = [], []
    for i in range(NA_ROWS_PER_STEP):
        r = rb * NA_ROWS_PER_STEP + i
        r0 = jnp.clip(r - NA_KH // 2, 0, rows - NA_KH)
        rel = r - r0
        start = pl.multiple_of(r0 * GRID_W, GRID_W)
        kw = k_ref[pl.ds(start, win), :]
        q = q_ref[i * GRID_W:(i + 1) * GRID_W, :]
        zero = jnp.zeros_like(q)
        qs = jnp.concatenate([jnp.where(head0_q, q, zero), jnp.where(head0_q, zero, q)], axis=0)
        bias = jnp.concatenate([tbl_ref[0, rel], tbl_ref[1, rel]], axis=0)
        scores.append(_dot_nt(qs, kw) * (NA_HEAD_DIM ** -0.5 * LOG2_E) + bias)
        starts.append(start)
    for i in range(NA_ROWS_PER_STEP):
        s = scores[i]
        e = jnp.exp2(s - jnp.max(s, axis=-1, keepdims=True))
        inv = 1.0 / jnp.sum(e, axis=-1, keepdims=True)
        o = _dot(e.astype(BF16), v_ref[pl.ds(starts[i], win), :]) * inv
        out = jnp.where(head0_q, o[:GRID_W], o[GRID_W:])
        zz = z_ref[i * GRID_W:(i + 1) * GRID_W, :].astype(F32)
        o_ref[i * GRID_W:(i + 1) * GRID_W, :] = (out * _silu(zz)).astype(o_ref.dtype)


def _natten(p, tbl, *, batch, seq, col0):
    t = p.shape[0]
    rows = seq // GRID_W
    width = NA_HEADS * NA_HEAD_DIM
    pairs = width // LANES
    tq = NA_ROWS_PER_STEP * GRID_W
    nrb = seq // tq
    cb = col0 // LANES
    kern = functools.partial(_na_kernel, rows=rows)
    return pl.pallas_call(
        kern,
        grid=(batch, pairs, nrb),
        in_specs=[
            pl.BlockSpec((tq, LANES), lambda b, h, r: (b * nrb + r, cb + h)),
            pl.BlockSpec((seq, LANES), lambda b, h, r: (b, cb + pairs + h)),
            pl.BlockSpec((seq, LANES), lambda b, h, r: (b, cb + 2 * pairs + h)),
            pl.BlockSpec((tq, LANES), lambda b, h, r: (b * nrb + r, cb + 3 * pairs + h)),
            pl.BlockSpec((2, NA_KH, GRID_W, NA_KH * GRID_W), lambda b, h, r: (h, 0, 0, 0)),
        ],
        out_specs=pl.BlockSpec((tq, LANES), lambda b, h, r: (b * nrb + r, h)),
        out_shape=jax.ShapeDtypeStruct((t, width), BF16),
        compiler_params=_params("parallel", "parallel", "arbitrary"),
        name="natten",
    )(p, p, p, p, tbl)


def _even_out_kernel(hf_ref, hb_ref, o_ref, z_ref, na_ref, g_ref, w_ref, x_ref, out_ref):
    wa = ML_HEADS * ML_HEAD_DIM
    y = _dot(na_ref[...], w_ref[wa:, :])
    for i in range(ML_HEADS):
        sl = slice(i * ML_HEAD_DIM, (i + 1) * ML_HEAD_DIM)
        h = hf_ref[:, sl].astype(F32) + hb_ref[:, sl].astype(F32)
        a = _rms(h) * g_ref[:, sl] * _sigmoid(o_ref[:, sl].astype(F32)) * _silu(z_ref[:, sl].astype(F32))
        y = y + _dot(a.astype(BF16), w_ref[sl, :])
    out_ref[...] = x_ref[...] + y


def _even_out(hf, hb, p, na, g, w, x, *, tm):
    t, d = x.shape
    wa = hf.shape[1]
    return pl.pallas_call(
        _even_out_kernel,
        grid=(t // tm,),
        in_specs=[
            pl.BlockSpec((tm, wa), lambda i: (i, 0)),
            pl.BlockSpec((tm, wa), lambda i: (i, 0)),
            pl.BlockSpec((tm, wa), lambda i: (i, 3)),
            pl.BlockSpec((tm, wa), lambda i: (i, 4)),
            pl.BlockSpec((tm, na.shape[1]), lambda i: (i, 0)),
            pl.BlockSpec((1, wa), lambda i: (0, 0)),
            pl.BlockSpec(w.shape, lambda i: (0, 0), pipeline_mode=pl.Buffered(1)),
            pl.BlockSpec((tm, d), lambda i: (i, 0)),
        ],
        out_specs=pl.BlockSpec((tm, d), lambda i: (i, 0)),
        out_shape=jax.ShapeDtypeStruct((t, d), F32),
        compiler_params=_params("parallel"),
        name="even_out_proj",
    )(hf, hb, p, p, na, g, w, x)


def _gqa_kernel(q_ref, k_ref, v_ref, o_ref, vaug_ref, m_ref, acc_ref, *, tk):
    tq = q_ref.shape[0]
    seq = k_ref.shape[0]
    dh = GQ_HEAD_DIM

    @pl.when(pl.program_id(2) == 0)
    def _():
        vaug_ref[:, :dh] = v_ref[...]
        vaug_ref[:, dh:] = jnp.ones((seq, dh), BF16)

    q = q_ref[...]
    qs = jnp.concatenate([q[:, g * dh:(g + 1) * dh] for g in range(GQ_GROUP)], axis=0)
    m_ref[...] = jnp.full_like(m_ref, -jnp.inf)
    acc_ref[...] = jnp.zeros_like(acc_ref)

    def scores(c):
        return _dot_nt(qs, k_ref[c * tk:(c + 1) * tk, :])

    def update(s, c):
        m_prev = m_ref[...]
        m_new = jnp.maximum(m_prev, jnp.max(s, axis=-1, keepdims=True))
        alpha = jnp.exp2(m_prev - m_new)
        p = jnp.exp2(s - jnp.concatenate([m_new] * (tk // LANES), axis=1))
        pv = _dot(p.astype(BF16), vaug_ref[c * tk:(c + 1) * tk, :])
        acc_ref[...] = jnp.concatenate([alpha, alpha], axis=1) * acc_ref[...] + pv
        m_ref[...] = m_new

    nk = seq // tk
    s_next = scores(0)
    for c in range(nk):
        s_cur = s_next
        if c + 1 < nk:
            s_next = scores(c + 1)
        update(s_cur, c)
    acc = acc_ref[...]
    out = acc[:, :dh] / acc[:, dh:]
    o_ref[...] = jnp.concatenate(
        [out[g * tq:(g + 1) * tq] for g in range(GQ_GROUP)], axis=-1).astype(o_ref.dtype)


def _gqa(p, *, batch, seq, k_col, v_col):
    t = p.shape[0]
    tq, tk = GQ_BLOCK_Q, GQ_BLOCK_K
    nq = seq // tq
    gw = GQ_GROUP * GQ_HEAD_DIM
    kb, vb = k_col // GQ_HEAD_DIM, v_col // GQ_HEAD_DIM
    kern = functools.partial(_gqa_kernel, tk=tk)
    return pl.pallas_call(
        kern,
        grid=(batch, GQ_KV_HEADS, nq),
        in_specs=[
            pl.BlockSpec((tq, gw), lambda b, h, i: (b * nq + i, h)),
            pl.BlockSpec((seq, GQ_HEAD_DIM), lambda b, h, i: (b, kb + h)),
            pl.BlockSpec((seq, GQ_HEAD_DIM), lambda b, h, i: (b, vb + h)),
        ],
        out_specs=pl.BlockSpec((tq, gw), lambda b, h, i: (b * nq + i, h)),
        out_shape=jax.ShapeDtypeStruct((t, GQ_HEADS * GQ_HEAD_DIM), BF16),
        scratch_shapes=[
            pltpu.VMEM((seq, 2 * GQ_HEAD_DIM), BF16),
            pltpu.VMEM((GQ_GROUP * tq, LANES), F32),
            pltpu.VMEM((GQ_GROUP * tq, 2 * GQ_HEAD_DIM), F32),
        ],
        compiler_params=_params("parallel", "parallel", "arbitrary"),
        name="gqa_attention",
    )(p, p, p)


def _odd_out_kernel(att_ref, gz_ref, cb_ref, cc_ref, cx_ref, cz_ref, ccp_ref, cxp_ref, ccn_ref, cxn_ref,
                    cw_ref, cbias_ref, w_ref, h_ref, fg_ref, out_ref, *, blocks_per_seq):
    tm, wa = att_ref.shape
    slab = 2 * LANES
    i = pl.program_id(0) % blocks_per_seq
    ridx = lax.broadcasted_iota(jnp.int32, (tm, slab), 0)
    hh = h_ref[...]
    for s in range(wa // slab):
        sl = slice(s * slab, (s + 1) * slab)
        att = att_ref[:, sl].astype(F32) * _silu(gz_ref[:, sl].astype(F32))
        hh = hh + _dot(att.astype(BF16), w_ref[sl, :])
    for s in range(wa // slab):
        sl = slice(s * slab, (s + 1) * slab)
        xc = cc_ref[:, sl].astype(F32) * cx_ref[:, sl].astype(F32)
        prev_row = ccp_ref[SUBLANES - 1:SUBLANES, sl].astype(F32) * cxp_ref[SUBLANES - 1:SUBLANES, sl].astype(F32)
        next_row = ccn_ref[0:1, sl].astype(F32) * cxn_ref[0:1, sl].astype(F32)
        prev_row = jnp.where(i == 0, 0.0, prev_row)
        next_row = jnp.where(i == blocks_per_seq - 1, 0.0, next_row)
        x_prev = jnp.where(ridx == 0, prev_row, pltpu.roll(xc, 1, 0))
        x_next = jnp.where(ridx == tm - 1, next_row, pltpu.roll(xc, tm - 1, 0))
        y = x_prev * cw_ref[0:1, sl] + xc * cw_ref[1:2, sl] + x_next * cw_ref[2:3, sl] + cbias_ref[:, sl]
        conv = cb_ref[:, sl].astype(F32) * y * _silu(cz_ref[:, sl].astype(F32))
        hh = hh + _dot(conv.astype(BF16), w_ref[wa + s * slab:wa + (s + 1) * slab, :])
    out_ref[...] = _rms(hh) * fg_ref[...]


def _odd_out(att, p, cw, cbias, w, h, fg, *, tm, seq, cols):
    t, d = h.shape
    bps = seq // tm
    sub = tm // SUBLANES
    nsub = t // SUBLANES
    z_c, b_c, c_c, x_c, cz_c = cols

    def blk(cb):
        return pl.BlockSpec((tm, d), lambda i: (i, cb))

    def halo_prev(cb):
        return pl.BlockSpec((SUBLANES, d), lambda i: (jnp.maximum(i * sub - 1, 0), cb))

    def halo_next(cb):
        return pl.BlockSpec((SUBLANES, d), lambda i: (jnp.minimum((i + 1) * sub, nsub - 1), cb))

    kern = functools.partial(_odd_out_kernel, blocks_per_seq=bps)
    return pl.pallas_call(
        kern,
        grid=(t // tm,),
        in_specs=[
            pl.BlockSpec((tm, d), lambda i: (i, 0)),
            blk(z_c), blk(b_c), blk(c_c), blk(x_c), blk(cz_c),
            halo_prev(c_c), halo_prev(x_c), halo_next(c_c), halo_next(x_c),
            pl.BlockSpec(cw.shape, lambda i: (0, 0)),
            pl.BlockSpec((1, d), lambda i: (0, 0)),
            pl.BlockSpec(w.shape, lambda i: (0, 0), pipeline_mode=pl.Buffered(1)),
            pl.BlockSpec((tm, d), lambda i: (i, 0)),
            pl.BlockSpec((1, d), lambda i: (0, 0)),
        ],
        out_specs=pl.BlockSpec((tm, d), lambda i: (i, 0)),
        out_shape=jax.ShapeDtypeStruct((t, d), F32),
        compiler_params=_params("parallel"),
        name="odd_out_proj",
    )(att, p, p, p, p, p, p, p, p, p, cw, cbias, w, h, fg)


def _rope_tables(seq):
    t = jnp.arange(seq)
    pos = jnp.stack([t // GRID_W, t % GRID_W], axis=-1).astype(F32)
    inv = ROPE_THETA ** (-jnp.arange(ROPE_HALF, dtype=F32) / ROPE_HALF)
    ang = pos[:, :, None] * inv
    cos, sin = jnp.cos(ang), jnp.sin(ang)
    cos_l = jnp.stack([cos, cos], axis=1).reshape(seq, GQ_HEAD_DIM)
    sin_l = jnp.stack([-sin, sin], axis=1).reshape(seq, GQ_HEAD_DIM)
    return cos_l, sin_l


def _half_swap(a):
    lead = a.shape[:-1]
    a = a.reshape(*lead, -1, 2, GQ_HEAD_DIM // 2)
    return a[..., ::-1, :].reshape(*lead, -1)


def _rope_dim_order(a):
    lead = a.shape[:-1]
    a = a.reshape(*lead, -1, 2, 2, ROPE_HALF)
    return jnp.swapaxes(a, -3, -2).reshape(*lead, -1)


def kernel(x, norm_g, final_g, ev_w_in, ev_gate_b, ev_w_out, ev_ml_norm_g, ev_na_rpb,
           od_w_in, od_w_out, od_q_norm_g, od_k_norm_g, od_conv_w, od_conv_b):
    batch, seq, d = x.shape
    assert d == D_MODEL and norm_g.shape[0] == 2
    assert seq % max(ML_CHUNK, GQ_BLOCK_K, GQ_BLOCK_Q, NA_ROWS_PER_STEP * GRID_W, 512) == 0
    t = batch * seq
    xf = x.reshape(t, d)
    tm_in = min(2048, seq)
    tm_out = min(1024, seq)

    n_gate = 4 * ML_HEADS
    g0 = 5 * D_MODEL
    w_in = ev_w_in[0].astype(BF16)
    w_gate = jnp.pad(w_in[:, g0:g0 + n_gate], ((0, 0), (0, LANES - n_gate)))
    tn_even = 1024
    p_ev, gates = _even_in_proj(xf, norm_g[0][None, :], w_in, w_in[:, g0 + n_gate:], w_gate,
                                na=g0 // tn_even, tm=tm_in, tn=tn_even)
    gate_bias = jnp.pad(ev_gate_b[0], (0, LANES - n_gate))[None, :]
    hf, hb = _mlstm(p_ev, gates, gate_bias, batch=batch, seq=seq)
    na = _natten(p_ev, _na_bias_table(ev_na_rpb[0]), batch=batch, seq=seq, col0=g0)
    h1 = _even_out(hf, hb, p_ev, na, ev_ml_norm_g[0][None, :], ev_w_out[0].astype(BF16), xf, tm=tm_out)

    w_in = od_w_in[0]
    qw, kvw = GQ_HEADS * GQ_HEAD_DIM, GQ_KV_HEADS * GQ_HEAD_DIM
    tn_odd = 512
    wq, wk = _rope_dim_order(w_in[:, :qw]), _rope_dim_order(w_in[:, qw:qw + kvw])
    w_qkv = jnp.concatenate([wq, wk, w_in[:, qw + kvw:qw + 2 * kvw]], axis=1).astype(BF16)
    w_swap = jnp.concatenate([_half_swap(wq), _half_swap(wk), jnp.zeros((d, tn_odd - kvw), w_in.dtype)],
                             axis=1).astype(BF16)
    cos, sin = _rope_tables(seq)
    qg = jnp.tile(_rope_dim_order(od_q_norm_g[0]), tn_odd // GQ_HEAD_DIM)[None, :]
    kg = jnp.tile(_rope_dim_order(od_k_norm_g[0]), GQ_KV_HEADS)[None, :]
    plain0 = (qw + 2 * kvw) // tn_odd
    p_od = _odd_in_proj(h1, norm_g[1][None, :], w_qkv, w_swap, w_in.astype(BF16), qg, _half_swap(qg),
                        kg, _half_swap(kg), cos, sin, plain0=plain0, n_plain=w_in.shape[1] // tn_odd - plain0,
                        tm=tm_in, tn=tn_odd, seq=seq)
    k_col = p_od.shape[1] - 2 * kvw
    att = _gqa(p_od, batch=batch, seq=seq, k_col=k_col, v_col=k_col + kvw)
    out = _odd_out(att, p_od, od_conv_w[0], od_conv_b[0][None, :], od_w_out[0].astype(BF16), h1,
                   final_g[None, :], tm=tm_out, seq=seq, cols=(1, 2, 3, 4, 5))
    return out.reshape(batch, seq, d)
```
